```python
import functools
import jax, jax.numpy as jnp
from jax import lax
import numpy as np


D_MODEL = 2048
BATCH = 8
SEQ = 8192
DEPTH = 4

N_EVEN = (DEPTH + 1) // 2
N_ODD = DEPTH // 2

D_FF = 4096

GDN_HEADS = 8
GDN_DK = 128
GDN_DV = 128
GDN_QK_W = GDN_HEADS * GDN_DK
GDN_V_W = GDN_HEADS * GDN_DV
CONV_K = 4
CHUNK = 64
POOL_WINDOWS = (2, 4, 8, 16)
POOL_GROUPS = 4
POOL_W = D_MODEL // 2
POOL_GROUP_W = POOL_W // POOL_GROUPS
EVEN_IN = 2 * GDN_QK_W + 2 * GDN_V_W + 2 * GDN_HEADS + POOL_W
EVEN_MIX = GDN_V_W + POOL_W

MLA_HEADS = 16
Q_LORA = 512
KV_LORA = 512
NOPE = 128
ROPE = 64
V_HEAD = 128
QK_HEAD = NOPE + ROPE
ODD_IN = Q_LORA + KV_LORA + ROPE
ROPE_THETA = 10000.0
Q_BLOCK = 128

EPS = 1e-6

kernel_name = 'hybrid_gdn_pool_mla_macaron'


def rms_norm(x, gain):
    xf = x.astype(jnp.float32)
    y = xf * lax.rsqrt(jnp.mean(xf * xf, axis=-1, keepdims=True) + EPS)
    return (y * gain.astype(jnp.float32)).astype(x.dtype)


def l2_norm(x):
    xf = x.astype(jnp.float32)
    return xf * lax.rsqrt(jnp.sum(xf * xf, axis=-1, keepdims=True) + EPS)


def swiglu(h, w_gate, w_up, w_down):
    return (jax.nn.silu(h @ w_gate) * (h @ w_up)) @ w_down


def causal_dwconv(x, w):
    c = x.shape[-1]
    return lax.conv_general_dilated(x, w[:, None, :].astype(x.dtype), window_strides=(1,),
                                    padding=[(CONV_K - 1, 0)],
                                    dimension_numbers=('NWC', 'WIO', 'NWC'),
                                    feature_group_count=c)


def gated_delta_rule(q, k, v, g, beta):
    B, S, H, DK = q.shape
    N = S // CHUNK

    def chunks(t):
        return t.reshape(B, N, CHUNK, H, -1).transpose(0, 3, 1, 2, 4)

    q = chunks(q) * DK ** -0.5
    k = chunks(k)
    v = chunks(v)
    g = chunks(g[..., None])[..., 0]
    beta = chunks(beta[..., None])[..., 0]
    gc = jnp.cumsum(g, axis=-1)
    idx = jnp.arange(CHUNK)
    causal = idx[:, None] >= idx[None, :]
    strict = idx[:, None] > idx[None, :]
    decay = jnp.exp(jnp.where(causal, gc[..., :, None] - gc[..., None, :], -jnp.inf))
    kb = k * beta[..., None]
    vb = v * beta[..., None]
    m = jnp.where(strict, jnp.einsum('bhnid,bhnjd->bhnij', kb, k) * decay, 0.0)
    tri = m + jnp.eye(CHUNK, dtype=m.dtype)
    solve = functools.partial(lax.linalg.triangular_solve, left_side=True, lower=True,
                              unit_diagonal=True)
    u = solve(tri, vb)
    w = solve(tri, kb * jnp.exp(gc)[..., None])
    attn = jnp.einsum('bhnid,bhnjd->bhnij', q, k) * decay
    g_last = gc[..., -1]
    q_dec = q * jnp.exp(gc)[..., None]
    k_dec = k * jnp.exp(g_last[..., None] - gc)[..., None]

    def step(state, xs):
        qd, kd, uc, wc, ac, gl = xs
        v_new = uc - jnp.einsum('bhck,bhkv->bhcv', wc, state)
        out = jnp.einsum('bhck,bhkv->bhcv', qd, state) + jnp.einsum('bhij,bhjv->bhiv', ac, v_new)
        state = state * jnp.exp(gl)[..., None, None] + jnp.einsum('bhck,bhcv->bhkv', kd, v_new)
        return state, out

    xs = (jnp.moveaxis(q_dec, 2, 0), jnp.moveaxis(k_dec, 2, 0), jnp.moveaxis(u, 2, 0),
          jnp.moveaxis(w, 2, 0), jnp.moveaxis(attn, 2, 0), jnp.moveaxis(g_last, 2, 0))
    state0 = jnp.zeros((B, H, DK, v.shape[-1]), jnp.float32)
    _, o = lax.scan(step, state0, xs)
    return o.transpose(1, 0, 3, 2, 4).reshape(B, S, H, -1)


def multiscale_pool(u, pool_w, pool_scale):
    B, S, C = u.shape
    ug = u.astype(jnp.float32).reshape(B, S, POOL_GROUPS, POOL_GROUP_W)
    cs = jnp.cumsum(ug, axis=1)
    t = jnp.arange(S)
    pooled = []
    for gi, win in enumerate(POOL_WINDOWS):
        c = cs[:, :, gi]
        prev = jnp.pad(c, ((0, 0), (win, 0), (0, 0)))[:, :S]
        cnt = jnp.minimum(t + 1, win).astype(jnp.float32)[None, :, None]
        pooled.append((c - prev) / cnt)
    diff = (jnp.stack(pooled, axis=2) - ug).astype(u.dtype)
    y = jnp.einsum('bsgc,gcd->bsgd', diff, pool_w)
    return y.reshape(B, S, C) * pool_scale


def delta_pool_mixer(h, w_in, conv_w, a_log, dt_bias, out_norm, pool_w, pool_scale, w_out):
    B, S, _ = h.shape
    proj = h @ w_in
    o1 = 2 * GDN_QK_W + GDN_V_W
    o2 = o1 + GDN_V_W
    o3 = o2 + GDN_HEADS
    o4 = o3 + GDN_HEADS
    qkv, z, a, b, u = jnp.split(proj, [o1, o2, o3, o4], axis=-1)
    qkv = jax.nn.silu(causal_dwconv(qkv, conv_w))
    q, k, v = jnp.split(qkv, [GDN_QK_W, 2 * GDN_QK_W], axis=-1)
    q = l2_norm(q.reshape(B, S, GDN_HEADS, GDN_DK))
    k = l2_norm(k.reshape(B, S, GDN_HEADS, GDN_DK))
    v = v.reshape(B, S, GDN_HEADS, GDN_DV).astype(jnp.float32)
    beta = jax.nn.sigmoid(b.astype(jnp.float32))
    g = -jnp.exp(a_log.astype(jnp.float32)) * jax.nn.softplus(a.astype(jnp.float32) + dt_bias.astype(jnp.float32))
    o = gated_delta_rule(q, k, v, g, beta)
    zf = z.reshape(B, S, GDN_HEADS, GDN_DV).astype(jnp.float32)
    o = (rms_norm(o, out_norm) * jax.nn.silu(zf)).reshape(B, S, GDN_V_W).astype(h.dtype)
    p = multiscale_pool(u, pool_w, pool_scale)
    return jnp.concatenate([o, p], axis=-1) @ w_out


def seg_head_norm(t, gain):
    tf = t.astype(jnp.float32)
    nope, pe = tf[..., :NOPE], tf[..., NOPE:]
    nope = nope * lax.rsqrt(jnp.mean(nope * nope, axis=-1, keepdims=True) + EPS)
    pe = pe * lax.rsqrt(jnp.mean(pe * pe, axis=-1, keepdims=True) + EPS)
    return (jnp.concatenate([nope, pe], axis=-1) * gain.astype(jnp.float32)).astype(t.dtype)


def apply_rope_tail(t, cos, sin):
    nope, pe = t[..., :NOPE], t[..., NOPE:].astype(jnp.float32)
    x1, x2 = pe[..., :ROPE // 2], pe[..., ROPE // 2:]
    rot = jnp.concatenate([x1 * cos - x2 * sin, x2 * cos + x1 * sin], axis=-1)
    return jnp.concatenate([nope, rot.astype(t.dtype)], axis=-1)


def causal_block_attention(q, k, v):
    B, S, H, Dqk = q.shape
    nb = S // Q_BLOCK
    qb = q.reshape(B, nb, Q_BLOCK, H, Dqk).transpose(1, 0, 2, 3, 4)
    starts = jnp.arange(nb, dtype=jnp.int32) * Q_BLOCK
    kpos = jnp.arange(S, dtype=jnp.int32)
    scale = Dqk ** -0.5
    neg = jnp.finfo(jnp.float32).min

    def one_block(args):
        qi, s0 = args
        s = jnp.einsum('bqhd,bkhd->bhqk', qi, k, preferred_element_type=jnp.float32) * scale
        qpos = s0 + jnp.arange(Q_BLOCK, dtype=jnp.int32)
        s = jnp.where(qpos[:, None] >= kpos[None, :], s, neg)
        p = jax.nn.softmax(s, axis=-1)
        return jnp.einsum('bhqk,bkhd->bqhd', p.astype(v.dtype), v)

    o = lax.map(one_block, (qb, starts))
    return o.transpose(1, 0, 2, 3, 4).reshape(B, S, H, v.shape[-1])


def mla_mixer(h, positions, w_in, q_norm, kv_norm, w_q_up, w_kv_up, q_head_norm, k_head_norm, w_out):
    B, S, _ = h.shape
    proj = h @ w_in
    q_lat, kv_lat, k_pe = jnp.split(proj, [Q_LORA, Q_LORA + KV_LORA], axis=-1)
    q = (rms_norm(q_lat, q_norm) @ w_q_up).reshape(B, S, MLA_HEADS, QK_HEAD)
    kv = (rms_norm(kv_lat, kv_norm) @ w_kv_up).reshape(B, S, MLA_HEADS, NOPE + V_HEAD)
    k_nope, v = kv[..., :NOPE], kv[..., NOPE:]
    k_pe = jnp.broadcast_to(k_pe[:, :, None, :], (B, S, MLA_HEADS, ROPE))
    k = jnp.concatenate([k_nope, k_pe], axis=-1)
    q = seg_head_norm(q, q_head_norm)
    k = seg_head_norm(k, k_head_norm)
    inv_freq = ROPE_THETA ** (-jnp.arange(0, ROPE, 2, dtype=jnp.float32) / ROPE)
    ang = positions.astype(jnp.float32)[..., None] * inv_freq
    cos, sin = jnp.cos(ang)[:, :, None, :], jnp.sin(ang)[:, :, None, :]
    q = apply_rope_tail(q, cos, sin)
    k = apply_rope_tail(k, cos, sin)
    o = causal_block_attention(q, k, v)
    return o.reshape(B, S, MLA_HEADS * V_HEAD) @ w_out


def _fwd_setup_inputs(seed: int = 0) -> dict:
    key = jax.random.key(seed)
    ks = jax.random.split(key, 32)

    def dense(k, shape, fan_in):
        return jax.random.normal(k, shape, jnp.float32) * fan_in ** -0.5

    def gain(k, shape):
        return 1.0 + 0.02 * jax.random.normal(k, shape, jnp.float32)

    x = jax.random.normal(ks[0], (BATCH, SEQ, D_MODEL), jnp.float32)
    positions = jnp.broadcast_to(jnp.arange(SEQ, dtype=jnp.int32)[None, :], (BATCH, SEQ))
    dt = jnp.exp(jax.random.uniform(ks[14], (N_EVEN, GDN_HEADS), jnp.float32,
                                    np.log(1e-3), np.log(1e-1)))
    return {
        'x': x,
        'positions': positions,
        'ffn1_norm': gain(ks[1], (DEPTH, D_MODEL)),
        'ffn1_w_gate': dense(ks[2], (DEPTH, D_MODEL, D_FF), D_MODEL),
        'ffn1_w_up': dense(ks[3], (DEPTH, D_MODEL, D_FF), D_MODEL),
        'ffn1_w_down': dense(ks[4], (DEPTH, D_FF, D_MODEL), D_FF),
        'mix_norm': gain(ks[5], (DEPTH, D_MODEL)),
        'ffn2_norm': gain(ks[6], (DEPTH, D_MODEL)),
        'ffn2_w_gate': dense(ks[7], (DEPTH, D_MODEL, D_FF), D_MODEL),
        'ffn2_w_up': dense(ks[8], (DEPTH, D_MODEL, D_FF), D_MODEL),
        'ffn2_w_down': dense(ks[9], (DEPTH, D_FF, D_MODEL), D_FF),
        'hyb_w_in': dense(ks[10], (N_EVEN, D_MODEL, EVEN_IN), D_MODEL),
        'gdn_conv': dense(ks[11], (N_EVEN, CONV_K, 2 * GDN_QK_W + GDN_V_W), CONV_K),
        'gdn_a_log': jnp.log(jax.random.uniform(ks[12], (N_EVEN, GDN_HEADS), jnp.float32, 1.0, 16.0)),
        'gdn_dt_bias': dt + jnp.log(-jnp.expm1(-dt)),
        'gdn_out_norm': gain(ks[13], (N_EVEN, GDN_DV)),
        'pool_w': dense(ks[15], (N_EVEN, POOL_GROUPS, POOL_GROUP_W, POOL_GROUP_W), POOL_GROUP_W),
        'pool_scale': gain(ks[16], (N_EVEN, POOL_W)),
        'hyb_w_out': dense(ks[17], (N_EVEN, EVEN_MIX, D_MODEL), EVEN_MIX),
        'mla_w_in': dense(ks[18], (N_ODD, D_MODEL, ODD_IN), D_MODEL),
        'mla_q_norm': gain(ks[19], (N_ODD, Q_LORA)),
        'mla_kv_norm': gain(ks[20], (N_ODD, KV_LORA)),
        'mla_w_q_up': dense(ks[21], (N_ODD, Q_LORA, MLA_HEADS * QK_HEAD), Q_LORA),
        'mla_w_kv_up': dense(ks[22], (N_ODD, KV_LORA, MLA_HEADS * (NOPE + V_HEAD)), KV_LORA),
        'mla_q_head_norm': gain(ks[23], (N_ODD, QK_HEAD)),
        'mla_k_head_norm': gain(ks[24], (N_ODD, QK_HEAD)),
        'mla_w_out': dense(ks[25], (N_ODD, MLA_HEADS * V_HEAD, D_MODEL), MLA_HEADS * V_HEAD),
    }


def _fwd_reference(x, positions, ffn1_norm, ffn1_w_gate, ffn1_w_up, ffn1_w_down, mix_norm,
              ffn2_norm, ffn2_w_gate, ffn2_w_up, ffn2_w_down,
              hyb_w_in, gdn_conv, gdn_a_log, gdn_dt_bias, gdn_out_norm, pool_w, pool_scale, hyb_w_out,
              mla_w_in, mla_q_norm, mla_kv_norm, mla_w_q_up, mla_w_kv_up,
              mla_q_head_norm, mla_k_head_norm, mla_w_out):
    for layer in range(DEPTH):
        x = x + 0.5 * swiglu(rms_norm(x, ffn1_norm[layer]), ffn1_w_gate[layer], ffn1_w_up[layer], ffn1_w_down[layer])
        h = rms_norm(x, mix_norm[layer])
        i = layer // 2
        if layer % 2 == 0:
            x = x + delta_pool_mixer(h, hyb_w_in[i], gdn_conv[i], gdn_a_log[i], gdn_dt_bias[i],
                                     gdn_out_norm[i], pool_w[i], pool_scale[i], hyb_w_out[i])
        else:
            x = x + mla_mixer(h, positions, mla_w_in[i], mla_q_norm[i], mla_kv_norm[i], mla_w_q_up[i],
                              mla_w_kv_up[i], mla_q_head_norm[i], mla_k_head_norm[i], mla_w_out[i])
        x = x + 0.5 * swiglu(rms_norm(x, ffn2_norm[layer]), ffn2_w_gate[layer], ffn2_w_up[layer], ffn2_w_down[layer])
    return x


import jax as _jax
import jax.numpy as _jnp

TWIN_FORMAT = 'train_step'
FWD_PARAMS = ['x', 'positions', 'ffn1_norm', 'ffn1_w_gate', 'ffn1_w_up', 'ffn1_w_down', 'mix_norm', 'ffn2_norm', 'ffn2_w_gate', 'ffn2_w_up', 'ffn2_w_down', 'hyb_w_in', 'gdn_conv', 'gdn_a_log', 'gdn_dt_bias', 'gdn_out_norm', 'pool_w', 'pool_scale', 'hyb_w_out', 'mla_w_in', 'mla_q_norm', 'mla_kv_norm', 'mla_w_q_up', 'mla_w_kv_up', 'mla_q_head_norm', 'mla_k_head_norm', 'mla_w_out']
TWIN_WEIGHTS = ['ffn1_norm', 'ffn1_w_gate', 'ffn1_w_up', 'ffn1_w_down', 'mix_norm', 'ffn2_norm', 'ffn2_w_gate', 'ffn2_w_up', 'ffn2_w_down', 'hyb_w_in', 'gdn_conv', 'gdn_a_log', 'gdn_dt_bias', 'gdn_out_norm', 'pool_w', 'pool_scale', 'hyb_w_out', 'mla_w_in', 'mla_q_norm', 'mla_kv_norm', 'mla_w_q_up', 'mla_w_kv_up', 'mla_q_head_norm', 'mla_k_head_norm', 'mla_w_out']
TWIN_DIFF_INPUT = 'x'
TWIN_INPUTS = ['x', 'positions', 'ffn1_norm', 'ffn1_w_gate', 'ffn1_w_up', 'ffn1_w_down', 'mix_norm', 'ffn2_norm', 'ffn2_w_gate', 'ffn2_w_up', 'ffn2_w_down', 'hyb_w_in', 'gdn_conv', 'gdn_a_log', 'gdn_dt_bias', 'gdn_out_norm', 'pool_w', 'pool_scale', 'hyb_w_out', 'mla_w_in', 'mla_q_norm', 'mla_kv_norm', 'mla_w_q_up', 'mla_w_kv_up', 'mla_q_head_norm', 'mla_k_head_norm', 'mla_w_out', 'loss_target', 'm_ffn1_norm', 'm_ffn1_w_gate', 'm_ffn1_w_up', 'm_ffn1_w_down', 'm_mix_norm', 'm_ffn2_norm', 'm_ffn2_w_gate', 'm_ffn2_w_up', 'm_ffn2_w_down', 'm_hyb_w_in', 'm_gdn_conv', 'm_gdn_a_log', 'm_gdn_dt_bias', 'm_gdn_out_norm', 'm_pool_w', 'm_pool_scale', 'm_hyb_w_out', 'm_mla_w_in', 'm_mla_q_norm', 'm_mla_kv_norm', 'm_mla_w_q_up', 'm_mla_w_kv_up', 'm_mla_q_head_norm', 'm_mla_k_head_norm', 'm_mla_w_out', 'v_ffn1_norm', 'v_ffn1_w_gate', 'v_ffn1_w_up', 'v_ffn1_w_down', 'v_mix_norm', 'v_ffn2_norm', 'v_ffn2_w_gate', 'v_ffn2_w_up', 'v_ffn2_w_down', 'v_hyb_w_in', 'v_gdn_conv', 'v_gdn_a_log', 'v_gdn_dt_bias', 'v_gdn_out_norm', 'v_pool_w', 'v_pool_scale', 'v_hyb_w_out', 'v_mla_w_in', 'v_mla_q_norm', 'v_mla_kv_norm', 'v_mla_w_q_up', 'v_mla_w_kv_up', 'v_mla_q_head_norm', 'v_mla_k_head_norm', 'v_mla_w_out']
TWIN_OUTPUTS = ['loss', 'grad_x', 'grad_ffn1_norm', 'grad_ffn1_w_gate', 'grad_ffn1_w_up', 'grad_ffn1_w_down', 'grad_mix_norm', 'grad_ffn2_norm', 'grad_ffn2_w_gate', 'grad_ffn2_w_up', 'grad_ffn2_w_down', 'grad_hyb_w_in', 'grad_gdn_conv', 'grad_gdn_a_log', 'grad_gdn_dt_bias', 'grad_gdn_out_norm', 'grad_pool_w', 'grad_pool_scale', 'grad_hyb_w_out', 'grad_mla_w_in', 'grad_mla_q_norm', 'grad_mla_kv_norm', 'grad_mla_w_q_up', 'grad_mla_w_kv_up', 'grad_mla_q_head_norm', 'grad_mla_k_head_norm', 'grad_mla_w_out', 'delta_ffn1_norm', 'delta_ffn1_w_gate', 'delta_ffn1_w_up', 'delta_ffn1_w_down', 'delta_mix_norm', 'delta_ffn2_norm', 'delta_ffn2_w_gate', 'delta_ffn2_w_up', 'delta_ffn2_w_down', 'delta_hyb_w_in', 'delta_gdn_conv', 'delta_gdn_a_log', 'delta_gdn_dt_bias', 'delta_gdn_out_norm', 'delta_pool_w', 'delta_pool_scale', 'delta_hyb_w_out', 'delta_mla_w_in', 'delta_mla_q_norm', 'delta_mla_kv_norm', 'delta_mla_w_q_up', 'delta_mla_w_kv_up', 'delta_mla_q_head_norm', 'delta_mla_k_head_norm', 'delta_mla_w_out', 'new_m_ffn1_norm', 'new_m_ffn1_w_gate', 'new_m_ffn1_w_up', 'new_m_ffn1_w_down', 'new_m_mix_norm', 'new_m_ffn2_norm', 'new_m_ffn2_w_gate', 'new_m_ffn2_w_up', 'new_m_ffn2_w_down', 'new_m_hyb_w_in', 'new_m_gdn_conv', 'new_m_gdn_a_log', 'new_m_gdn_dt_bias', 'new_m_gdn_out_norm', 'new_m_pool_w', 'new_m_pool_scale', 'new_m_hyb_w_out', 'new_m_mla_w_in', 'new_m_mla_q_norm', 'new_m_mla_kv_norm', 'new_m_mla_w_q_up', 'new_m_mla_w_kv_up', 'new_m_mla_q_head_norm', 'new_m_mla_k_head_norm', 'new_m_mla_w_out', 'new_v_ffn1_norm', 'new_v_ffn1_w_gate', 'new_v_ffn1_w_up', 'new_v_ffn1_w_down', 'new_v_mix_norm', 'new_v_ffn2_norm', 'new_v_ffn2_w_gate', 'new_v_ffn2_w_up', 'new_v_ffn2_w_down', 'new_v_hyb_w_in', 'new_v_gdn_conv', 'new_v_gdn_a_log', 'new_v_gdn_dt_bias', 'new_v_gdn_out_norm', 'new_v_pool_w', 'new_v_pool_scale', 'new_v_hyb_w_out', 'new_v_mla_w_in', 'new_v_mla_q_norm', 'new_v_mla_kv_norm', 'new_v_mla_w_q_up', 'new_v_mla_w_kv_up', 'new_v_mla_q_head_norm', 'new_v_mla_k_head_norm', 'new_v_mla_w_out']
TWIN_LEAF_KINDS = {'loss': 'loss', 'grad_x': 'grad_x', 'grad_ffn1_norm': 'grad_w', 'grad_ffn1_w_gate': 'grad_w', 'grad_ffn1_w_up': 'grad_w', 'grad_ffn1_w_down': 'grad_w', 'grad_mix_norm': 'grad_w', 'grad_ffn2_norm': 'grad_w', 'grad_ffn2_w_gate': 'grad_w', 'grad_ffn2_w_up': 'grad_w', 'grad_ffn2_w_down': 'grad_w', 'grad_hyb_w_in': 'grad_w', 'grad_gdn_conv': 'grad_w', 'grad_gdn_a_log': 'grad_w', 'grad_gdn_dt_bias': 'grad_w', 'grad_gdn_out_norm': 'grad_w', 'grad_pool_w': 'grad_w', 'grad_pool_scale': 'grad_w', 'grad_hyb_w_out': 'grad_w', 'grad_mla_w_in': 'grad_w', 'grad_mla_q_norm': 'grad_w', 'grad_mla_kv_norm': 'grad_w', 'grad_mla_w_q_up': 'grad_w', 'grad_mla_w_kv_up': 'grad_w', 'grad_mla_q_head_norm': 'grad_w', 'grad_mla_k_head_norm': 'grad_w', 'grad_mla_w_out': 'grad_w', 'delta_ffn1_norm': 'delta_w', 'delta_ffn1_w_gate': 'delta_w', 'delta_ffn1_w_up': 'delta_w', 'delta_ffn1_w_down': 'delta_w', 'delta_mix_norm': 'delta_w', 'delta_ffn2_norm': 'delta_w', 'delta_ffn2_w_gate': 'delta_w', 'delta_ffn2_w_up': 'delta_w', 'delta_ffn2_w_down': 'delta_w', 'delta_hyb_w_in': 'delta_w', 'delta_gdn_conv': 'delta_w', 'delta_gdn_a_log': 'delta_w', 'delta_gdn_dt_bias': 'delta_w', 'delta_gdn_out_norm': 'delta_w', 'delta_pool_w': 'delta_w', 'delta_pool_scale': 'delta_w', 'delta_hyb_w_out': 'delta_w', 'delta_mla_w_in': 'delta_w', 'delta_mla_q_norm': 'delta_w', 'delta_mla_kv_norm': 'delta_w', 'delta_mla_w_q_up': 'delta_w', 'delta_mla_w_kv_up': 'delta_w', 'delta_mla_q_head_norm': 'delta_w', 'delta_mla_k_head_norm': 'delta_w', 'delta_mla_w_out': 'delta_w', 'new_m_ffn1_norm': 'new_m', 'new_m_ffn1_w_gate': 'new_m', 'new_m_ffn1_w_up': 'new_m', 'new_m_ffn1_w_down': 'new_m', 'new_m_mix_norm': 'new_m', 'new_m_ffn2_norm': 'new_m', 'new_m_ffn2_w_gate': 'new_m', 'new_m_ffn2_w_up': 'new_m', 'new_m_ffn2_w_down': 'new_m', 'new_m_hyb_w_in': 'new_m', 'new_m_gdn_conv': 'new_m', 'new_m_gdn_a_log': 'new_m', 'new_m_gdn_dt_bias': 'new_m', 'new_m_gdn_out_norm': 'new_m', 'new_m_pool_w': 'new_m', 'new_m_pool_scale': 'new_m', 'new_m_hyb_w_out': 'new_m', 'new_m_mla_w_in': 'new_m', 'new_m_mla_q_norm': 'new_m', 'new_m_mla_kv_norm': 'new_m', 'new_m_mla_w_q_up': 'new_m', 'new_m_mla_w_kv_up': 'new_m', 'new_m_mla_q_head_norm': 'new_m', 'new_m_mla_k_head_norm': 'new_m', 'new_m_mla_w_out': 'new_m', 'new_v_ffn1_norm': 'new_v', 'new_v_ffn1_w_gate': 'new_v', 'new_v_ffn1_w_up': 'new_v', 'new_v_ffn1_w_down': 'new_v', 'new_v_mix_norm': 'new_v', 'new_v_ffn2_norm': 'new_v', 'new_v_ffn2_w_gate': 'new_v', 'new_v_ffn2_w_up': 'new_v', 'new_v_ffn2_w_down': 'new_v', 'new_v_hyb_w_in': 'new_v', 'new_v_gdn_conv': 'new_v', 'new_v_gdn_a_log': 'new_v', 'new_v_gdn_dt_bias': 'new_v', 'new_v_gdn_out_norm': 'new_v', 'new_v_pool_w': 'new_v', 'new_v_pool_scale': 'new_v', 'new_v_hyb_w_out': 'new_v', 'new_v_mla_w_in': 'new_v', 'new_v_mla_q_norm': 'new_v', 'new_v_mla_kv_norm': 'new_v', 'new_v_mla_w_q_up': 'new_v', 'new_v_mla_w_kv_up': 'new_v', 'new_v_mla_q_head_norm': 'new_v', 'new_v_mla_k_head_norm': 'new_v', 'new_v_mla_w_out': 'new_v'}


def _forward(args):
    return _fwd_reference(*[args[k] for k in FWD_PARAMS])


def _output_shape():
    def fwd():
        inp = _fwd_setup_inputs(0)
        return _fwd_reference(*[inp[k] for k in FWD_PARAMS])
    out = _jax.eval_shape(fwd)
    return out.shape, out.dtype

N_MICROBATCH = 1
ADAM_LR = 0.001
ADAM_B1 = 0.9
ADAM_B2 = 0.999
ADAM_EPS = 1e-08
ADAM_WD = 0.01
ADAM_STEP = 10
PER_EXAMPLE_BATCH_AXIS = {'x': 0, 'positions': 0, 'loss_target': 0}
SHARED_INPUTS = []
_WEIGHT_DTYPES = {'ffn1_norm': _jnp.float32, 'ffn1_w_gate': _jnp.float32, 'ffn1_w_up': _jnp.float32, 'ffn1_w_down': _jnp.float32, 'mix_norm': _jnp.float32, 'ffn2_norm': _jnp.float32, 'ffn2_w_gate': _jnp.float32, 'ffn2_w_up': _jnp.float32, 'ffn2_w_down': _jnp.float32, 'hyb_w_in': _jnp.float32, 'gdn_conv': _jnp.float32, 'gdn_a_log': _jnp.float32, 'gdn_dt_bias': _jnp.float32, 'gdn_out_norm': _jnp.float32, 'pool_w': _jnp.float32, 'pool_scale': _jnp.float32, 'hyb_w_out': _jnp.float32, 'mla_w_in': _jnp.float32, 'mla_q_norm': _jnp.float32, 'mla_kv_norm': _jnp.float32, 'mla_w_q_up': _jnp.float32, 'mla_w_kv_up': _jnp.float32, 'mla_q_head_norm': _jnp.float32, 'mla_k_head_norm': _jnp.float32, 'mla_w_out': _jnp.float32}
MOMENT_SCALE = {'ffn1_norm': 6.107592e+00, 'ffn1_w_gate': 1.116006e-01, 'ffn1_w_up': 1.201211e-01, 'ffn1_w_down': 1.691058e-01, 'mix_norm': 1.366729e+01, 'ffn2_norm': 6.107553e+00, 'ffn2_w_gate': 1.011689e-01, 'ffn2_w_up': 1.101522e-01, 'ffn2_w_down': 1.532276e-01, 'hyb_w_in': 5.156651e-01, 'gdn_conv': 6.257011e-01, 'gdn_a_log': 3.320361e+01, 'gdn_dt_bias': 3.177926e+01, 'gdn_out_norm': 9.422884e+01, 'pool_w': 1.652188e+00, 'pool_scale': 2.512553e+01, 'hyb_w_out': 1.218455e+00, 'mla_w_in': 7.047304e-01, 'mla_q_norm': 1.377181e-01, 'mla_kv_norm': 1.404283e+00, 'mla_w_q_up': 5.877228e-02, 'mla_w_kv_up': 3.187217e-01, 'mla_q_head_norm': 1.614217e+00, 'mla_k_head_norm': 1.611908e+00, 'mla_w_out': 4.588535e-01}


def _to_microbatches(a, axis):
    t = _jnp.moveaxis(a, axis, 0)
    t = t.reshape((N_MICROBATCH, t.shape[0] // N_MICROBATCH) + t.shape[1:])
    return _jnp.moveaxis(t, 1, axis + 1)


def setup_inputs(seed: int = 0) -> dict:
    inp = _fwd_setup_inputs(seed)
    key = _jax.random.fold_in(_jax.random.key(seed), 7919)
    shape, _ = _output_shape()
    out = dict(inp)
    out["loss_target"] = _jax.random.normal(_jax.random.fold_in(key, 0), shape, _jnp.float32)
    for i, name in enumerate(TWIN_WEIGHTS):
        w = inp[name].astype(_jnp.float32)
        if MOMENT_SCALE is None:
            s = _jnp.sqrt(_jnp.mean(_jnp.square(w)) + 1e-30)
        else:
            s = MOMENT_SCALE[name]
        km, kv = _jax.random.split(_jax.random.fold_in(key, i + 1))
        out[name] = w
        out["m_" + name] = s * _jax.random.normal(km, w.shape, _jnp.float32)
        out["v_" + name] = (s * s) * _jax.random.uniform(kv, w.shape, _jnp.float32, 0.5, 1.5)
    if N_MICROBATCH > 1:
        for name, axis in PER_EXAMPLE_BATCH_AXIS.items():
            out[name] = _to_microbatches(out[name], axis)
    return {'x': out['x'], 'positions': out['positions'], 'ffn1_norm': out['ffn1_norm'], 'ffn1_w_gate': out['ffn1_w_gate'], 'ffn1_w_up': out['ffn1_w_up'], 'ffn1_w_down': out['ffn1_w_down'], 'mix_norm': out['mix_norm'], 'ffn2_norm': out['ffn2_norm'], 'ffn2_w_gate': out['ffn2_w_gate'], 'ffn2_w_up': out['ffn2_w_up'], 'ffn2_w_down': out['ffn2_w_down'], 'hyb_w_in': out['hyb_w_in'], 'gdn_conv': out['gdn_conv'], 'gdn_a_log': out['gdn_a_log'], 'gdn_dt_bias': out['gdn_dt_bias'], 'gdn_out_norm': out['gdn_out_norm'], 'pool_w': out['pool_w'], 'pool_scale': out['pool_scale'], 'hyb_w_out': out['hyb_w_out'], 'mla_w_in': out['mla_w_in'], 'mla_q_norm': out['mla_q_norm'], 'mla_kv_norm': out['mla_kv_norm'], 'mla_w_q_up': out['mla_w_q_up'], 'mla_w_kv_up': out['mla_w_kv_up'], 'mla_q_head_norm': out['mla_q_head_norm'], 'mla_k_head_norm': out['mla_k_head_norm'], 'mla_w_out': out['mla_w_out'], 'loss_target': out['loss_target'], 'm_ffn1_norm': out['m_ffn1_norm'], 'm_ffn1_w_gate': out['m_ffn1_w_gate'], 'm_ffn1_w_up': out['m_ffn1_w_up'], 'm_ffn1_w_down': out['m_ffn1_w_down'], 'm_mix_norm': out['m_mix_norm'], 'm_ffn2_norm': out['m_ffn2_norm'], 'm_ffn2_w_gate': out['m_ffn2_w_gate'], 'm_ffn2_w_up': out['m_ffn2_w_up'], 'm_ffn2_w_down': out['m_ffn2_w_down'], 'm_hyb_w_in': out['m_hyb_w_in'], 'm_gdn_conv': out['m_gdn_conv'], 'm_gdn_a_log': out['m_gdn_a_log'], 'm_gdn_dt_bias': out['m_gdn_dt_bias'], 'm_gdn_out_norm': out['m_gdn_out_norm'], 'm_pool_w': out['m_pool_w'], 'm_pool_scale': out['m_pool_scale'], 'm_hyb_w_out': out['m_hyb_w_out'], 'm_mla_w_in': out['m_mla_w_in'], 'm_mla_q_norm': out['m_mla_q_norm'], 'm_mla_kv_norm': out['m_mla_kv_norm'], 'm_mla_w_q_up': out['m_mla_w_q_up'], 'm_mla_w_kv_up': out['m_mla_w_kv_up'], 'm_mla_q_head_norm': out['m_mla_q_head_norm'], 'm_mla_k_head_norm': out['m_mla_k_head_norm'], 'm_mla_w_out': out['m_mla_w_out'], 'v_ffn1_norm': out['v_ffn1_norm'], 'v_ffn1_w_gate': out['v_ffn1_w_gate'], 'v_ffn1_w_up': out['v_ffn1_w_up'], 'v_ffn1_w_down': out['v_ffn1_w_down'], 'v_mix_norm': out['v_mix_norm'], 'v_ffn2_norm': out['v_ffn2_norm'], 'v_ffn2_w_gate': out['v_ffn2_w_gate'], 'v_ffn2_w_up': out['v_ffn2_w_up'], 'v_ffn2_w_down': out['v_ffn2_w_down'], 'v_hyb_w_in': out['v_hyb_w_in'], 'v_gdn_conv': out['v_gdn_conv'], 'v_gdn_a_log': out['v_gdn_a_log'], 'v_gdn_dt_bias': out['v_gdn_dt_bias'], 'v_gdn_out_norm': out['v_gdn_out_norm'], 'v_pool_w': out['v_pool_w'], 'v_pool_scale': out['v_pool_scale'], 'v_hyb_w_out': out['v_hyb_w_out'], 'v_mla_w_in': out['v_mla_w_in'], 'v_mla_q_norm': out['v_mla_q_norm'], 'v_mla_kv_norm': out['v_mla_kv_norm'], 'v_mla_w_q_up': out['v_mla_w_q_up'], 'v_mla_w_kv_up': out['v_mla_w_kv_up'], 'v_mla_q_head_norm': out['v_mla_q_head_norm'], 'v_mla_k_head_norm': out['v_mla_k_head_norm'], 'v_mla_w_out': out['v_mla_w_out']}


def _loss(weights, diff, rest, loss_target):
    with _jax.named_scope("forward"):
        args = {**rest, TWIN_DIFF_INPUT: diff, **{k: w.astype(_WEIGHT_DTYPES[k]) for k, w in weights.items()}}
        y = _forward(args)
    with _jax.named_scope("loss_head"):
        err = _jnp.square(y.astype(_jnp.float32) - loss_target)
        return 0.5 * _jnp.sum(_jnp.mean(err, axis=-1)) if err.ndim else 0.5 * err


def _adamw(w, g, m, v):
    m = ADAM_B1 * m + (1.0 - ADAM_B1) * g
    v = ADAM_B2 * v + (1.0 - ADAM_B2) * _jnp.square(g)
    m_hat = m / (1.0 - ADAM_B1 ** ADAM_STEP)
    v_hat = v / (1.0 - ADAM_B2 ** ADAM_STEP)
    delta = -ADAM_LR * (m_hat / (_jnp.sqrt(v_hat) + ADAM_EPS) + ADAM_WD * w)
    return delta, m, v


def reference(x, positions, ffn1_norm, ffn1_w_gate, ffn1_w_up, ffn1_w_down, mix_norm, ffn2_norm, ffn2_w_gate, ffn2_w_up, ffn2_w_down, hyb_w_in, gdn_conv, gdn_a_log, gdn_dt_bias, gdn_out_norm, pool_w, pool_scale, hyb_w_out, mla_w_in, mla_q_norm, mla_kv_norm, mla_w_q_up, mla_w_kv_up, mla_q_head_norm, mla_k_head_norm, mla_w_out, loss_target, m_ffn1_norm, m_ffn1_w_gate, m_ffn1_w_up, m_ffn1_w_down, m_mix_norm, m_ffn2_norm, m_ffn2_w_gate, m_ffn2_w_up, m_ffn2_w_down, m_hyb_w_in, m_gdn_conv, m_gdn_a_log, m_gdn_dt_bias, m_gdn_out_norm, m_pool_w, m_pool_scale, m_hyb_w_out, m_mla_w_in, m_mla_q_norm, m_mla_kv_norm, m_mla_w_q_up, m_mla_w_kv_up, m_mla_q_head_norm, m_mla_k_head_norm, m_mla_w_out, v_ffn1_norm, v_ffn1_w_gate, v_ffn1_w_up, v_ffn1_w_down, v_mix_norm, v_ffn2_norm, v_ffn2_w_gate, v_ffn2_w_up, v_ffn2_w_down, v_hyb_w_in, v_gdn_conv, v_gdn_a_log, v_gdn_dt_bias, v_gdn_out_norm, v_pool_w, v_pool_scale, v_hyb_w_out, v_mla_w_in, v_mla_q_norm, v_mla_kv_norm, v_mla_w_q_up, v_mla_w_kv_up, v_mla_q_head_norm, v_mla_k_head_norm, v_mla_w_out):
    given = dict(x=x, positions=positions, ffn1_norm=ffn1_norm, ffn1_w_gate=ffn1_w_gate, ffn1_w_up=ffn1_w_up, ffn1_w_down=ffn1_w_down, mix_norm=mix_norm, ffn2_norm=ffn2_norm, ffn2_w_gate=ffn2_w_gate, ffn2_w_up=ffn2_w_up, ffn2_w_down=ffn2_w_down, hyb_w_in=hyb_w_in, gdn_conv=gdn_conv, gdn_a_log=gdn_a_log, gdn_dt_bias=gdn_dt_bias, gdn_out_norm=gdn_out_norm, pool_w=pool_w, pool_scale=pool_scale, hyb_w_out=hyb_w_out, mla_w_in=mla_w_in, mla_q_norm=mla_q_norm, mla_kv_norm=mla_kv_norm, mla_w_q_up=mla_w_q_up, mla_w_kv_up=mla_w_kv_up, mla_q_head_norm=mla_q_head_norm, mla_k_head_norm=mla_k_head_norm, mla_w_out=mla_w_out, loss_target=loss_target, m_ffn1_norm=m_ffn1_norm, m_ffn1_w_gate=m_ffn1_w_gate, m_ffn1_w_up=m_ffn1_w_up, m_ffn1_w_down=m_ffn1_w_down, m_mix_norm=m_mix_norm, m_ffn2_norm=m_ffn2_norm, m_ffn2_w_gate=m_ffn2_w_gate, m_ffn2_w_up=m_ffn2_w_up, m_ffn2_w_down=m_ffn2_w_down, m_hyb_w_in=m_hyb_w_in, m_gdn_conv=m_gdn_conv, m_gdn_a_log=m_gdn_a_log, m_gdn_dt_bias=m_gdn_dt_bias, m_gdn_out_norm=m_gdn_out_norm, m_pool_w=m_pool_w, m_pool_scale=m_pool_scale, m_hyb_w_out=m_hyb_w_out, m_mla_w_in=m_mla_w_in, m_mla_q_norm=m_mla_q_norm, m_mla_kv_norm=m_mla_kv_norm, m_mla_w_q_up=m_mla_w_q_up, m_mla_w_kv_up=m_mla_w_kv_up, m_mla_q_head_norm=m_mla_q_head_norm, m_mla_k_head_norm=m_mla_k_head_norm, m_mla_w_out=m_mla_w_out, v_ffn1_norm=v_ffn1_norm, v_ffn1_w_gate=v_ffn1_w_gate, v_ffn1_w_up=v_ffn1_w_up, v_ffn1_w_down=v_ffn1_w_down, v_mix_norm=v_mix_norm, v_ffn2_norm=v_ffn2_norm, v_ffn2_w_gate=v_ffn2_w_gate, v_ffn2_w_up=v_ffn2_w_up, v_ffn2_w_down=v_ffn2_w_down, v_hyb_w_in=v_hyb_w_in, v_gdn_conv=v_gdn_conv, v_gdn_a_log=v_gdn_a_log, v_gdn_dt_bias=v_gdn_dt_bias, v_gdn_out_norm=v_gdn_out_norm, v_pool_w=v_pool_w, v_pool_scale=v_pool_scale, v_hyb_w_out=v_hyb_w_out, v_mla_w_in=v_mla_w_in, v_mla_q_norm=v_mla_q_norm, v_mla_kv_norm=v_mla_kv_norm, v_mla_w_q_up=v_mla_w_q_up, v_mla_w_kv_up=v_mla_w_kv_up, v_mla_q_head_norm=v_mla_q_head_norm, v_mla_k_head_norm=v_mla_k_head_norm, v_mla_w_out=v_mla_w_out)
    weights = {n: given[n] for n in TWIN_WEIGHTS}
    shared = {n: given[n] for n in SHARED_INPUTS}
    per_example = {n: given[n] for n in ['x', 'positions']}
    grad_fn = _jax.value_and_grad(_loss, argnums=(0, 1))

    def one_microbatch(ex, loss_target):
        ex = dict(ex)
        diff = ex.pop(TWIN_DIFF_INPUT)
        return grad_fn(weights, diff, {**shared, **ex}, loss_target)

    if N_MICROBATCH == 1:
        loss, (grad_w, grad_x) = one_microbatch(per_example, given["loss_target"])
    else:
        def body(carry, xs):
            loss_sum, grad_sum = carry
            l_k, (gw_k, gx_k) = one_microbatch(xs[0], xs[1])
            with _jax.named_scope("update"):
                return (loss_sum + l_k, _jax.tree.map(_jnp.add, grad_sum, gw_k)), gx_k

        init = (_jnp.zeros((), _jnp.float32), _jax.tree.map(_jnp.zeros_like, weights))
        (loss, grad_w), grad_x = _jax.lax.scan(body, init, (per_example, given["loss_target"]))
    with _jax.named_scope("update"):
        delta_w, new_m, new_v = {}, {}, {}
        for n in TWIN_WEIGHTS:
            delta_w[n], new_m[n], new_v[n] = _adamw(weights[n], grad_w[n], given["m_" + n], given["v_" + n])
    return (loss, grad_x, *[grad_w[n] for n in TWIN_WEIGHTS], *[delta_w[n] for n in TWIN_WEIGHTS],
            *[new_m[n] for n in TWIN_WEIGHTS], *[new_v[n] for n in TWIN_WEIGHTS])
```

```python
import functools
import math

import jax
import jax.numpy as jnp
import numpy as np
from jax import lax
from jax.experimental import pallas as pl
from jax.experimental.pallas import tpu as pltpu

F32 = jnp.float32
BF = jnp.bfloat16
HI = lax.Precision.HIGHEST
MESH = pl.DeviceIdType.MESH

D_MODEL = 2048
D_FF = 4096
DEPTH = 4
GDN_HEADS = 8
HEAD_W = 128
GDN_W = GDN_HEADS * HEAD_W
CONV_K = 4
CHUNK = 64
POOL_WINDOWS = (2, 4, 8, 16)
POOL_W = 1024
POOL_GROUP_W = 256
EVEN_IN = 5136
MLA_HEADS = 16
LORA = 512
ROPE = 64
QK_HEAD = HEAD_W + ROPE
ODD_IN = 2 * LORA + ROPE
ODD_IN_PAD = 2 * LORA + HEAD_W
ROPE_THETA = 10000.0
EPS = 1e-6
N_CHIPS = 4

ADAM_LR = 0.001
ADAM_B1 = 0.9
ADAM_B2 = 0.999
ADAM_EPS = 1e-08
ADAM_WD = 0.01
ADAM_STEP = 10

ROW_BLOCK = 256
COL_BLOCK = 1024
MM_TILE = 512
MM_TILE_K = 1024
VMEM_LIMIT = 56 * 1024 * 1024


def _cparams(sem=None):
    return pltpu.CompilerParams(dimension_semantics=sem, vmem_limit_bytes=VMEM_LIMIT)


def _bdot(a, b, ca, cb):
    return lax.dot_general(a.astype(BF), b.astype(BF), (((ca,), (cb,)), ((), ())),
                           preferred_element_type=F32)


@jax.custom_vjp
def mm_nn(a, b):
    return _bdot(a, b, 1, 0)


@jax.custom_vjp
def mm_nt(a, b):
    return _bdot(a, b, 1, 1)


@jax.custom_vjp
def mm_tn(a, b):
    return _bdot(a, b, 0, 0)


mm_nn.defvjp(lambda a, b: (mm_nn(a, b), (a, b)), lambda r, g: (mm_nt(g, r[1]), mm_tn(r[0], g)))
mm_nt.defvjp(lambda a, b: (mm_nt(a, b), (a, b)), lambda r, g: (mm_nn(g, r[1]), mm_tn(g, r[0])))
mm_tn.defvjp(lambda a, b: (mm_tn(a, b), (a, b)), lambda r, g: (mm_nt(r[1], g), mm_nn(r[0], g)))


def hdot(a, b):
    return lax.dot_general(a, b, (((1,), (0,)), ((), ())), precision=HI, preferred_element_type=F32)


def hdot_nt(a, b):
    return lax.dot_general(a, b, (((1,), (1,)), ((), ())), precision=HI, preferred_element_type=F32)


def _sigmoid(x):
    return 1.0 / (1.0 + jnp.exp(-x))


def _silu(x):
    return x * _sigmoid(x)


def _softplus(x):
    return jnp.maximum(x, 0.0) + jnp.log(1.0 + jnp.exp(-jnp.abs(x)))


def _tile(dim, cap):
    if dim <= cap:
        return dim
    t = (cap // 128) * 128
    while t >= 128:
        if dim % t == 0:
            return t
        t -= 128
    raise ValueError(f"no tile for {dim}")


def _as_tuple(r):
    return tuple(r) if isinstance(r, (tuple, list)) else (r,)


def _row_block(t):
    return min(ROW_BLOCK, t)


def matmul(a, b, mode, *, name, alpha=1.0, add=None, out_dtype=F32):
    if mode == "nn":
        (m, k), (k2, n) = a.shape, b.shape
    elif mode == "nt":
        (m, k), (n, k2) = a.shape, b.shape
    else:
        (k, m), (k2, n) = a.shape, b.shape
    assert k == k2, (a.shape, b.shape, mode)
    tm, tn, tk = _tile(m, MM_TILE), _tile(n, MM_TILE), _tile(k, MM_TILE_K)
    nk = k // tk
    ca = 0 if mode == "tn" else 1
    cb = 1 if mode == "nt" else 0
    a_spec = (pl.BlockSpec((tk, tm), lambda i, j, kk: (kk, i)) if mode == "tn"
              else pl.BlockSpec((tm, tk), lambda i, j, kk: (i, kk)))
    b_spec = (pl.BlockSpec((tn, tk), lambda i, j, kk: (j, kk)) if mode == "nt"
              else pl.BlockSpec((tk, tn), lambda i, j, kk: (kk, j)))
    o_spec = pl.BlockSpec((tm, tn), lambda i, j, kk: (i, j))
    has_add = add is not None

    def body(*refs):
        if has_add:
            a_ref, b_ref, add_ref, o_ref, acc_ref = refs
        else:
            a_ref, b_ref, o_ref, acc_ref = refs
        kk = pl.program_id(2)

        @pl.when(kk == 0)
        def _():
            acc_ref[...] = jnp.zeros_like(acc_ref)

        acc_ref[...] += _bdot(a_ref[...], b_ref[...], ca, cb)

        @pl.when(kk == nk - 1)
        def _():
            r = acc_ref[...]
            if alpha != 1.0:
                r = r * alpha
            if has_add:
                r = r + add_ref[...].astype(F32)
            o_ref[...] = r.astype(out_dtype)

    in_specs = [a_spec, b_spec] + ([o_spec] if has_add else [])
    args = (a, b) + ((add,) if has_add else ())
    return pl.pallas_call(
        body, name=name, grid=(m // tm, n // tn, nk), in_specs=in_specs, out_specs=o_spec,
        out_shape=jax.ShapeDtypeStruct((m, n), out_dtype),
        scratch_shapes=[pltpu.VMEM((tm, tn), F32)],
        compiler_params=_cparams(("parallel", "parallel", "arbitrary")),
    )(*args)


def _row_spec(tb, bc, cf):
    return pl.BlockSpec((tb, bc), lambda i, j, cf=cf: (i, cf(j)))


def _par_spec(p):
    return pl.BlockSpec(p.shape, lambda i, j: (0, 0))


def rowwise(f, rows, pars, outs, *, ncol=1, name):
    t = rows[0][0].shape[0]
    tb = _row_block(t)
    nr = len(rows)

    def body(*refs):
        vals = [r[...].astype(F32) for r in refs[:nr + len(pars)]]
        res = _as_tuple(f(*vals))
        for o_ref, r in zip(refs[nr + len(pars):], res):
            o_ref[...] = r.astype(o_ref.dtype)

    return pl.pallas_call(
        body, name=name, grid=(t // tb, ncol),
        in_specs=[_row_spec(tb, bc, cf) for _, bc, cf in rows] + [_par_spec(p) for p in pars],
        out_specs=[_row_spec(tb, bc, cf) for _, bc, cf, _ in outs],
        out_shape=[jax.ShapeDtypeStruct((t, tc), dt) for tc, _, _, dt in outs],
        compiler_params=_cparams(("parallel", "arbitrary")),
    )(*[r[0] for r in rows], *pars)


def rowwise_vjp(f, rows, pars, cts, row_grads, *, ncol=1, name, par_grads=True, consts=()):
    t = rows[0][0].shape[0]
    tb = _row_block(t)
    consts = list(consts)
    nr, npar, nct, ncon = len(rows), len(pars), len(cts), len(consts)
    diff_rows = [i for i, g in enumerate(row_grads) if g is not None]
    adds = [row_grads[i][3] for i in diff_rows]
    add_idx = [i for i, a in enumerate(adds) if a is not None]

    def body(*refs):
        pos = 0
        row_refs = refs[pos:pos + nr]; pos += nr
        par_refs = refs[pos:pos + npar]; pos += npar
        con_refs = refs[pos:pos + ncon]; pos += ncon
        ct_refs = refs[pos:pos + nct]; pos += nct
        add_refs = refs[pos:pos + len(add_idx)]; pos += len(add_idx)
        grow_refs = refs[pos:pos + len(diff_rows)]; pos += len(diff_rows)
        gpar_refs = refs[pos:]
        row_vals = [r[...].astype(F32) for r in row_refs]
        par_vals = [r[...].astype(F32) for r in par_refs]
        con_vals = [r[...].astype(F32) for r in con_refs]

        def g(*dvals):
            rv = list(row_vals)
            for i, v in zip(diff_rows, dvals[:len(diff_rows)]):
                rv[i] = v
            pv = dvals[len(diff_rows):] if par_grads else par_vals
            return _as_tuple(f(*rv, *pv, *con_vals))

        prim = [row_vals[i] for i in diff_rows] + (par_vals if par_grads else [])
        _, pull = jax.vjp(g, *prim)
        grads = pull(tuple(c[...].astype(F32) for c in ct_refs))
        for n, ref in enumerate(grow_refs):
            gr = grads[n]
            if n in add_idx:
                gr = gr + add_refs[add_idx.index(n)][...]
            ref[...] = gr.astype(ref.dtype)
        if par_grads:
            first = jnp.logical_and(pl.program_id(0) == 0, pl.program_id(1) == 0)
            for ref, gr in zip(gpar_refs, grads[len(diff_rows):]):
                @pl.when(first)
                def _(ref=ref):
                    ref[...] = jnp.zeros_like(ref)
                ref[...] += gr

    gspecs = [row_grads[i] for i in diff_rows]
    in_specs = ([_row_spec(tb, bc, cf) for _, bc, cf in rows] + [_par_spec(p) for p in pars + consts]
                + [_row_spec(tb, bc, cf) for _, bc, cf in cts]
                + [_row_spec(tb, gspecs[i][1], gspecs[i][2]) for i in add_idx])
    out_specs = [_row_spec(tb, bc, cf) for _, bc, cf, _ in gspecs]
    out_shape = [jax.ShapeDtypeStruct((t, tc), F32) for tc, _, _, _ in gspecs]
    if par_grads:
        out_specs += [_par_spec(p) for p in pars]
        out_shape += [jax.ShapeDtypeStruct(p.shape, F32) for p in pars]
    res = pl.pallas_call(
        body, name=name, grid=(t // tb, ncol), in_specs=in_specs, out_specs=out_specs,
        out_shape=out_shape, compiler_params=_cparams(("arbitrary", "arbitrary")),
    )(*[r[0] for r in rows], *pars, *consts, *[c[0] for c in cts], *[adds[i] for i in add_idx])
    return list(res[:len(diff_rows)]), list(res[len(diff_rows):])


def _col0(j):
    return 0


def _colj(j):
    return j


def _full(a):
    return (a, a.shape[1], _col0)


def _rms(x, gain):
    return x * lax.rsqrt(jnp.mean(x * x, axis=-1, keepdims=True) + EPS) * gain


def _swiglu_act(g, u):
    return _silu(g) * u


def ffn_fwd(x, gain, wg, wu, wd, tag):
    d = x.shape[1]
    (h,) = rowwise(_rms, [_full(x)], [gain], [(d, d, _col0, BF)], name=f"{tag}_norm")
    g = matmul(h, wg, "nn", name=f"{tag}_gate")
    u = matmul(h, wu, "nn", name=f"{tag}_up")
    f = g.shape[1]
    cb = _tile(f, COL_BLOCK)
    (a,) = rowwise(_swiglu_act, [(g, cb, _colj), (u, cb, _colj)], [], [(f, cb, _colj, BF)], ncol=f // cb,
                   name=f"{tag}_act")
    y = matmul(a, wd, "nn", alpha=0.5, add=x, name=f"{tag}_down")
    return y, (x, h, g, u, a)


def ffn_bwd(dy, saved, gain, wg, wu, wd, tag):
    x, h, g, u, a = saved
    d, f = x.shape[1], g.shape[1]
    da = matmul(dy, wd, "nt", alpha=0.5, name=f"{tag}_down_dx")
    dwd = matmul(a, dy, "tn", alpha=0.5, name=f"{tag}_down_dw")
    cb = _tile(f, COL_BLOCK)
    (dg, du), _ = rowwise_vjp(_swiglu_act, [(g, cb, _colj), (u, cb, _colj)], [], [(da, cb, _colj)],
                              [(f, cb, _colj, None), (f, cb, _colj, None)], ncol=f // cb,
                              name=f"{tag}_act_bwd", par_grads=False)
    dwg = matmul(h, dg, "tn", name=f"{tag}_gate_dw")
    dwu = matmul(h, du, "tn", name=f"{tag}_up_dw")
    dh = matmul(dg, wg, "nt", name=f"{tag}_gate_dx")
    dh = matmul(du, wu, "nt", add=dh, name=f"{tag}_up_dx")
    (dx,), (dgain,) = rowwise_vjp(_rms, [_full(x)], [gain], [_full(dh)], [(d, d, _col0, dy)],
                                  name=f"{tag}_norm_bwd")
    return dx, dgain, dwg, dwu, dwd


def rope_tables(positions, inv_freq, sign):
    t = positions.shape[0]
    tb = _row_block(t)

    def body(pos_ref, f_ref, s_ref, c_ref, sn_ref):
        ang = pos_ref[...].astype(F32) * f_ref[...]
        live = jnp.abs(s_ref[...])
        c_ref[...] = jnp.cos(ang) * live
        sn_ref[...] = jnp.sin(ang) * s_ref[...]

    return pl.pallas_call(
        body, name="rope_tables", grid=(t // tb,),
        in_specs=[pl.BlockSpec((tb, 1), lambda i: (i, 0)), pl.BlockSpec((1, HEAD_W), lambda i: (0, 0)),
                  pl.BlockSpec((1, HEAD_W), lambda i: (0, 0))],
        out_specs=[pl.BlockSpec((tb, HEAD_W), lambda i: (i, 0))] * 2,
        out_shape=[jax.ShapeDtypeStruct((t, HEAD_W), F32)] * 2,
        compiler_params=_cparams(("parallel",)),
    )(positions, inv_freq, sign)


def _rope(p, c, s, swap):
    return p * c + hdot(p, swap) * s


def _pe_norm(pe, gp):
    return pe * lax.rsqrt(jnp.sum(pe * pe, axis=-1, keepdims=True) * (1.0 / ROPE) + EPS) * gp


def _q_head(nope, pe, c, s, gn, gp, swap):
    return _rms(nope, gn), _rope(_pe_norm(pe, gp), c, s, swap)


def _k_head(nope, v, gn):
    return _rms(nope, gn), v


def _kpe_head(pe, c, s, gp, swap):
    return _rope(_pe_norm(pe, gp), c, s, swap)


ATT_BLOCK = 512
ATT_SCALE = QK_HEAD ** -0.5
NEG = float(np.finfo(np.float32).min)


def _att_block(t):
    return min(ATT_BLOCK, t)


def _scores(qn, qp, kn, kp, diag):
    q = jnp.concatenate([qn, qp], axis=1)
    k = jnp.concatenate([kn, kp], axis=1)
    s = lax.dot_general(q, k, (((1,), (1,)), ((), ())), preferred_element_type=F32) * ATT_SCALE
    rows = lax.broadcasted_iota(jnp.int32, s.shape, 0)
    cols = lax.broadcasted_iota(jnp.int32, s.shape, 1)
    return jnp.where(jnp.logical_or(jnp.logical_not(diag), rows >= cols), s, NEG), q, k


def attention_fwd(qn, qp, kn, kp, v):
    t = qn.shape[0]
    h = qn.shape[1] // HEAD_W
    tq = _att_block(t)
    nq = t // tq

    def body(qn_ref, qp_ref, kn_ref, kp_ref, v_ref, o_ref, lse_ref, m_ref, l_ref, acc_ref):
        i, j = pl.program_id(1), pl.program_id(2)

        @pl.when(j == 0)
        def _():
            m_ref[...] = jnp.full_like(m_ref, NEG)
            l_ref[...] = jnp.zeros_like(l_ref)
            acc_ref[...] = jnp.zeros_like(acc_ref)

        @pl.when(j <= i)
        def _():
            s, _, _ = _scores(qn_ref[...], qp_ref[...], kn_ref[...], kp_ref[...], j == i)
            m_new = jnp.maximum(m_ref[...], jnp.max(s, axis=-1, keepdims=True))
            a = jnp.exp(m_ref[...] - m_new)
            p = jnp.exp(s - m_new)
            l_ref[...] = a * l_ref[...] + jnp.sum(p, axis=-1, keepdims=True)
            acc_ref[...] = a * acc_ref[...] + jnp.dot(p.astype(BF), v_ref[...], preferred_element_type=F32)
            m_ref[...] = m_new

        @pl.when(j == nq - 1)
        def _():
            o_ref[...] = acc_ref[...] / l_ref[...]
            lse_ref[...] = m_ref[...] + jnp.log(l_ref[...])

    qspec = pl.BlockSpec((tq, HEAD_W), lambda hh, i, j: (i, hh))
    kspec = pl.BlockSpec((tq, HEAD_W), lambda hh, i, j: (jnp.minimum(i, j), hh))
    kpspec = pl.BlockSpec((tq, HEAD_W), lambda hh, i, j: (jnp.minimum(i, j), 0))
    return pl.pallas_call(
        body, name="attention_fwd", grid=(h, nq, nq),
        in_specs=[qspec, qspec, kspec, kpspec, kspec],
        out_specs=[qspec, pl.BlockSpec((None, tq, 1), lambda hh, i, j: (hh, i, 0))],
        out_shape=[jax.ShapeDtypeStruct((t, h * HEAD_W), F32), jax.ShapeDtypeStruct((h, t, 1), F32)],
        scratch_shapes=[pltpu.VMEM((tq, 1), F32), pltpu.VMEM((tq, 1), F32), pltpu.VMEM((tq, HEAD_W), F32)],
        compiler_params=_cparams(("parallel", "parallel", "arbitrary")),
    )(qn, qp, kn, kp, v)


def attention_delta(o, do):
    t = o.shape[0]
    h = o.shape[1] // HEAD_W
    tq = _att_block(t)

    def body(o_ref, do_ref, d_ref):
        d_ref[...] = jnp.sum(o_ref[...] * do_ref[...], axis=-1, keepdims=True)

    spec = pl.BlockSpec((tq, HEAD_W), lambda hh, i: (i, hh))
    return pl.pallas_call(
        body, name="attention_delta", grid=(h, t // tq), in_specs=[spec, spec],
        out_specs=pl.BlockSpec((None, tq, 1), lambda hh, i: (hh, i, 0)),
        out_shape=jax.ShapeDtypeStruct((h, t, 1), F32),
        compiler_params=_cparams(("parallel", "parallel")),
    )(o, do)


def _att_grads(qn_ref, qp_ref, kn_ref, kp_ref, v_ref, do_ref, lse_ref, dl_ref, diag):
    s, q, k = _scores(qn_ref[...], qp_ref[...], kn_ref[...], kp_ref[...], diag)
    p = jnp.exp(s - lse_ref[...])
    do = do_ref[...].astype(BF)
    dp = lax.dot_general(do, v_ref[...], (((1,), (1,)), ((), ())), preferred_element_type=F32)
    ds = p * (dp - dl_ref[...]) * ATT_SCALE
    return p, ds, q, k, do


def attention_bwd_q(qn, qp, kn, kp, v, do, lse, delta):
    t = qn.shape[0]
    h = qn.shape[1] // HEAD_W
    tq = _att_block(t)
    nq = t // tq

    def body(qn_ref, qp_ref, kn_ref, kp_ref, v_ref, do_ref, lse_ref, dl_ref, dqn_ref, dqp_ref, acc_ref):
        i, j = pl.program_id(1), pl.program_id(2)

        @pl.when(j == 0)
        def _():
            acc_ref[...] = jnp.zeros_like(acc_ref)

        @pl.when(j <= i)
        def _():
            _, ds, _, k, _ = _att_grads(qn_ref, qp_ref, kn_ref, kp_ref, v_ref, do_ref, lse_ref, dl_ref, j == i)
            acc_ref[...] += jnp.dot(ds.astype(BF), k, preferred_element_type=F32)

        @pl.when(j == nq - 1)
        def _():
            dqn_ref[...] = acc_ref[:, :HEAD_W]
            dqp_ref[...] = acc_ref[:, HEAD_W:]

    qspec = pl.BlockSpec((tq, HEAD_W), lambda hh, i, j: (i, hh))
    kspec = pl.BlockSpec((tq, HEAD_W), lambda hh, i, j: (jnp.minimum(i, j), hh))
    kpspec = pl.BlockSpec((tq, HEAD_W), lambda hh, i, j: (jnp.minimum(i, j), 0))
    vec = pl.BlockSpec((None, tq, 1), lambda hh, i, j: (hh, i, 0))
    return pl.pallas_call(
        body, name="attention_bwd_q", grid=(h, nq, nq),
        in_specs=[qspec, qspec, kspec, kpspec, kspec, qspec, vec, vec],
        out_specs=[qspec, qspec],
        out_shape=[jax.ShapeDtypeStruct((t, h * HEAD_W), F32)] * 2,
        scratch_shapes=[pltpu.VMEM((tq, 2 * HEAD_W), F32)],
        compiler_params=_cparams(("parallel", "parallel", "arbitrary")),
    )(qn, qp, kn, kp, v, do, lse, delta)


def attention_bwd_kv(qn, qp, kn, kp, v, do, lse, delta):
    t = qn.shape[0]
    h = qn.shape[1] // HEAD_W
    tq = _att_block(t)
    nq = t // tq

    def body(qn_ref, qp_ref, kn_ref, kp_ref, v_ref, do_ref, lse_ref, dl_ref, dkn_ref, dkp_ref, dv_ref,
             dk_acc, dv_acc):
        j, hh, i = pl.program_id(0), pl.program_id(1), pl.program_id(2)

        @pl.when(i == 0)
        def _():
            dk_acc[...] = jnp.zeros_like(dk_acc)
            dv_acc[...] = jnp.zeros_like(dv_acc)

        @pl.when(jnp.logical_and(i == 0, hh == 0))
        def _():
            dkp_ref[...] = jnp.zeros_like(dkp_ref)

        @pl.when(i >= j)
        def _():
            p, ds, q, _, do = _att_grads(qn_ref, qp_ref, kn_ref, kp_ref, v_ref, do_ref, lse_ref, dl_ref, j == i)
            dv_acc[...] += lax.dot_general(p.astype(BF), do, (((0,), (0,)), ((), ())), preferred_element_type=F32)
            dk_acc[...] += lax.dot_general(ds.astype(BF), q, (((0,), (0,)), ((), ())), preferred_element_type=F32)

        @pl.when(i == nq - 1)
        def _():
            dkn_ref[...] = dk_acc[:, :HEAD_W]
            dkp_ref[...] += dk_acc[:, HEAD_W:]
            dv_ref[...] = dv_acc[...]

    qspec = pl.BlockSpec((tq, HEAD_W), lambda j, hh, i: (jnp.maximum(i, j), hh))
    kspec = pl.BlockSpec((tq, HEAD_W), lambda j, hh, i: (j, hh))
    kpspec = pl.BlockSpec((tq, HEAD_W), lambda j, hh, i: (j, 0))
    vec = pl.BlockSpec((None, tq, 1), lambda j, hh, i: (hh, jnp.maximum(i, j), 0))
    return pl.pallas_call(
        body, name="attention_bwd_kv", grid=(nq, h, nq),
        in_specs=[qspec, qspec, kspec, kpspec, kspec, qspec, vec, vec],
        out_specs=[kspec, kpspec, kspec],
        out_shape=[jax.ShapeDtypeStruct((t, h * HEAD_W), F32), jax.ShapeDtypeStruct((t, HEAD_W), F32),
                   jax.ShapeDtypeStruct((t, h * HEAD_W), F32)],
        scratch_shapes=[pltpu.VMEM((tq, 2 * HEAD_W), F32), pltpu.VMEM((tq, HEAD_W), F32)],
        compiler_params=_cparams(("parallel", "arbitrary", "arbitrary")),
    )(qn, qp, kn, kp, v, do, lse, delta)


def _even(j):
    return 2 * j


def _odd(j):
    return 2 * j + 1


def _rope_consts():
    lane = np.arange(HEAD_W)
    half = ROPE // 2
    swap = np.zeros((HEAD_W, HEAD_W), np.float32)
    swap[lane[:half] + half, lane[:half]] = 1.0
    swap[lane[:half], lane[:half] + half] = 1.0
    sign = np.where(lane < half, -1.0, np.where(lane < ROPE, 1.0, 0.0)).astype(np.float32)[None]
    inv_freq = ROPE_THETA ** (-jnp.arange(0, ROPE, 2, dtype=F32) / ROPE)
    inv_freq = jnp.concatenate([inv_freq, inv_freq, jnp.zeros((HEAD_W - ROPE,), F32)])[None]
    return jnp.asarray(swap), jnp.asarray(sign), inv_freq


def _pad_gain(g):
    return g[None, :HEAD_W], jnp.pad(g[HEAD_W:], (0, HEAD_W - ROPE))[None]


def mla_fwd(x, cos, sin, mix_gain, w_in, q_gain, kv_gain, w_q, w_kv, qh_gain, kh_gain, w_out, tag):
    d = x.shape[1]
    nh = MLA_HEADS
    swap = _rope_consts()[0]
    (h,) = rowwise(_rms, [_full(x)], [mix_gain], [(d, d, _col0, BF)], name=f"{tag}_norm")
    proj = matmul(h, w_in, "nn", name=f"{tag}_in")
    (qlat,) = rowwise(_rms, [(proj, LORA, _col0)], [q_gain], [(LORA, LORA, _col0, BF)], name=f"{tag}_qnorm")
    (kvlat,) = rowwise(_rms, [(proj, LORA, lambda j: 1)], [kv_gain], [(LORA, LORA, _col0, BF)],
                       name=f"{tag}_kvnorm")
    q = matmul(qlat, w_q, "nn", name=f"{tag}_qup")
    kv = matmul(kvlat, w_kv, "nn", name=f"{tag}_kvup")
    qgn, qgp = _pad_gain(qh_gain)
    kgn, kgp = _pad_gain(kh_gain)
    w = nh * HEAD_W
    qn, qp = rowwise(_q_head, [(q, HEAD_W, _even), (q, HEAD_W, _odd), _full(cos), _full(sin)], [qgn, qgp, swap],
                     [(w, HEAD_W, _colj, BF), (w, HEAD_W, _colj, BF)], ncol=nh, name=f"{tag}_qhead")
    kn, v = rowwise(_k_head, [(kv, HEAD_W, _even), (kv, HEAD_W, _odd)], [kgn],
                    [(w, HEAD_W, _colj, BF), (w, HEAD_W, _colj, BF)], ncol=nh, name=f"{tag}_khead")
    (kp,) = rowwise(_kpe_head, [(proj, HEAD_W, lambda j: 2 * LORA // HEAD_W), _full(cos), _full(sin)], [kgp, swap],
                    [(HEAD_W, HEAD_W, _col0, BF)], name=f"{tag}_kpe")
    o, lse = attention_fwd(qn, qp, kn, kp, v)
    y = matmul(o, w_out, "nn", add=x, name=f"{tag}_out")
    return y, (x, h, proj, qlat, kvlat, q, kv, qn, qp, kn, kp, v, o, lse)


def mla_bwd(dy, saved, cos, sin, mix_gain, w_in, q_gain, kv_gain, w_q, w_kv, qh_gain, kh_gain, w_out, tag):
    x, h, proj, qlat, kvlat, q, kv, qn, qp, kn, kp, v, o, lse = saved
    d = x.shape[1]
    nh = MLA_HEADS
    w = nh * HEAD_W
    swap = _rope_consts()[0]
    qgn, qgp = _pad_gain(qh_gain)
    kgn, kgp = _pad_gain(kh_gain)
    do = matmul(dy, w_out, "nt", name=f"{tag}_out_dx")
    dw_out = matmul(o, dy, "tn", name=f"{tag}_out_dw")
    delta = attention_delta(o, do)
    dqn, dqp = attention_bwd_q(qn, qp, kn, kp, v, do, lse, delta)
    dkn, dkp, dv = attention_bwd_kv(qn, qp, kn, kp, v, do, lse, delta)
    (dq_a, dq_b), (dqgn, dqgp) = rowwise_vjp(
        _q_head, [(q, HEAD_W, _even), (q, HEAD_W, _odd), _full(cos), _full(sin)], [qgn, qgp],
        [(dqn, HEAD_W, _colj), (dqp, HEAD_W, _colj)],
        [(w, HEAD_W, _colj, None), (w, HEAD_W, _colj, None), None, None], ncol=nh, consts=[swap],
        name=f"{tag}_qhead_bwd")
    dq = _interleave(dq_a, dq_b)
    (dkv_a, dkv_b), (dkgn,) = rowwise_vjp(
        _k_head, [(kv, HEAD_W, _even), (kv, HEAD_W, _odd)], [kgn], [(dkn, HEAD_W, _colj), (dv, HEAD_W, _colj)],
        [(w, HEAD_W, _colj, None), (w, HEAD_W, _colj, None)], ncol=nh, name=f"{tag}_khead_bwd")
    dkv = _interleave(dkv_a, dkv_b)
    (dpe,), (dkgp,) = rowwise_vjp(
        _kpe_head, [(proj, HEAD_W, lambda j: 2 * LORA // HEAD_W), _full(cos), _full(sin)], [kgp], [_full(dkp)],
        [(HEAD_W, HEAD_W, _col0, None), None, None], consts=[swap], name=f"{tag}_kpe_bwd")
    dw_q = matmul(qlat, dq, "tn", name=f"{tag}_qup_dw")
    dw_kv = matmul(kvlat, dkv, "tn", name=f"{tag}_kvup_dw")
    dqlat = matmul(dq, w_q, "nt", name=f"{tag}_qup_dx")
    dkvlat = matmul(dkv, w_kv, "nt", name=f"{tag}_kvup_dx")
    (dpq,), (dq_gain,) = rowwise_vjp(_rms, [(proj, LORA, _col0)], [q_gain], [_full(dqlat)],
                                     [(LORA, LORA, _col0, None)], name=f"{tag}_qnorm_bwd")
    (dpkv,), (dkv_gain,) = rowwise_vjp(_rms, [(proj, LORA, lambda j: 1)], [kv_gain], [_full(dkvlat)],
                                       [(LORA, LORA, _col0, None)], name=f"{tag}_kvnorm_bwd")
    dproj = jnp.concatenate([dpq, dpkv, dpe], axis=1)
    dw_in = matmul(h, dproj, "tn", name=f"{tag}_in_dw")
    dh = matmul(dproj, w_in, "nt", name=f"{tag}_in_dx")
    (dx,), (dmix,) = rowwise_vjp(_rms, [_full(x)], [mix_gain], [_full(dh)], [(d, d, _col0, dy)],
                                 name=f"{tag}_norm_bwd")
    dqh = jnp.concatenate([dqgn[0], dqgp[0, :ROPE]])
    dkh = jnp.concatenate([dkgn[0], dkgp[0, :ROPE]])
    return dx, (dmix, dw_in, dq_gain, dkv_gain, dw_q, dw_kv, dqh, dkh, dw_out)


def _interleave(a, b):
    t, w = a.shape
    n = w // HEAD_W
    return jnp.stack([a.reshape(t, n, HEAD_W), b.reshape(t, n, HEAD_W)], axis=2).reshape(t, 2 * w)


CONV_COLS = 512
HALO = 8


def conv_fwd(x, w):
    t = x.shape[0]
    c = w.shape[1]
    tb = _row_block(t)
    assert t % tb == 0

    def body(x_ref, prev_ref, w_ref, y_ref):
        i = pl.program_id(1)
        xv = x_ref[...]
        prev = jnp.where(i > 0, prev_ref[...], 0.0)
        ext = jnp.concatenate([prev, xv], axis=0)
        acc = xv * w_ref[CONV_K - 1:CONV_K, :]
        for s in range(1, CONV_K):
            acc = acc + pltpu.roll(ext, s, 0)[HALO:] * w_ref[CONV_K - 1 - s:CONV_K - s, :]
        y_ref[...] = acc

    spec = pl.BlockSpec((tb, CONV_COLS), lambda j, i: (i, j))
    return pl.pallas_call(
        body, name="conv_fwd", grid=(c // CONV_COLS, t // tb),
        in_specs=[spec, pl.BlockSpec((HALO, CONV_COLS), lambda j, i: (jnp.maximum(i * (tb // HALO) - 1, 0), j)),
                  pl.BlockSpec((HALO, CONV_COLS), lambda j, i: (0, j))],
        out_specs=spec, out_shape=jax.ShapeDtypeStruct((t, c), F32),
        compiler_params=_cparams(("parallel", "parallel")),
    )(x, x, w)


def conv_bwd(x, dy, w):
    t = x.shape[0]
    c = w.shape[1]
    tb = _row_block(t)
    nrb = t // tb

    def body(x_ref, prev_ref, dy_ref, next_ref, w_ref, dx_ref, dw_ref):
        i = pl.program_id(1)
        ext_x = jnp.concatenate([jnp.where(i > 0, prev_ref[...], 0.0), x_ref[...]], axis=0)
        dyv = dy_ref[...]
        ext_dy = jnp.concatenate([dyv, jnp.where(i < nrb - 1, next_ref[...], 0.0)], axis=0)

        @pl.when(i == 0)
        def _():
            dw_ref[...] = jnp.zeros_like(dw_ref)

        acc = dyv * w_ref[CONV_K - 1:CONV_K, :]
        dw_ref[CONV_K - 1:CONV_K, :] += jnp.sum(dyv * x_ref[...], axis=0, keepdims=True)
        for s in range(1, CONV_K):
            acc = acc + pltpu.roll(ext_dy, tb + HALO - s, 0)[:tb] * w_ref[CONV_K - 1 - s:CONV_K - s, :]
            dw_ref[CONV_K - 1 - s:CONV_K - s, :] += jnp.sum(dyv * pltpu.roll(ext_x, s, 0)[HALO:], axis=0, keepdims=True)
        dx_ref[...] = acc

    spec = pl.BlockSpec((tb, CONV_COLS), lambda j, i: (i, j))
    wspec = pl.BlockSpec((HALO, CONV_COLS), lambda j, i: (0, j))
    return pl.pallas_call(
        body, name="conv_bwd", grid=(c // CONV_COLS, nrb),
        in_specs=[spec, pl.BlockSpec((HALO, CONV_COLS), lambda j, i: (jnp.maximum(i * (tb // HALO) - 1, 0), j)),
                  spec, pl.BlockSpec((HALO, CONV_COLS), lambda j, i: (jnp.minimum(i + 1, nrb - 1) * (tb // HALO), j)),
                  wspec],
        out_specs=[spec, wspec],
        out_shape=[jax.ShapeDtypeStruct((t, c), F32), jax.ShapeDtypeStruct((HALO, c), F32)],
        compiler_params=_cparams(("parallel", "arbitrary")),
    )(x, x, dy, dy, w)


def _l2_silu(c):
    a = _silu(c)
    return a * lax.rsqrt(jnp.sum(a * a, axis=-1, keepdims=True) + EPS)


def _gates(ab, a_log, dt_bias, ea, eb):
    g = -jnp.exp(a_log) * _softplus(hdot(ab, ea) + dt_bias)
    return g, _sigmoid(hdot(ab, eb))


def _gate_consts():
    ea = np.zeros((HEAD_W, GDN_W), np.float32)
    eb = np.zeros((HEAD_W, GDN_W), np.float32)
    for h in range(GDN_HEADS):
        ea[h, h * HEAD_W:(h + 1) * HEAD_W] = 1.0
        eb[GDN_HEADS + h, h * HEAD_W:(h + 1) * HEAD_W] = 1.0
    return jnp.asarray(ea), jnp.asarray(eb)


def _gdn_chunk(q, k, v, g, beta, state):
    n = CHUNK
    ii = lax.broadcasted_iota(jnp.int32, (n, n), 0)
    jj = lax.broadcasted_iota(jnp.int32, (n, n), 1)
    causal = ii >= jj
    eye = (ii == jj).astype(F32)
    gc = hdot(causal.astype(F32), g)
    mean_w = jnp.full((n, HEAD_W), 1.0 / HEAD_W, F32)
    gc_i = hdot_nt(gc, mean_w)
    gc_j = hdot_nt(mean_w, gc)
    decay = jnp.exp(jnp.where(causal, gc_i - gc_j, -1e30))
    kb = k * beta
    vb = v * beta
    m = jnp.where(ii > jj, mm_nt(kb, k) * decay, 0.0)
    inv = eye - m
    pw = m
    for _ in range(5):
        pw = hdot(pw, pw)
        inv = hdot(inv, eye + pw)
    eg = jnp.exp(gc)
    u = hdot(inv, vb)
    w = hdot(inv, kb * eg)
    qs = q * (HEAD_W ** -0.5)
    attn = mm_nt(qs, k) * decay
    g_last = hdot((jj == n - 1).astype(F32), gc)
    k_dec = k * jnp.exp(g_last - gc)
    v_new = u - mm_nn(w, state)
    out = mm_nn(qs * eg, state) + mm_nn(attn, v_new)
    new_state = state * jnp.exp(jnp.concatenate([g_last, g_last], axis=0)) + mm_tn(k_dec, v_new)
    return out, new_state


def gdn_fwd(qk, v, g, beta):
    t = v.shape[0]
    n = t // CHUNK
    nh = GDN_HEADS

    def body(q_ref, k_ref, v_ref, g_ref, b_ref, o_ref, s_ref, state):
        @pl.when(pl.program_id(1) == 0)
        def _():
            state[...] = jnp.zeros_like(state)

        s_ref[...] = state[...]
        out, new = _gdn_chunk(q_ref[...], k_ref[...], v_ref[...], g_ref[...], b_ref[...], state[...])
        o_ref[...] = out
        state[...] = new

    spec = pl.BlockSpec((CHUNK, HEAD_W), lambda h, c: (c, h))
    return pl.pallas_call(
        body, name="gdn_fwd", grid=(nh, n),
        in_specs=[spec, pl.BlockSpec((CHUNK, HEAD_W), lambda h, c: (c, nh + h)), spec, spec, spec],
        out_specs=[spec, pl.BlockSpec((None, None, HEAD_W, HEAD_W), lambda h, c: (h, c, 0, 0))],
        out_shape=[jax.ShapeDtypeStruct((t, nh * HEAD_W), F32), jax.ShapeDtypeStruct((nh, n, HEAD_W, HEAD_W), F32)],
        scratch_shapes=[pltpu.VMEM((HEAD_W, HEAD_W), F32)],
        compiler_params=_cparams(("parallel", "arbitrary")),
    )(qk, qk, v, g, beta)


def gdn_bwd(qk, v, g, beta, states, do):
    t = v.shape[0]
    n = t // CHUNK
    nh = GDN_HEADS

    def body(q_ref, k_ref, v_ref, g_ref, b_ref, s_ref, do_ref, dq_ref, dk_ref, dv_ref, dg_ref, db_ref, dstate):
        @pl.when(pl.program_id(1) == 0)
        def _():
            dstate[...] = jnp.zeros_like(dstate)

        _, pull = jax.vjp(_gdn_chunk, q_ref[...], k_ref[...], v_ref[...], g_ref[...], b_ref[...], s_ref[...])
        dq, dk, dv, dg, db, ds = pull((do_ref[...], dstate[...]))
        dq_ref[...] = dq
        dk_ref[...] = dk
        dv_ref[...] = dv
        dg_ref[...] = dg
        db_ref[...] = db
        dstate[...] = ds

    spec = pl.BlockSpec((CHUNK, HEAD_W), lambda h, c: (n - 1 - c, h))
    return pl.pallas_call(
        body, name="gdn_bwd", grid=(nh, n),
        in_specs=[spec, pl.BlockSpec((CHUNK, HEAD_W), lambda h, c: (n - 1 - c, nh + h)), spec, spec, spec,
                  pl.BlockSpec((None, None, HEAD_W, HEAD_W), lambda h, c: (h, n - 1 - c, 0, 0)), spec],
        out_specs=[spec] * 5,
        out_shape=[jax.ShapeDtypeStruct((t, nh * HEAD_W), F32)] * 5,
        scratch_shapes=[pltpu.VMEM((HEAD_W, HEAD_W), F32)],
        compiler_params=_cparams(("parallel", "arbitrary")),
    )(qk, qk, v, g, beta, states, do)


def _gdn_post(o, z, gain):
    return _rms(o, gain) * _silu(z)


POOL_HALO = 16


def _pool_counts(t0, rows, cols):
    tt = t0 + lax.broadcasted_iota(jnp.int32, (rows, cols), 0) + 1
    grp = lax.broadcasted_iota(jnp.int32, (rows, cols), 1) // POOL_GROUP_W
    win = jnp.left_shift(2, grp)
    return jnp.minimum(tt, win).astype(F32), grp


def _by_group(grp, parts):
    out = parts[-1]
    for gi in range(len(parts) - 2, -1, -1):
        out = jnp.where(grp == gi, parts[gi], out)
    return out


def pool_window_fwd(u):
    t, c = u.shape
    tb = _row_block(t)

    def body(u_ref, prev_ref, d_ref):
        i = pl.program_id(0)
        xv = u_ref[...]
        ext = jnp.concatenate([jnp.where(i > 0, prev_ref[...], 0.0), xv], axis=0)
        sums = []
        s = ext
        for sh in (1, 2, 4, 8):
            s = s + pltpu.roll(s, sh, 0)
            sums.append(s[POOL_HALO:])
        cnt, grp = _pool_counts(i * tb, tb, c)
        d_ref[...] = _by_group(grp, sums) / cnt - xv

    spec = pl.BlockSpec((tb, c), lambda i: (i, 0))
    return pl.pallas_call(
        body, name="pool_window_fwd", grid=(t // tb,),
        in_specs=[spec, pl.BlockSpec((POOL_HALO, c), lambda i: (jnp.maximum(i * (tb // POOL_HALO) - 1, 0), 0))],
        out_specs=spec, out_shape=jax.ShapeDtypeStruct((t, c), F32),
        compiler_params=_cparams(("parallel",)),
    )(u, u)


def pool_window_bwd(dd):
    t, c = dd.shape
    tb = _row_block(t)
    nrb = t // tb
    length = tb + POOL_HALO

    def body(d_ref, next_ref, du_ref):
        i = pl.program_id(0)
        dv = d_ref[...]
        ext = jnp.concatenate([dv, jnp.where(i < nrb - 1, next_ref[...], 0.0)], axis=0)
        cnt, grp = _pool_counts(i * tb, length, c)
        s = ext / cnt
        sums = []
        for sh in (1, 2, 4, 8):
            s = s + pltpu.roll(s, length - sh, 0)
            sums.append(s[:tb])
        du_ref[...] = _by_group(grp[:tb], sums) - dv

    spec = pl.BlockSpec((tb, c), lambda i: (i, 0))
    return pl.pallas_call(
        body, name="pool_window_bwd", grid=(nrb,),
        in_specs=[spec, pl.BlockSpec((POOL_HALO, c), lambda i: (jnp.minimum(i + 1, nrb - 1) * (tb // POOL_HALO), 0))],
        out_specs=spec, out_shape=jax.ShapeDtypeStruct((t, c), F32),
        compiler_params=_cparams(("parallel",)),
    )(dd, dd)


def _pool_mix(d, w, scale):
    return mm_nn(d, w) * scale


def pool_mix_fwd(diff, w, scale):
    t = diff.shape[0]
    tb = _row_block(t)
    gw = POOL_GROUP_W

    def body(d_ref, w_ref, s_ref, o_ref):
        o_ref[...] = _pool_mix(d_ref[...], w_ref[...], s_ref[...]).astype(o_ref.dtype)

    spec = pl.BlockSpec((tb, gw), lambda i, g: (i, g))
    return pl.pallas_call(
        body, name="pool_mix_fwd", grid=(t // tb, POOL_W // gw),
        in_specs=[spec, pl.BlockSpec((None, gw, gw), lambda i, g: (g, 0, 0)), pl.BlockSpec((1, gw), lambda i, g: (0, g))],
        out_specs=spec, out_shape=jax.ShapeDtypeStruct((t, POOL_W), BF),
        compiler_params=_cparams(("parallel", "parallel")),
    )(diff, w, scale)


def pool_mix_bwd(diff, w, scale, dp, dp_col0):
    t = diff.shape[0]
    tb = _row_block(t)
    gw = POOL_GROUP_W

    def body(d_ref, w_ref, s_ref, dp_ref, dd_ref, dw_ref, ds_ref):
        @pl.when(pl.program_id(1) == 0)
        def _():
            dw_ref[...] = jnp.zeros_like(dw_ref)
            ds_ref[...] = jnp.zeros_like(ds_ref)

        _, pull = jax.vjp(_pool_mix, d_ref[...], w_ref[...].astype(F32), s_ref[...])
        dd, dw, ds = pull(dp_ref[...])
        dd_ref[...] = dd
        dw_ref[...] += dw
        ds_ref[...] += ds

    spec = pl.BlockSpec((tb, gw), lambda g, i: (i, g))
    wspec = pl.BlockSpec((None, gw, gw), lambda g, i: (g, 0, 0))
    sspec = pl.BlockSpec((1, gw), lambda g, i: (0, g))
    return pl.pallas_call(
        body, name="pool_mix_bwd", grid=(POOL_W // gw, t // tb),
        in_specs=[spec, wspec, sspec, pl.BlockSpec((tb, gw), lambda g, i: (i, dp_col0 + g))],
        out_specs=[spec, wspec, sspec],
        out_shape=[jax.ShapeDtypeStruct((t, POOL_W), F32), jax.ShapeDtypeStruct(w.shape, F32),
                   jax.ShapeDtypeStruct(scale.shape, F32)],
        compiler_params=_cparams(("parallel", "arbitrary")),
    )(diff, w, scale, dp)


def hyb_fwd(x, mix_gain, w_qkvz, w_ab, w_u, conv_w, a_log, dt_bias, out_gain, pool_w, pool_scale, w_out, tag):
    d = x.shape[1]
    ea, eb = _gate_consts()
    nh = GDN_HEADS
    (h,) = rowwise(_rms, [_full(x)], [mix_gain], [(d, d, _col0, BF)], name=f"{tag}_norm")
    p1 = matmul(h, w_qkvz, "nn", name=f"{tag}_in_qkvz")
    ab = matmul(h, w_ab, "nn", name=f"{tag}_in_ab")
    u = matmul(h, w_u, "nn", name=f"{tag}_in_u")
    cv = conv_fwd(p1, conv_w)
    (qk,) = rowwise(_l2_silu, [(cv, HEAD_W, _colj)], [], [(2 * GDN_W, HEAD_W, _colj, F32)], ncol=2 * nh,
                    name=f"{tag}_qk_act")
    (v,) = rowwise(_silu, [(cv, GDN_W, lambda j: 2)], [], [(GDN_W, GDN_W, _col0, F32)], name=f"{tag}_v_act")
    g, beta = rowwise(_gates, [_full(ab)], [a_log, dt_bias, ea, eb],
                      [(GDN_W, GDN_W, _col0, F32), (GDN_W, GDN_W, _col0, F32)], name=f"{tag}_gates")
    o, states = gdn_fwd(qk, v, g, beta)
    (on,) = rowwise(_gdn_post, [(o, HEAD_W, _colj), (p1, HEAD_W, lambda j: 3 * nh + j)], [out_gain],
                    [(GDN_W, HEAD_W, _colj, BF)], ncol=nh, name=f"{tag}_post")
    diff = pool_window_fwd(u)
    pm = pool_mix_fwd(diff, pool_w, pool_scale)
    cat = jnp.concatenate([on, pm], axis=1)
    y = matmul(cat, w_out, "nn", add=x, name=f"{tag}_out")
    return y, (x, h, p1, ab, cv, qk, v, g, beta, o, states, diff, cat)


def hyb_bwd(dy, saved, mix_gain, w_qkvz, w_ab, w_u, conv_w, a_log, dt_bias, out_gain, pool_w, pool_scale, w_out, tag):
    x, h, p1, ab, cv, qk, v, g, beta, o, states, diff, cat = saved
    d = x.shape[1]
    nh = GDN_HEADS
    ea, eb = _gate_consts()
    dcat = matmul(dy, w_out, "nt", name=f"{tag}_out_dx")
    dw_out = matmul(cat, dy, "tn", name=f"{tag}_out_dw")
    (do, dz), (dout_gain,) = rowwise_vjp(
        _gdn_post, [(o, HEAD_W, _colj), (p1, HEAD_W, lambda j: 3 * nh + j)], [out_gain], [(dcat, HEAD_W, _colj)],
        [(GDN_W, HEAD_W, _colj, None), (GDN_W, HEAD_W, _colj, None)], ncol=nh, name=f"{tag}_post_bwd")
    ddiff, dpool_w, dpool_scale = pool_mix_bwd(diff, pool_w, pool_scale, dcat, GDN_W // POOL_GROUP_W)
    du = pool_window_bwd(ddiff)
    dq, dk, dv, dg, dbeta = gdn_bwd(qk, v, g, beta, states, do)
    (dab,), (da_log, ddt_bias) = rowwise_vjp(
        _gates, [_full(ab)], [a_log, dt_bias], [_full(dg), _full(dbeta)], [(HEAD_W, HEAD_W, _col0, None)],
        consts=[ea, eb], name=f"{tag}_gates_bwd")
    (dcq,), _ = rowwise_vjp(_l2_silu, [(cv, HEAD_W, _colj)], [], [(dq, HEAD_W, _colj)],
                            [(GDN_W, HEAD_W, _colj, None)], ncol=nh, par_grads=False, name=f"{tag}_q_act_bwd")
    (dck,), _ = rowwise_vjp(_l2_silu, [(cv, HEAD_W, lambda j: nh + j)], [], [(dk, HEAD_W, _colj)],
                            [(GDN_W, HEAD_W, _colj, None)], ncol=nh, par_grads=False, name=f"{tag}_k_act_bwd")
    (dcv,), _ = rowwise_vjp(_silu, [(cv, GDN_W, lambda j: 2)], [], [_full(dv)],
                            [(GDN_W, GDN_W, _col0, None)], par_grads=False, name=f"{tag}_v_act_bwd")
    dqkv, dconv_w = conv_bwd(p1, jnp.concatenate([dcq, dck, dcv], axis=1), conv_w)
    dp1 = jnp.concatenate([dqkv, dz], axis=1)
    dw_qkvz = matmul(h, dp1, "tn", name=f"{tag}_in_qkvz_dw")
    dw_ab = matmul(h, dab, "tn", name=f"{tag}_in_ab_dw")
    dw_u = matmul(h, du, "tn", name=f"{tag}_in_u_dw")
    dh = matmul(dp1, w_qkvz, "nt", name=f"{tag}_in_qkvz_dx")
    dh = matmul(dab, w_ab, "nt", add=dh, name=f"{tag}_in_ab_dx")
    dh = matmul(du, w_u, "nt", add=dh, name=f"{tag}_in_u_dx")
    (dx,), (dmix,) = rowwise_vjp(_rms, [_full(x)], [mix_gain], [_full(dh)], [(d, d, _col0, dy)],
                                 name=f"{tag}_norm_bwd")
    return dx, (dmix, dw_qkvz, dw_ab, dw_u, dconv_w, da_log, ddt_bias, dout_gain, dpool_w, dpool_scale, dw_out)


def loss_head(y, target):
    t, d = y.shape
    tb = _row_block(t)

    def body(y_ref, t_ref, dy_ref, loss_ref):
        @pl.when(pl.program_id(0) == 0)
        def _():
            loss_ref[...] = jnp.zeros_like(loss_ref)

        e = y_ref[...] - t_ref[...]
        dy_ref[...] = e * (1.0 / d)
        loss_ref[...] += 0.5 * jnp.sum(jnp.mean(e * e, axis=-1, keepdims=True))

    spec = pl.BlockSpec((tb, d), lambda i: (i, 0))
    return pl.pallas_call(
        body, name="loss_head", grid=(t // tb,), in_specs=[spec, spec],
        out_specs=[spec, pl.BlockSpec((8, 128), lambda i: (0, 0))],
        out_shape=[jax.ShapeDtypeStruct((t, d), F32), jax.ShapeDtypeStruct((8, 128), F32)],
        compiler_params=_cparams(("arbitrary",)),
    )(y, target)


ADAM_BLOCK_ELEMS = 256 * 1024


def _adam_rows(rows, cols):
    tb = 1024
    while tb >= 8:
        if rows % tb == 0 and tb * cols <= ADAM_BLOCK_ELEMS:
            return tb
        tb //= 2
    return rows


def adamw(w, g, m, v, name):
    rows, cols = w.shape
    tb = _adam_rows(rows, cols)
    c1 = 1.0 - ADAM_B1 ** ADAM_STEP
    c2 = 1.0 - ADAM_B2 ** ADAM_STEP

    def body(w_ref, g_ref, m_ref, v_ref, d_ref, nm_ref, nv_ref):
        gv = g_ref[...]
        nm = ADAM_B1 * m_ref[...] + (1.0 - ADAM_B1) * gv
        nv = ADAM_B2 * v_ref[...] + (1.0 - ADAM_B2) * (gv * gv)
        d_ref[...] = -ADAM_LR * ((nm / c1) / (jnp.sqrt(nv / c2) + ADAM_EPS) + ADAM_WD * w_ref[...])
        nm_ref[...] = nm
        nv_ref[...] = nv

    spec = pl.BlockSpec((tb, cols), lambda i: (i, 0))
    return pl.pallas_call(
        body, name=name, grid=(rows // tb,), in_specs=[spec] * 4, out_specs=[spec] * 3,
        out_shape=[jax.ShapeDtypeStruct((rows, cols), F32)] * 3,
        compiler_params=_cparams(("parallel",)),
    )(w, g, m, v)


LANES = 1024
HBM = pl.BlockSpec(memory_space=pltpu.HBM)


def _place():
    x, y, c = lax.axis_index("x"), lax.axis_index("y"), lax.axis_index("c")
    others = [(1 - x, y), (x, 1 - y), (1 - x, 1 - y)]
    return x, y, c, 2 * x + y, others


def _comm_call(body, name, out_shape, n_sems, *args):
    return pl.pallas_call(
        body, name=name, out_shape=out_shape, in_specs=[HBM] * len(args),
        out_specs=HBM,
        scratch_shapes=[pltpu.SemaphoreType.DMA((n_sems,)), pltpu.SemaphoreType.DMA((n_sems,)),
                        pltpu.SemaphoreType.DMA],
        compiler_params=pltpu.CompilerParams(has_side_effects=True),
    )(*args)


def all_gather_chips(xl):
    def body(x_ref, out_ref, send_sems, recv_sems, local_sem):
        x, y, c, me, others = _place()
        sib = (x, y, 1 - c)

        def copy(k, src, dst, to):
            return pltpu.make_async_remote_copy(src_ref=src, dst_ref=dst, send_sem=send_sems.at[k],
                                                recv_sem=recv_sems.at[k], device_id=to, device_id_type=MESH)

        mine = pltpu.make_async_copy(x_ref, out_ref.at[me], local_sem)
        mine.start()
        first = [copy(k, x_ref.at[c], out_ref.at[me, c], (*chip, c)) for k, chip in enumerate(others)]
        for cp in first:
            cp.start()
        passed = []
        for k, (cx, cy) in enumerate(others):
            slot = out_ref.at[2 * cx + cy, c]
            copy(k, slot, slot, sib).wait_recv()
            fwd = copy(3 + k, slot, slot, sib)
            fwd.start()
            passed.append(fwd)
        for k, (cx, cy) in enumerate(others):
            slot = out_ref.at[2 * cx + cy, 1 - c]
            copy(3 + k, slot, slot, sib).wait_recv()
        for cp in first + passed:
            cp.wait_send()
        mine.wait()

    return _comm_call(body, "all_gather_chips", jax.ShapeDtypeStruct((N_CHIPS,) + xl.shape, xl.dtype), 6, xl)


def pair_split(g):
    def body(g_ref, out_ref, send_sems, recv_sems, local_sem):
        x, y, c, _, _ = _place()
        mine = pltpu.make_async_copy(g_ref.at[:, c], out_ref.at[c], local_sem)
        mine.start()
        swap = pltpu.make_async_remote_copy(src_ref=g_ref.at[:, 1 - c], dst_ref=out_ref.at[c], send_sem=send_sems.at[0],
                                            recv_sem=recv_sems.at[0], device_id=(x, y, 1 - c), device_id_type=MESH)
        swap.start()
        got = out_ref.at[1 - c]
        pltpu.make_async_remote_copy(src_ref=got, dst_ref=got, send_sem=send_sems.at[0], recv_sem=recv_sems.at[0],
                                     device_id=(x, y, 1 - c), device_id_type=MESH).wait_recv()
        swap.wait_send()
        mine.wait()

    n, two, h, lanes = g.shape
    return _comm_call(body, "pair_split", jax.ShapeDtypeStruct((two, n, h, lanes), g.dtype), 1, g)


def chip_scatter(s):
    def body(s_ref, out_ref, send_sems, recv_sems, local_sem):
        x, y, c, me, others = _place()
        mine = pltpu.make_async_copy(s_ref.at[me], out_ref.at[me], local_sem)
        mine.start()

        def copy(k, src, dst, to):
            return pltpu.make_async_remote_copy(src_ref=src, dst_ref=dst, send_sem=send_sems.at[k],
                                                recv_sem=recv_sems.at[k], device_id=to, device_id_type=MESH)

        sends = [copy(k, s_ref.at[2 * cx + cy], out_ref.at[me], (cx, cy, c)) for k, (cx, cy) in enumerate(others)]
        for cp in sends:
            cp.start()
        for k, (cx, cy) in enumerate(others):
            slot = out_ref.at[2 * cx + cy]
            copy(k, slot, slot, (cx, cy, c)).wait_recv()
        for cp in sends:
            cp.wait_send()
        mine.wait()

    return _comm_call(body, "chip_scatter", jax.ShapeDtypeStruct(s.shape, s.dtype), 3, s)


def pair_join(f):
    def body(f_ref, out_ref, send_sems, recv_sems, local_sem):
        x, y, c, _, _ = _place()
        mine = pltpu.make_async_copy(f_ref, out_ref.at[c], local_sem)
        mine.start()
        swap = pltpu.make_async_remote_copy(src_ref=f_ref, dst_ref=out_ref.at[c], send_sem=send_sems.at[0],
                                            recv_sem=recv_sems.at[0], device_id=(x, y, 1 - c), device_id_type=MESH)
        swap.start()
        got = out_ref.at[1 - c]
        pltpu.make_async_remote_copy(src_ref=got, dst_ref=got, send_sem=send_sems.at[0], recv_sem=recv_sems.at[0],
                                     device_id=(x, y, 1 - c), device_id_type=MESH).wait_recv()
        swap.wait_send()
        mine.wait()

    return _comm_call(body, "pair_join", jax.ShapeDtypeStruct((2,) + f.shape, f.dtype), 1, f)


SUM_ROWS = 256


def _tile_rows(h, cap):
    tb = cap
    while tb > 8 and h % tb:
        tb //= 2
    assert h % tb == 0, (h, tb)
    return tb


def sum_blocks(a, name):
    n, h, lanes = a.shape
    tb = _tile_rows(h, SUM_ROWS)

    def body(a_ref, o_ref):
        acc = a_ref[0]
        for k in range(1, n):
            acc = acc + a_ref[k]
        o_ref[...] = acc

    return pl.pallas_call(
        body, name=name, grid=(h // tb,), in_specs=[pl.BlockSpec((n, tb, lanes), lambda i: (0, i, 0))],
        out_specs=pl.BlockSpec((tb, lanes), lambda i: (i, 0)), out_shape=jax.ShapeDtypeStruct((h, lanes), a.dtype),
        compiler_params=_cparams(("parallel",)),
    )(a)


def reduce_scatter_chips(g):
    n, two, h, lanes = g.shape
    pair = pair_split(g)
    s = sum_blocks(pair.reshape(two, n * h, lanes), "pair_sum").reshape(n, h, lanes)
    got = chip_scatter(s)
    f = sum_blocks(got, "chip_sum")
    return pair_join(f).reshape(two * h * lanes)


SHARDED = {
    "ffn1_w_gate": 2, "ffn1_w_up": 2, "ffn1_w_down": 1, "ffn2_w_gate": 2, "ffn2_w_up": 2, "ffn2_w_down": 1,
    "hyb_w_in": 2, "gdn_conv": 2, "pool_w": 2, "hyb_w_out": 1, "mla_w_in": 1, "mla_q_norm": 1, "mla_kv_norm": 1,
    "mla_w_q_up": 2, "mla_w_kv_up": 2, "mla_w_out": 1,
}
EXACT = ("gdn_conv", "mla_q_norm", "mla_kv_norm")
EVEN_ONLY = ("hyb_w_in", "gdn_conv", "gdn_a_log", "gdn_dt_bias", "gdn_out_norm", "pool_w", "pool_scale", "hyb_w_out")
WEIGHTS = ["ffn1_norm", "ffn1_w_gate", "ffn1_w_up", "ffn1_w_down", "mix_norm", "ffn2_norm", "ffn2_w_gate",
           "ffn2_w_up", "ffn2_w_down", "hyb_w_in", "gdn_conv", "gdn_a_log", "gdn_dt_bias", "gdn_out_norm", "pool_w",
           "pool_scale", "hyb_w_out", "mla_w_in", "mla_q_norm", "mla_kv_norm", "mla_w_q_up", "mla_w_kv_up",
           "mla_q_head_norm", "mla_k_head_norm", "mla_w_out"]


def _pad_rows(flat, mult):
    n = flat.shape[0]
    rows = -(-n // LANES)
    rows = -(-rows // mult) * mult
    return jnp.pad(flat, (0, rows * LANES - n)), rows


def gather_weights(w):
    parts = []
    for name in SHARDED:
        a = w[name]
        parts.append(lax.bitcast_convert_type(a, BF).reshape(-1) if name in EXACT else a.astype(BF).reshape(-1))
    flat, rows = _pad_rows(jnp.concatenate(parts), 32)
    got = all_gather_chips(flat.reshape(2, rows // 2, LANES)).reshape(N_CHIPS, rows * LANES)
    full, off = {}, 0
    for name, axis in SHARDED.items():
        a = w[name]
        n = a.size * (2 if name in EXACT else 1)
        seg = got[:, off:off + n]
        off += n
        if name in EXACT:
            seg = lax.bitcast_convert_type(seg.reshape((N_CHIPS,) + a.shape + (2,)), F32)
        else:
            seg = seg.reshape((N_CHIPS,) + a.shape)
        seg = jnp.moveaxis(seg, 0, axis)
        full[name] = seg.reshape(a.shape[:axis] + (N_CHIPS * a.shape[axis],) + a.shape[axis + 1:])
    return full


def reduce_grads(grads, loss_tile):
    sections = []
    for k in range(N_CHIPS):
        parts = []
        for name in WEIGHTS:
            g = grads[name]
            if name in SHARDED:
                axis = SHARDED[name]
                size = g.shape[axis] // N_CHIPS
                g = lax.slice_in_dim(g, k * size, (k + 1) * size, axis=axis)
            parts.append(g.reshape(-1))
        parts.append(loss_tile.reshape(-1))
        flat, rows = _pad_rows(jnp.concatenate(parts), 16)
        sections.append(flat.reshape(2, rows // 2, LANES))
    total = reduce_scatter_chips(jnp.stack(sections))
    out, off = {}, 0
    for name in WEIGHTS:
        shape = grads[name].shape
        if name in SHARDED:
            axis = SHARDED[name]
            shape = shape[:axis] + (shape[axis] // N_CHIPS,) + shape[axis + 1:]
        n = int(np.prod(shape))
        out[name] = total[off:off + n].reshape(shape)
        off += n
    return out, total[off]


def _as2d(a):
    return a.reshape(-1, a.shape[-1])


def kernel(x, positions, ffn1_norm, ffn1_w_gate, ffn1_w_up, ffn1_w_down, mix_norm, ffn2_norm, ffn2_w_gate, ffn2_w_up, ffn2_w_down, hyb_w_in, gdn_conv, gdn_a_log, gdn_dt_bias, gdn_out_norm, pool_w, pool_scale, hyb_w_out, mla_w_in, mla_q_norm, mla_kv_norm, mla_w_q_up, mla_w_kv_up, mla_q_head_norm, mla_k_head_norm, mla_w_out, loss_target, m_ffn1_norm, m_ffn1_w_gate, m_ffn1_w_up, m_ffn1_w_down, m_mix_norm, m_ffn2_norm, m_ffn2_w_gate, m_ffn2_w_up, m_ffn2_w_down, m_hyb_w_in, m_gdn_conv, m_gdn_a_log, m_gdn_dt_bias, m_gdn_out_norm, m_pool_w, m_pool_scale, m_hyb_w_out, m_mla_w_in, m_mla_q_norm, m_mla_kv_norm, m_mla_w_q_up, m_mla_w_kv_up, m_mla_q_head_norm, m_mla_k_head_norm, m_mla_w_out, v_ffn1_norm, v_ffn1_w_gate, v_ffn1_w_up, v_ffn1_w_down, v_mix_norm, v_ffn2_norm, v_ffn2_w_gate, v_ffn2_w_up, v_ffn2_w_down, v_hyb_w_in, v_gdn_conv, v_gdn_a_log, v_gdn_dt_bias, v_gdn_out_norm, v_pool_w, v_pool_scale, v_hyb_w_out, v_mla_w_in, v_mla_q_norm, v_mla_kv_norm, v_mla_w_q_up, v_mla_w_kv_up, v_mla_q_head_norm, v_mla_k_head_norm, v_mla_w_out):
    given = dict(locals())
    w = {n: given[n] for n in WEIGHTS}
    moments_m = {n: given["m_" + n] for n in WEIGHTS}
    moments_v = {n: given["v_" + n] for n in WEIGHTS}
    t = x.shape[1]
    xs = x.reshape(t, D_MODEL)
    full = gather_weights(w)
    n_even = hyb_w_in.shape[0]
    n_odd = mla_w_in.shape[0]

    _, sign, inv_freq = _rope_consts()
    cos, sin = rope_tables(positions.reshape(t, 1), inv_freq, sign)

    def ffn_args(which, layer):
        return (w[f"{which}_norm"][layer][None], full[f"{which}_w_gate"][layer], full[f"{which}_w_up"][layer],
                full[f"{which}_w_down"][layer])

    def hyb_args(i):
        win = full["hyb_w_in"][i]
        cut = 4 * GDN_W
        w_ab = jnp.pad(win[:, cut:cut + 2 * GDN_HEADS], ((0, 0), (0, HEAD_W - 2 * GDN_HEADS)))
        return (w["mix_norm"][2 * i][None], win[:, :cut], w_ab, win[:, cut + 2 * GDN_HEADS:],
                jnp.pad(full["gdn_conv"][i], ((0, HALO - CONV_K), (0, 0))), jnp.repeat(w["gdn_a_log"][i], HEAD_W)[None],
                jnp.repeat(w["gdn_dt_bias"][i], HEAD_W)[None], w["gdn_out_norm"][i][None], full["pool_w"][i],
                w["pool_scale"][i][None], full["hyb_w_out"][i])

    def mla_args(i):
        w_in = jnp.pad(full["mla_w_in"][i], ((0, 0), (0, ODD_IN_PAD - ODD_IN)))
        w_q = jnp.pad(full["mla_w_q_up"][i].reshape(LORA, MLA_HEADS, QK_HEAD),
                      ((0, 0), (0, 0), (0, 2 * HEAD_W - QK_HEAD))).reshape(LORA, MLA_HEADS * 2 * HEAD_W)
        return (cos, sin, w["mix_norm"][2 * i + 1][None], w_in, full["mla_q_norm"][i][None], full["mla_kv_norm"][i][None],
                w_q, full["mla_w_kv_up"][i], w["mla_q_head_norm"][i], w["mla_k_head_norm"][i], full["mla_w_out"][i])

    saved = []
    h = xs
    for layer in range(DEPTH):
        i = layer // 2
        h, s1 = ffn_fwd(h, *ffn_args("ffn1", layer), f"l{layer}_ffn1")
        if layer % 2 == 0:
            h, s2 = hyb_fwd(h, *hyb_args(i), f"l{layer}_hyb")
        else:
            h, s2 = mla_fwd(h, *mla_args(i), f"l{layer}_mla")
        h, s3 = ffn_fwd(h, *ffn_args("ffn2", layer), f"l{layer}_ffn2")
        saved.append((s1, s2, s3))

    dh, loss_tile = loss_head(h, loss_target.reshape(t, D_MODEL))

    per_layer = {n: [None] * (DEPTH if n.startswith(("ffn", "mix")) else (n_even if n in EVEN_ONLY else n_odd))
                 for n in WEIGHTS}
    for layer in reversed(range(DEPTH)):
        i = layer // 2
        s1, s2, s3 = saved[layer]
        dh, dg, dwg, dwu, dwd = ffn_bwd(dh, s3, *ffn_args("ffn2", layer), f"l{layer}_ffn2")
        per_layer["ffn2_norm"][layer], per_layer["ffn2_w_gate"][layer] = dg[0], dwg
        per_layer["ffn2_w_up"][layer], per_layer["ffn2_w_down"][layer] = dwu, dwd
        if layer % 2 == 0:
            dh, g = hyb_bwd(dh, s2, *hyb_args(i), f"l{layer}_hyb")
            dmix, dw_qkvz, dw_ab, dw_u, dconv, da_log, ddt, dog, dpw, dps, dwo = g
            per_layer["hyb_w_in"][i] = jnp.concatenate([dw_qkvz, dw_ab[:, :2 * GDN_HEADS], dw_u], axis=1)
            per_layer["gdn_conv"][i] = dconv[:CONV_K]
            per_layer["gdn_a_log"][i] = da_log.reshape(GDN_HEADS, HEAD_W).sum(axis=1)
            per_layer["gdn_dt_bias"][i] = ddt.reshape(GDN_HEADS, HEAD_W).sum(axis=1)
            per_layer["gdn_out_norm"][i], per_layer["pool_w"][i] = dog[0], dpw
            per_layer["pool_scale"][i], per_layer["hyb_w_out"][i] = dps[0], dwo
        else:
            dh, g = mla_bwd(dh, s2, *mla_args(i), f"l{layer}_mla")
            dmix, dw_in, dqg, dkvg, dwq, dwkv, dqh, dkh, dwo = g
            per_layer["mla_w_in"][i] = dw_in[:, :ODD_IN]
            per_layer["mla_q_norm"][i], per_layer["mla_kv_norm"][i] = dqg[0], dkvg[0]
            per_layer["mla_w_q_up"][i] = dwq.reshape(LORA, MLA_HEADS, 2 * HEAD_W)[:, :, :QK_HEAD].reshape(LORA, -1)
            per_layer["mla_w_kv_up"][i] = dwkv
            per_layer["mla_q_head_norm"][i], per_layer["mla_k_head_norm"][i] = dqh, dkh
            per_layer["mla_w_out"][i] = dwo
        per_layer["mix_norm"][layer] = dmix[0]
        dh, dg, dwg, dwu, dwd = ffn_bwd(dh, s1, *ffn_args("ffn1", layer), f"l{layer}_ffn1")
        per_layer["ffn1_norm"][layer], per_layer["ffn1_w_gate"][layer] = dg[0], dwg
        per_layer["ffn1_w_up"][layer], per_layer["ffn1_w_down"][layer] = dwu, dwd

    grads_full = {n: jnp.stack(per_layer[n]) for n in WEIGHTS}
    grads, loss = reduce_grads(grads_full, loss_tile)

    deltas, new_m, new_v = {}, {}, {}
    for n in WEIGHTS:
        d2, m2, v2 = adamw(_as2d(w[n]), _as2d(grads[n]), _as2d(moments_m[n]), _as2d(moments_v[n]), f"adamw_{n}")
        deltas[n], new_m[n], new_v[n] = d2.reshape(w[n].shape), m2.reshape(w[n].shape), v2.reshape(w[n].shape)
    return (loss, dh.reshape(x.shape), *[grads[n] for n in WEIGHTS], *[deltas[n] for n in WEIGHTS],
            *[new_m[n] for n in WEIGHTS], *[new_v[n] for n in WEIGHTS])
```

```python
import functools
import math

import jax
import jax.numpy as jnp
import numpy as np
from jax import lax
from jax.experimental import pallas as pl
from jax.experimental.pallas import tpu as pltpu

F32 = jnp.float32
BF = jnp.bfloat16
HI = lax.Precision.HIGHEST
MESH = pl.DeviceIdType.MESH

D_MODEL = 2048
D_FF = 4096
DEPTH = 4
GDN_HEADS = 8
HEAD_W = 128
GDN_W = GDN_HEADS * HEAD_W
CONV_K = 4
CHUNK = 64
POOL_WINDOWS = (2, 4, 8, 16)
POOL_W = 1024
POOL_GROUP_W = 256
EVEN_IN = 5136
MLA_HEADS = 16
LORA = 512
ROPE = 64
QK_HEAD = HEAD_W + ROPE
ODD_IN = 2 * LORA + ROPE
ODD_IN_PAD = 2 * LORA + HEAD_W
ROPE_THETA = 10000.0
EPS = 1e-6
N_CHIPS = 4

ADAM_LR = 0.001
ADAM_B1 = 0.9
ADAM_B2 = 0.999
ADAM_EPS = 1e-08
ADAM_WD = 0.01
ADAM_STEP = 10

ROW_BLOCK = 256
COL_BLOCK = 1024
MM_TILE = 1024
MM_TILE_K = 2048
VMEM_LIMIT = 56 * 1024 * 1024


def _cparams(sem=None):
    return pltpu.CompilerParams(dimension_semantics=sem, vmem_limit_bytes=VMEM_LIMIT)


def _bdot(a, b, ca, cb):
    return lax.dot_general(a.astype(BF), b.astype(BF), (((ca,), (cb,)), ((), ())),
                           preferred_element_type=F32)


@jax.custom_vjp
def mm_nn(a, b):
    return _bdot(a, b, 1, 0)


@jax.custom_vjp
def mm_nt(a, b):
    return _bdot(a, b, 1, 1)


@jax.custom_vjp
def mm_tn(a, b):
    return _bdot(a, b, 0, 0)


mm_nn.defvjp(lambda a, b: (mm_nn(a, b), (a, b)), lambda r, g: (mm_nt(g, r[1]), mm_tn(r[0], g)))
mm_nt.defvjp(lambda a, b: (mm_nt(a, b), (a, b)), lambda r, g: (mm_nn(g, r[1]), mm_tn(g, r[0])))
mm_tn.defvjp(lambda a, b: (mm_tn(a, b), (a, b)), lambda r, g: (mm_nt(r[1], g), mm_nn(r[0], g)))


def hdot(a, b):
    return lax.dot_general(a, b, (((1,), (0,)), ((), ())), precision=HI, preferred_element_type=F32)


def hdot_nt(a, b):
    return lax.dot_general(a, b, (((1,), (1,)), ((), ())), precision=HI, preferred_element_type=F32)


def _sigmoid(x):
    return 1.0 / (1.0 + jnp.exp(-x))


def _silu(x):
    return x * _sigmoid(x)


def _softplus(x):
    return jnp.maximum(x, 0.0) + jnp.log(1.0 + jnp.exp(-jnp.abs(x)))


def _tile(dim, cap):
    if dim <= cap:
        return dim
    t = (cap // 128) * 128
    while t >= 128:
        if dim % t == 0:
            return t
        t -= 128
    raise ValueError(f"no tile for {dim}")


def _as_tuple(r):
    return tuple(r) if isinstance(r, (tuple, list)) else (r,)


def _row_block(t):
    return min(ROW_BLOCK, t)


def matmul(a, b, mode, *, name, alpha=1.0, add=None, out_dtype=F32, out_shards=1):
    if mode == "nn":
        (m, k), (k2, n) = a.shape, b.shape
    elif mode == "nt":
        (m, k), (n, k2) = a.shape, b.shape
    else:
        (k, m), (k2, n) = a.shape, b.shape
    assert k == k2, (a.shape, b.shape, mode)
    tm, tn, tk = _tile(m, MM_TILE), _tile(n, MM_TILE), _tile(k, MM_TILE_K)
    nk = k // tk
    ca = 0 if mode == "tn" else 1
    cb = 1 if mode == "nt" else 0
    a_spec = (pl.BlockSpec((tk, tm), lambda i, j, kk: (kk, i)) if mode == "tn"
              else pl.BlockSpec((tm, tk), lambda i, j, kk: (i, kk)))
    b_spec = (pl.BlockSpec((tn, tk), lambda i, j, kk: (j, kk)) if mode == "nt"
              else pl.BlockSpec((tk, tn), lambda i, j, kk: (kk, j)))
    o_spec = pl.BlockSpec((tm, tn), lambda i, j, kk: (i, j))
    has_add = add is not None

    def body(*refs):
        if has_add:
            a_ref, b_ref, add_ref, o_ref, acc_ref = refs
        else:
            a_ref, b_ref, o_ref, acc_ref = refs
        kk = pl.program_id(2)

        @pl.when(kk == 0)
        def _():
            acc_ref[...] = jnp.zeros_like(acc_ref)

        acc_ref[...] += _bdot(a_ref[...], b_ref[...], ca, cb)

        @pl.when(kk == nk - 1)
        def _():
            r = acc_ref[...]
            if alpha != 1.0:
                r = r * alpha
            if has_add:
                r = r + add_ref[...].astype(F32)
            o_ref[...] = r.astype(out_dtype)

    in_specs = [a_spec, b_spec] + ([o_spec] if has_add else [])
    args = (a, b) + ((add,) if has_add else ())
    out_spec, out_shape = o_spec, (m, n)
    if out_shards > 1:
        assert not has_add and (n // out_shards) % tn == 0
        per = n // out_shards // tn
        out_spec = pl.BlockSpec((None, tm, tn), lambda i, j, kk: (j // per, i, j % per))
        out_shape = (out_shards, m, n // out_shards)
    return pl.pallas_call(
        body, name=name, grid=(m // tm, n // tn, nk), in_specs=in_specs, out_specs=out_spec,
        out_shape=jax.ShapeDtypeStruct(out_shape, out_dtype),
        scratch_shapes=[pltpu.VMEM((tm, tn), F32)],
        compiler_params=_cparams(("parallel", "parallel", "arbitrary")),
    )(*args)


def _row_spec(tb, bc, cf):
    return pl.BlockSpec((tb, bc), lambda i, j, cf=cf: (i, cf(j)))


def _par_spec(p):
    return pl.BlockSpec(p.shape, lambda i, j: (0, 0))


def rowwise(f, rows, pars, outs, *, ncol=1, name):
    t = rows[0][0].shape[0]
    tb = _row_block(t)
    nr = len(rows)

    def body(*refs):
        vals = [r[...].astype(F32) for r in refs[:nr + len(pars)]]
        res = _as_tuple(f(*vals))
        for o_ref, r in zip(refs[nr + len(pars):], res):
            o_ref[...] = r.astype(o_ref.dtype)

    return pl.pallas_call(
        body, name=name, grid=(t // tb, ncol),
        in_specs=[_row_spec(tb, bc, cf) for _, bc, cf in rows] + [_par_spec(p) for p in pars],
        out_specs=[_row_spec(tb, bc, cf) for _, bc, cf, _ in outs],
        out_shape=[jax.ShapeDtypeStruct((t, tc), dt) for tc, _, _, dt in outs],
        compiler_params=_cparams(("parallel", "arbitrary")),
    )(*[r[0] for r in rows], *pars)


def rowwise_vjp(f, rows, pars, cts, row_grads, *, ncol=1, name, par_grads=True, consts=(), grad_dtype=F32):
    t = rows[0][0].shape[0]
    tb = _row_block(t)
    consts = list(consts)
    nr, npar, nct, ncon = len(rows), len(pars), len(cts), len(consts)
    diff_rows = [i for i, g in enumerate(row_grads) if g is not None]
    adds = [row_grads[i][3] for i in diff_rows]
    add_idx = [i for i, a in enumerate(adds) if a is not None]

    def body(*refs):
        pos = 0
        row_refs = refs[pos:pos + nr]; pos += nr
        par_refs = refs[pos:pos + npar]; pos += npar
        con_refs = refs[pos:pos + ncon]; pos += ncon
        ct_refs = refs[pos:pos + nct]; pos += nct
        add_refs = refs[pos:pos + len(add_idx)]; pos += len(add_idx)
        grow_refs = refs[pos:pos + len(diff_rows)]; pos += len(diff_rows)
        gpar_refs = refs[pos:]
        row_vals = [r[...].astype(F32) for r in row_refs]
        par_vals = [r[...].astype(F32) for r in par_refs]
        con_vals = [r[...].astype(F32) for r in con_refs]

        def g(*dvals):
            rv = list(row_vals)
            for i, v in zip(diff_rows, dvals[:len(diff_rows)]):
                rv[i] = v
            pv = dvals[len(diff_rows):] if par_grads else par_vals
            return _as_tuple(f(*rv, *pv, *con_vals))

        prim = [row_vals[i] for i in diff_rows] + (par_vals if par_grads else [])
        _, pull = jax.vjp(g, *prim)
        grads = pull(tuple(c[...].astype(F32) for c in ct_refs))
        for n, ref in enumerate(grow_refs):
            gr = grads[n]
            if n in add_idx:
                gr = gr + add_refs[add_idx.index(n)][...]
            ref[...] = gr.astype(ref.dtype)
        if par_grads:
            first = jnp.logical_and(pl.program_id(0) == 0, pl.program_id(1) == 0)
            for ref, gr in zip(gpar_refs, grads[len(diff_rows):]):
                @pl.when(first)
                def _(ref=ref):
                    ref[...] = jnp.zeros_like(ref)
                ref[...] += gr

    gspecs = [row_grads[i] for i in diff_rows]
    in_specs = ([_row_spec(tb, bc, cf) for _, bc, cf in rows] + [_par_spec(p) for p in pars + consts]
                + [_row_spec(tb, bc, cf) for _, bc, cf in cts]
                + [_row_spec(tb, gspecs[i][1], gspecs[i][2]) for i in add_idx])
    out_specs = [_row_spec(tb, bc, cf) for _, bc, cf, _ in gspecs]
    out_shape = [jax.ShapeDtypeStruct((t, tc), grad_dtype) for tc, _, _, _ in gspecs]
    if par_grads:
        out_specs += [_par_spec(p) for p in pars]
        out_shape += [jax.ShapeDtypeStruct(p.shape, F32) for p in pars]
    res = pl.pallas_call(
        body, name=name, grid=(t // tb, ncol), in_specs=in_specs, out_specs=out_specs,
        out_shape=out_shape, compiler_params=_cparams(("arbitrary", "arbitrary")),
    )(*[r[0] for r in rows], *pars, *consts, *[c[0] for c in cts], *[adds[i] for i in add_idx])
    return list(res[:len(diff_rows)]), list(res[len(diff_rows):])


def _col0(j):
    return 0


def _colj(j):
    return j


def _full(a):
    return (a, a.shape[1], _col0)


def _rms(x, gain):
    return x * lax.rsqrt(jnp.mean(x * x, axis=-1, keepdims=True) + EPS) * gain


def _swiglu_act(g, u):
    return _silu(g) * u


def ffn_fwd(x, gain, wg, wu, wd, tag):
    d = x.shape[1]
    (h,) = rowwise(_rms, [_full(x)], [gain], [(d, d, _col0, BF)], name=f"{tag}_norm")
    g = matmul(h, wg, "nn", name=f"{tag}_gate")
    u = matmul(h, wu, "nn", name=f"{tag}_up")
    f = g.shape[1]
    cb = _tile(f, COL_BLOCK)
    (a,) = rowwise(_swiglu_act, [(g, cb, _colj), (u, cb, _colj)], [], [(f, cb, _colj, BF)], ncol=f // cb,
                   name=f"{tag}_act")
    y = matmul(a, wd, "nn", alpha=0.5, add=x, name=f"{tag}_down")
    return y, (x, h, g, u, a)


def ffn_bwd(dy, saved, gain, wg, wu, wd, tag):
    x, h, g, u, a = saved
    d, f = x.shape[1], g.shape[1]
    da = matmul(dy, wd, "nt", alpha=0.5, name=f"{tag}_down_dx")
    dwd = matmul(a, dy, "tn", alpha=0.5, name=f"{tag}_down_dw")
    cb = _tile(f, COL_BLOCK)
    (dg, du), _ = rowwise_vjp(_swiglu_act, [(g, cb, _colj), (u, cb, _colj)], [], [(da, cb, _colj)],
                              [(f, cb, _colj, None), (f, cb, _colj, None)], ncol=f // cb,
                              name=f"{tag}_act_bwd", par_grads=False, grad_dtype=BF)
    dwg = matmul(h, dg, "tn", name=f"{tag}_gate_dw", out_shards=N_CHIPS)
    dwu = matmul(h, du, "tn", name=f"{tag}_up_dw", out_shards=N_CHIPS)
    dh = matmul(dg, wg, "nt", name=f"{tag}_gate_dx")
    dh = matmul(du, wu, "nt", add=dh, name=f"{tag}_up_dx")
    (dx,), (dgain,) = rowwise_vjp(_rms, [_full(x)], [gain], [_full(dh)], [(d, d, _col0, dy)],
                                  name=f"{tag}_norm_bwd")
    return dx, dgain, dwg, dwu, dwd


def rope_tables(positions, inv_freq, sign):
    t = positions.shape[0]
    tb = _row_block(t)

    def body(pos_ref, f_ref, s_ref, c_ref, sn_ref):
        ang = pos_ref[...].astype(F32) * f_ref[...]
        live = jnp.abs(s_ref[...])
        c_ref[...] = jnp.cos(ang) * live
        sn_ref[...] = jnp.sin(ang) * s_ref[...]

    return pl.pallas_call(
        body, name="rope_tables", grid=(t // tb,),
        in_specs=[pl.BlockSpec((tb, 1), lambda i: (i, 0)), pl.BlockSpec((1, HEAD_W), lambda i: (0, 0)),
                  pl.BlockSpec((1, HEAD_W), lambda i: (0, 0))],
        out_specs=[pl.BlockSpec((tb, HEAD_W), lambda i: (i, 0))] * 2,
        out_shape=[jax.ShapeDtypeStruct((t, HEAD_W), F32)] * 2,
        compiler_params=_cparams(("parallel",)),
    )(positions, inv_freq, sign)


def _rope(p, c, s, swap):
    return p * c + hdot(p, swap) * s


def _pe_norm(pe, gp):
    return pe * lax.rsqrt(jnp.sum(pe * pe, axis=-1, keepdims=True) * (1.0 / ROPE) + EPS) * gp


def _q_head(nope, pe, c, s, gn, gp, swap):
    return _rms(nope, gn), _rope(_pe_norm(pe, gp), c, s, swap)


def _k_head(nope, v, gn):
    return _rms(nope, gn), v


def _kpe_head(pe, c, s, gp, swap):
    return _rope(_pe_norm(pe, gp), c, s, swap)


ATT_BLOCK = 1024
ATT_SCALE = QK_HEAD ** -0.5
NEG = float(np.finfo(np.float32).min)


def _att_block(t):
    return min(ATT_BLOCK, t)


def _scores(qn, qp, kn, kp, diag):
    q = jnp.concatenate([qn, qp], axis=1)
    k = jnp.concatenate([kn, kp], axis=1)
    s = lax.dot_general(q, k, (((1,), (1,)), ((), ())), preferred_element_type=F32) * ATT_SCALE
    rows = lax.broadcasted_iota(jnp.int32, s.shape, 0)
    cols = lax.broadcasted_iota(jnp.int32, s.shape, 1)
    return jnp.where(jnp.logical_or(jnp.logical_not(diag), rows >= cols), s, NEG), q, k


def attention_fwd(qn, qp, kn, kp, v):
    t = qn.shape[0]
    h = qn.shape[1] // HEAD_W
    tq = _att_block(t)
    nq = t // tq

    def body(qn_ref, qp_ref, kn_ref, kp_ref, v_ref, o_ref, lse_ref, m_ref, l_ref, acc_ref):
        i, j = pl.program_id(1), pl.program_id(2)

        @pl.when(j == 0)
        def _():
            m_ref[...] = jnp.full_like(m_ref, NEG)
            l_ref[...] = jnp.zeros_like(l_ref)
            acc_ref[...] = jnp.zeros_like(acc_ref)

        @pl.when(j <= i)
        def _():
            s, _, _ = _scores(qn_ref[...], qp_ref[...], kn_ref[...], kp_ref[...], j == i)
            m_new = jnp.maximum(m_ref[...], jnp.max(s, axis=-1, keepdims=True))
            a = jnp.exp(m_ref[...] - m_new)
            p = jnp.exp(s - m_new)
            l_ref[...] = a * l_ref[...] + jnp.sum(p, axis=-1, keepdims=True)
            acc_ref[...] = a * acc_ref[...] + jnp.dot(p.astype(BF), v_ref[...], preferred_element_type=F32)
            m_ref[...] = m_new

        @pl.when(j == nq - 1)
        def _():
            o_ref[...] = acc_ref[...] / l_ref[...]
            lse_ref[...] = m_ref[...] + jnp.log(l_ref[...])

    qspec = pl.BlockSpec((tq, HEAD_W), lambda hh, i, j: (i, hh))
    kspec = pl.BlockSpec((tq, HEAD_W), lambda hh, i, j: (jnp.minimum(i, j), hh))
    kpspec = pl.BlockSpec((tq, HEAD_W), lambda hh, i, j: (jnp.minimum(i, j), 0))
    return pl.pallas_call(
        body, name="attention_fwd", grid=(h, nq, nq),
        in_specs=[qspec, qspec, kspec, kpspec, kspec],
        out_specs=[qspec, pl.BlockSpec((None, tq, 1), lambda hh, i, j: (hh, i, 0))],
        out_shape=[jax.ShapeDtypeStruct((t, h * HEAD_W), F32), jax.ShapeDtypeStruct((h, t, 1), F32)],
        scratch_shapes=[pltpu.VMEM((tq, 1), F32), pltpu.VMEM((tq, 1), F32), pltpu.VMEM((tq, HEAD_W), F32)],
        compiler_params=_cparams(("parallel", "parallel", "arbitrary")),
    )(qn, qp, kn, kp, v)


def attention_delta(o, do):
    t = o.shape[0]
    h = o.shape[1] // HEAD_W
    tq = _att_block(t)

    def body(o_ref, do_ref, d_ref):
        d_ref[...] = jnp.sum(o_ref[...] * do_ref[...], axis=-1, keepdims=True)

    spec = pl.BlockSpec((tq, HEAD_W), lambda hh, i: (i, hh))
    return pl.pallas_call(
        body, name="attention_delta", grid=(h, t // tq), in_specs=[spec, spec],
        out_specs=pl.BlockSpec((None, tq, 1), lambda hh, i: (hh, i, 0)),
        out_shape=jax.ShapeDtypeStruct((h, t, 1), F32),
        compiler_params=_cparams(("parallel", "parallel")),
    )(o, do)


def _att_grads(qn_ref, qp_ref, kn_ref, kp_ref, v_ref, do_ref, lse_ref, dl_ref, diag):
    s, q, k = _scores(qn_ref[...], qp_ref[...], kn_ref[...], kp_ref[...], diag)
    p = jnp.exp(s - lse_ref[...])
    do = do_ref[...].astype(BF)
    dp = lax.dot_general(do, v_ref[...], (((1,), (1,)), ((), ())), preferred_element_type=F32)
    ds = p * (dp - dl_ref[...]) * ATT_SCALE
    return p, ds, q, k, do


def attention_bwd_q(qn, qp, kn, kp, v, do, lse, delta):
    t = qn.shape[0]
    h = qn.shape[1] // HEAD_W
    tq = _att_block(t)
    nq = t // tq

    def body(qn_ref, qp_ref, kn_ref, kp_ref, v_ref, do_ref, lse_ref, dl_ref, dqn_ref, dqp_ref, acc_ref):
        i, j = pl.program_id(1), pl.program_id(2)

        @pl.when(j == 0)
        def _():
            acc_ref[...] = jnp.zeros_like(acc_ref)

        @pl.when(j <= i)
        def _():
            _, ds, _, k, _ = _att_grads(qn_ref, qp_ref, kn_ref, kp_ref, v_ref, do_ref, lse_ref, dl_ref, j == i)
            acc_ref[...] += jnp.dot(ds.astype(BF), k, preferred_element_type=F32)

        @pl.when(j == nq - 1)
        def _():
            dqn_ref[...] = acc_ref[:, :HEAD_W]
            dqp_ref[...] = acc_ref[:, HEAD_W:]

    qspec = pl.BlockSpec((tq, HEAD_W), lambda hh, i, j: (i, hh))
    kspec = pl.BlockSpec((tq, HEAD_W), lambda hh, i, j: (jnp.minimum(i, j), hh))
    kpspec = pl.BlockSpec((tq, HEAD_W), lambda hh, i, j: (jnp.minimum(i, j), 0))
    vec = pl.BlockSpec((None, tq, 1), lambda hh, i, j: (hh, i, 0))
    return pl.pallas_call(
        body, name="attention_bwd_q", grid=(h, nq, nq),
        in_specs=[qspec, qspec, kspec, kpspec, kspec, qspec, vec, vec],
        out_specs=[qspec, qspec],
        out_shape=[jax.ShapeDtypeStruct((t, h * HEAD_W), F32)] * 2,
        scratch_shapes=[pltpu.VMEM((tq, 2 * HEAD_W), F32)],
        compiler_params=_cparams(("parallel", "parallel", "arbitrary")),
    )(qn, qp, kn, kp, v, do, lse, delta)


def attention_bwd_kv(qn, qp, kn, kp, v, do, lse, delta):
    t = qn.shape[0]
    h = qn.shape[1] // HEAD_W
    tq = _att_block(t)
    nq = t // tq

    def body(qn_ref, qp_ref, kn_ref, kp_ref, v_ref, do_ref, lse_ref, dl_ref, dkn_ref, dkp_ref, dv_ref,
             dk_acc, dv_acc):
        j, hh, i = pl.program_id(0), pl.program_id(1), pl.program_id(2)

        @pl.when(i == 0)
        def _():
            dk_acc[...] = jnp.zeros_like(dk_acc)
            dv_acc[...] = jnp.zeros_like(dv_acc)

        @pl.when(jnp.logical_and(i == 0, hh == 0))
        def _():
            dkp_ref[...] = jnp.zeros_like(dkp_ref)

        @pl.when(i >= j)
        def _():
            p, ds, q, _, do = _att_grads(qn_ref, qp_ref, kn_ref, kp_ref, v_ref, do_ref, lse_ref, dl_ref, j == i)
            dv_acc[...] += lax.dot_general(p.astype(BF), do, (((0,), (0,)), ((), ())), preferred_element_type=F32)
            dk_acc[...] += lax.dot_general(ds.astype(BF), q, (((0,), (0,)), ((), ())), preferred_element_type=F32)

        @pl.when(i == nq - 1)
        def _():
            dkn_ref[...] = dk_acc[:, :HEAD_W]
            dkp_ref[...] += dk_acc[:, HEAD_W:]
            dv_ref[...] = dv_acc[...]

    qspec = pl.BlockSpec((tq, HEAD_W), lambda j, hh, i: (jnp.maximum(i, j), hh))
    kspec = pl.BlockSpec((tq, HEAD_W), lambda j, hh, i: (j, hh))
    kpspec = pl.BlockSpec((tq, HEAD_W), lambda j, hh, i: (j, 0))
    vec = pl.BlockSpec((None, tq, 1), lambda j, hh, i: (hh, jnp.maximum(i, j), 0))
    return pl.pallas_call(
        body, name="attention_bwd_kv", grid=(nq, h, nq),
        in_specs=[qspec, qspec, kspec, kpspec, kspec, qspec, vec, vec],
        out_specs=[kspec, kpspec, kspec],
        out_shape=[jax.ShapeDtypeStruct((t, h * HEAD_W), F32), jax.ShapeDtypeStruct((t, HEAD_W), F32),
                   jax.ShapeDtypeStruct((t, h * HEAD_W), F32)],
        scratch_shapes=[pltpu.VMEM((tq, 2 * HEAD_W), F32), pltpu.VMEM((tq, HEAD_W), F32)],
        compiler_params=_cparams(("parallel", "arbitrary", "arbitrary")),
    )(qn, qp, kn, kp, v, do, lse, delta)


def _even(j):
    return 2 * j


def _odd(j):
    return 2 * j + 1


def _rope_consts():
    lane = np.arange(HEAD_W)
    half = ROPE // 2
    swap = np.zeros((HEAD_W, HEAD_W), np.float32)
    swap[lane[:half] + half, lane[:half]] = 1.0
    swap[lane[:half], lane[:half] + half] = 1.0
    sign = np.where(lane < half, -1.0, np.where(lane < ROPE, 1.0, 0.0)).astype(np.float32)[None]
    inv_freq = ROPE_THETA ** (-jnp.arange(0, ROPE, 2, dtype=F32) / ROPE)
    inv_freq = jnp.concatenate([inv_freq, inv_freq, jnp.zeros((HEAD_W - ROPE,), F32)])[None]
    return jnp.asarray(swap), jnp.asarray(sign), inv_freq


def _pad_gain(g):
    return g[None, :HEAD_W], jnp.pad(g[HEAD_W:], (0, HEAD_W - ROPE))[None]


def mla_fwd(x, cos, sin, mix_gain, w_in, q_gain, kv_gain, w_q, w_kv, qh_gain, kh_gain, w_out, tag):
    d = x.shape[1]
    nh = MLA_HEADS
    swap = _rope_consts()[0]
    (h,) = rowwise(_rms, [_full(x)], [mix_gain], [(d, d, _col0, BF)], name=f"{tag}_norm")
    proj = matmul(h, w_in, "nn", name=f"{tag}_in")
    (qlat,) = rowwise(_rms, [(proj, LORA, _col0)], [q_gain], [(LORA, LORA, _col0, BF)], name=f"{tag}_qnorm")
    (kvlat,) = rowwise(_rms, [(proj, LORA, lambda j: 1)], [kv_gain], [(LORA, LORA, _col0, BF)],
                       name=f"{tag}_kvnorm")
    q = matmul(qlat, w_q, "nn", name=f"{tag}_qup")
    kv = matmul(kvlat, w_kv, "nn", name=f"{tag}_kvup")
    qgn, qgp = _pad_gain(qh_gain)
    kgn, kgp = _pad_gain(kh_gain)
    w = nh * HEAD_W
    qn, qp = rowwise(_q_head, [(q, HEAD_W, _even), (q, HEAD_W, _odd), _full(cos), _full(sin)], [qgn, qgp, swap],
                     [(w, HEAD_W, _colj, BF), (w, HEAD_W, _colj, BF)], ncol=nh, name=f"{tag}_qhead")
    kn, v = rowwise(_k_head, [(kv, HEAD_W, _even), (kv, HEAD_W, _odd)], [kgn],
                    [(w, HEAD_W, _colj, BF), (w, HEAD_W, _colj, BF)], ncol=nh, name=f"{tag}_khead")
    (kp,) = rowwise(_kpe_head, [(proj, HEAD_W, lambda j: 2 * LORA // HEAD_W), _full(cos), _full(sin)], [kgp, swap],
                    [(HEAD_W, HEAD_W, _col0, BF)], name=f"{tag}_kpe")
    o, lse = attention_fwd(qn, qp, kn, kp, v)
    y = matmul(o, w_out, "nn", add=x, name=f"{tag}_out")
    return y, (x, h, proj, qlat, kvlat, q, kv, qn, qp, kn, kp, v, o, lse)


def mla_bwd(dy, saved, cos, sin, mix_gain, w_in, q_gain, kv_gain, w_q, w_kv, qh_gain, kh_gain, w_out, tag):
    x, h, proj, qlat, kvlat, q, kv, qn, qp, kn, kp, v, o, lse = saved
    d = x.shape[1]
    nh = MLA_HEADS
    w = nh * HEAD_W
    swap = _rope_consts()[0]
    qgn, qgp = _pad_gain(qh_gain)
    kgn, kgp = _pad_gain(kh_gain)
    do = matmul(dy, w_out, "nt", name=f"{tag}_out_dx")
    dw_out = matmul(o, dy, "tn", name=f"{tag}_out_dw")
    delta = attention_delta(o, do)
    dqn, dqp = attention_bwd_q(qn, qp, kn, kp, v, do, lse, delta)
    dkn, dkp, dv = attention_bwd_kv(qn, qp, kn, kp, v, do, lse, delta)
    (dq_a, dq_b), (dqgn, dqgp) = rowwise_vjp(
        _q_head, [(q, HEAD_W, _even), (q, HEAD_W, _odd), _full(cos), _full(sin)], [qgn, qgp],
        [(dqn, HEAD_W, _colj), (dqp, HEAD_W, _colj)],
        [(w, HEAD_W, _colj, None), (w, HEAD_W, _colj, None), None, None], ncol=nh, consts=[swap],
        name=f"{tag}_qhead_bwd")
    dq = _interleave(dq_a, dq_b)
    (dkv_a, dkv_b), (dkgn,) = rowwise_vjp(
        _k_head, [(kv, HEAD_W, _even), (kv, HEAD_W, _odd)], [kgn], [(dkn, HEAD_W, _colj), (dv, HEAD_W, _colj)],
        [(w, HEAD_W, _colj, None), (w, HEAD_W, _colj, None)], ncol=nh, name=f"{tag}_khead_bwd")
    dkv = _interleave(dkv_a, dkv_b)
    (dpe,), (dkgp,) = rowwise_vjp(
        _kpe_head, [(proj, HEAD_W, lambda j: 2 * LORA // HEAD_W), _full(cos), _full(sin)], [kgp], [_full(dkp)],
        [(HEAD_W, HEAD_W, _col0, None), None, None], consts=[swap], name=f"{tag}_kpe_bwd")
    dw_q = matmul(qlat, dq, "tn", name=f"{tag}_qup_dw")
    dw_kv = matmul(kvlat, dkv, "tn", name=f"{tag}_kvup_dw")
    dqlat = matmul(dq, w_q, "nt", name=f"{tag}_qup_dx")
    dkvlat = matmul(dkv, w_kv, "nt", name=f"{tag}_kvup_dx")
    (dpq,), (dq_gain,) = rowwise_vjp(_rms, [(proj, LORA, _col0)], [q_gain], [_full(dqlat)],
                                     [(LORA, LORA, _col0, None)], name=f"{tag}_qnorm_bwd")
    (dpkv,), (dkv_gain,) = rowwise_vjp(_rms, [(proj, LORA, lambda j: 1)], [kv_gain], [_full(dkvlat)],
                                       [(LORA, LORA, _col0, None)], name=f"{tag}_kvnorm_bwd")
    dproj = jnp.concatenate([dpq, dpkv, dpe], axis=1)
    dw_in = matmul(h, dproj, "tn", name=f"{tag}_in_dw")
    dh = matmul(dproj, w_in, "nt", name=f"{tag}_in_dx")
    (dx,), (dmix,) = rowwise_vjp(_rms, [_full(x)], [mix_gain], [_full(dh)], [(d, d, _col0, dy)],
                                 name=f"{tag}_norm_bwd")
    dqh = jnp.concatenate([dqgn[0], dqgp[0, :ROPE]])
    dkh = jnp.concatenate([dkgn[0], dkgp[0, :ROPE]])
    return dx, (dmix, dw_in, dq_gain, dkv_gain, dw_q, dw_kv, dqh, dkh, dw_out)


def _interleave(a, b):
    t, w = a.shape
    n = w // HEAD_W
    return jnp.stack([a.reshape(t, n, HEAD_W), b.reshape(t, n, HEAD_W)], axis=2).reshape(t, 2 * w)


GDN_HEADS_PER_STEP = 4
CONV_COLS = 512
HALO = 8


def conv_fwd(x, w):
    t = x.shape[0]
    c = w.shape[1]
    tb = _row_block(t)
    assert t % tb == 0

    def body(x_ref, prev_ref, w_ref, y_ref):
        i = pl.program_id(1)
        xv = x_ref[...]
        prev = jnp.where(i > 0, prev_ref[...], 0.0)
        ext = jnp.concatenate([prev, xv], axis=0)
        acc = xv * w_ref[CONV_K - 1:CONV_K, :]
        for s in range(1, CONV_K):
            acc = acc + pltpu.roll(ext, s, 0)[HALO:] * w_ref[CONV_K - 1 - s:CONV_K - s, :]
        y_ref[...] = acc

    spec = pl.BlockSpec((tb, CONV_COLS), lambda j, i: (i, j))
    return pl.pallas_call(
        body, name="conv_fwd", grid=(c // CONV_COLS, t // tb),
        in_specs=[spec, pl.BlockSpec((HALO, CONV_COLS), lambda j, i: (jnp.maximum(i * (tb // HALO) - 1, 0), j)),
                  pl.BlockSpec((HALO, CONV_COLS), lambda j, i: (0, j))],
        out_specs=spec, out_shape=jax.ShapeDtypeStruct((t, c), F32),
        compiler_params=_cparams(("parallel", "parallel")),
    )(x, x, w)


def conv_bwd(x, dy, w):
    t = x.shape[0]
    c = w.shape[1]
    tb = _row_block(t)
    nrb = t // tb

    def body(x_ref, prev_ref, dy_ref, next_ref, w_ref, dx_ref, dw_ref):
        i = pl.program_id(1)
        ext_x = jnp.concatenate([jnp.where(i > 0, prev_ref[...], 0.0), x_ref[...]], axis=0)
        dyv = dy_ref[...]
        ext_dy = jnp.concatenate([dyv, jnp.where(i < nrb - 1, next_ref[...], 0.0)], axis=0)

        @pl.when(i == 0)
        def _():
            dw_ref[...] = jnp.zeros_like(dw_ref)

        acc = dyv * w_ref[CONV_K - 1:CONV_K, :]
        dw_ref[CONV_K - 1:CONV_K, :] += jnp.sum(dyv * x_ref[...], axis=0, keepdims=True)
        for s in range(1, CONV_K):
            acc = acc + pltpu.roll(ext_dy, tb + HALO - s, 0)[:tb] * w_ref[CONV_K - 1 - s:CONV_K - s, :]
            dw_ref[CONV_K - 1 - s:CONV_K - s, :] += jnp.sum(dyv * pltpu.roll(ext_x, s, 0)[HALO:], axis=0, keepdims=True)
        dx_ref[...] = acc

    spec = pl.BlockSpec((tb, CONV_COLS), lambda j, i: (i, j))
    wspec = pl.BlockSpec((HALO, CONV_COLS), lambda j, i: (0, j))
    return pl.pallas_call(
        body, name="conv_bwd", grid=(c // CONV_COLS, nrb),
        in_specs=[spec, pl.BlockSpec((HALO, CONV_COLS), lambda j, i: (jnp.maximum(i * (tb // HALO) - 1, 0), j)),
                  spec, pl.BlockSpec((HALO, CONV_COLS), lambda j, i: (jnp.minimum(i + 1, nrb - 1) * (tb // HALO), j)),
                  wspec],
        out_specs=[spec, wspec],
        out_shape=[jax.ShapeDtypeStruct((t, c), F32), jax.ShapeDtypeStruct((HALO, c), F32)],
        compiler_params=_cparams(("parallel", "arbitrary")),
    )(x, x, dy, dy, w)


def _l2_silu(c):
    a = _silu(c)
    return a * lax.rsqrt(jnp.sum(a * a, axis=-1, keepdims=True) + EPS)


def _gates(ab, a_log, dt_bias, ea, eb):
    g = -jnp.exp(a_log) * _softplus(hdot(ab, ea) + dt_bias)
    return g, _sigmoid(hdot(ab, eb))


def _gate_consts():
    ea = np.zeros((HEAD_W, GDN_W), np.float32)
    eb = np.zeros((HEAD_W, GDN_W), np.float32)
    for h in range(GDN_HEADS):
        ea[h, h * HEAD_W:(h + 1) * HEAD_W] = 1.0
        eb[GDN_HEADS + h, h * HEAD_W:(h + 1) * HEAD_W] = 1.0
    return jnp.asarray(ea), jnp.asarray(eb)


def _batched_dots(precision, to_bf16):
    def raw(a, b, ca, cb):
        if to_bf16:
            a, b = a.astype(BF), b.astype(BF)
        return lax.dot_general(a, b, (((ca,), (cb,)), ((0,), (0,))), precision=precision,
                               preferred_element_type=F32)

    nn = jax.custom_vjp(lambda a, b: raw(a, b, 2, 1))
    nt = jax.custom_vjp(lambda a, b: raw(a, b, 2, 2))
    tn = jax.custom_vjp(lambda a, b: raw(a, b, 1, 1))
    nn.defvjp(lambda a, b: (nn(a, b), (a, b)), lambda r, g: (nt(g, r[1]), tn(r[0], g)))
    nt.defvjp(lambda a, b: (nt(a, b), (a, b)), lambda r, g: (nn(g, r[1]), tn(g, r[0])))
    tn.defvjp(lambda a, b: (tn(a, b), (a, b)), lambda r, g: (nt(r[1], g), nn(r[0], g)))
    return nn, nt, tn


bmm_nn, bmm_nt, bmm_tn = _batched_dots(None, True)
bh_nn, bh_nt, bh_tn = _batched_dots(HI, False)


def _gdn_chunk(q, k, v, g, beta, state):
    n = CHUNK
    nb = q.shape[0]
    ii = lax.broadcasted_iota(jnp.int32, (nb, n, n), 1)
    jj = lax.broadcasted_iota(jnp.int32, (nb, n, n), 2)
    causal = ii >= jj
    eye = (ii == jj).astype(F32)
    gc = bh_nn(causal.astype(F32), g)
    mean_w = jnp.full((nb, n, HEAD_W), 1.0 / HEAD_W, F32)
    gc_i = bh_nt(gc, mean_w)
    gc_j = bh_nt(mean_w, gc)
    decay = jnp.exp(jnp.where(causal, gc_i - gc_j, -1e30))
    kb = k * beta
    vb = v * beta
    m = jnp.where(ii > jj, bmm_nt(kb, k) * decay, 0.0)
    inv = eye - m
    pw = m
    for _ in range(5):
        pw = bh_nn(pw, pw)
        inv = bh_nn(inv, eye + pw)
    eg = jnp.exp(gc)
    u = bh_nn(inv, vb)
    w = bh_nn(inv, kb * eg)
    qs = q * (HEAD_W ** -0.5)
    attn = bmm_nt(qs, k) * decay
    g_last = bh_nn((jj == n - 1).astype(F32), gc)
    k_dec = k * jnp.exp(g_last - gc)
    v_new = u - bmm_nn(w, state)
    out = bmm_nn(qs * eg, state) + bmm_nn(attn, v_new)
    new_state = state * jnp.exp(jnp.concatenate([g_last, g_last], axis=1)) + bmm_tn(k_dec, v_new)
    return out, new_state


def _heads(ref, hb):
    return jnp.stack([ref[:, j * HEAD_W:(j + 1) * HEAD_W] for j in range(hb)])


def gdn_fwd(qk, v, g, beta):
    t = v.shape[0]
    n = t // CHUNK
    nh = GDN_HEADS

    hb = GDN_HEADS_PER_STEP
    ng = nh // hb

    def body(q_ref, k_ref, v_ref, g_ref, b_ref, o_ref, s_ref, state):
        @pl.when(pl.program_id(1) == 0)
        def _():
            state[...] = jnp.zeros_like(state)

        old = state[...]
        out, new = _gdn_chunk(*[_heads(r, hb) for r in (q_ref, k_ref, v_ref, g_ref, b_ref)], old)
        s_ref[:, 0] = old
        for j in range(hb):
            o_ref[:, j * HEAD_W:(j + 1) * HEAD_W] = out[j]
        state[...] = new

    spec = pl.BlockSpec((CHUNK, hb * HEAD_W), lambda h, c: (c, h))
    return pl.pallas_call(
        body, name="gdn_fwd", grid=(ng, n),
        in_specs=[spec, pl.BlockSpec((CHUNK, hb * HEAD_W), lambda h, c: (c, ng + h)), spec, spec, spec],
        out_specs=[spec, pl.BlockSpec((hb, 1, HEAD_W, HEAD_W), lambda h, c: (h, c, 0, 0))],
        out_shape=[jax.ShapeDtypeStruct((t, nh * HEAD_W), F32), jax.ShapeDtypeStruct((nh, n, HEAD_W, HEAD_W), F32)],
        scratch_shapes=[pltpu.VMEM((hb, HEAD_W, HEAD_W), F32)],
        compiler_params=_cparams(("parallel", "arbitrary")),
    )(qk, qk, v, g, beta)


def gdn_bwd(qk, v, g, beta, states, do):
    t = v.shape[0]
    n = t // CHUNK
    nh = GDN_HEADS

    hb = GDN_HEADS_PER_STEP
    ng = nh // hb

    def body(q_ref, k_ref, v_ref, g_ref, b_ref, s_ref, do_ref, dq_ref, dk_ref, dv_ref, dg_ref, db_ref, dstate):
        @pl.when(pl.program_id(1) == 0)
        def _():
            dstate[...] = jnp.zeros_like(dstate)

        _, pull = jax.vjp(_gdn_chunk, *[_heads(r, hb) for r in (q_ref, k_ref, v_ref, g_ref, b_ref)], s_ref[:, 0])
        grads = pull((_heads(do_ref, hb), dstate[...]))
        for ref, gr in zip((dq_ref, dk_ref, dv_ref, dg_ref, db_ref), grads[:5]):
            for j in range(hb):
                ref[:, j * HEAD_W:(j + 1) * HEAD_W] = gr[j]
        dstate[...] = grads[5]

    spec = pl.BlockSpec((CHUNK, hb * HEAD_W), lambda h, c: (n - 1 - c, h))
    return pl.pallas_call(
        body, name="gdn_bwd", grid=(ng, n),
        in_specs=[spec, pl.BlockSpec((CHUNK, hb * HEAD_W), lambda h, c: (n - 1 - c, ng + h)), spec, spec, spec,
                  pl.BlockSpec((hb, 1, HEAD_W, HEAD_W), lambda h, c: (h, n - 1 - c, 0, 0)), spec],
        out_specs=[spec] * 5,
        out_shape=[jax.ShapeDtypeStruct((t, nh * HEAD_W), F32)] * 5,
        scratch_shapes=[pltpu.VMEM((hb, HEAD_W, HEAD_W), F32)],
        compiler_params=_cparams(("parallel", "arbitrary")),
    )(qk, qk, v, g, beta, states, do)


def _gdn_post(o, z, gain):
    return _rms(o, gain) * _silu(z)


POOL_HALO = 16


def _pool_counts(t0, rows, cols):
    tt = t0 + lax.broadcasted_iota(jnp.int32, (rows, cols), 0) + 1
    grp = lax.broadcasted_iota(jnp.int32, (rows, cols), 1) // POOL_GROUP_W
    win = jnp.left_shift(2, grp)
    return jnp.minimum(tt, win).astype(F32), grp


def _by_group(grp, parts):
    out = parts[-1]
    for gi in range(len(parts) - 2, -1, -1):
        out = jnp.where(grp == gi, parts[gi], out)
    return out


def pool_window_fwd(u):
    t, c = u.shape
    tb = _row_block(t)

    def body(u_ref, prev_ref, d_ref):
        i = pl.program_id(0)
        xv = u_ref[...]
        ext = jnp.concatenate([jnp.where(i > 0, prev_ref[...], 0.0), xv], axis=0)
        sums = []
        s = ext
        for sh in (1, 2, 4, 8):
            s = s + pltpu.roll(s, sh, 0)
            sums.append(s[POOL_HALO:])
        cnt, grp = _pool_counts(i * tb, tb, c)
        d_ref[...] = _by_group(grp, sums) / cnt - xv

    spec = pl.BlockSpec((tb, c), lambda i: (i, 0))
    return pl.pallas_call(
        body, name="pool_window_fwd", grid=(t // tb,),
        in_specs=[spec, pl.BlockSpec((POOL_HALO, c), lambda i: (jnp.maximum(i * (tb // POOL_HALO) - 1, 0), 0))],
        out_specs=spec, out_shape=jax.ShapeDtypeStruct((t, c), F32),
        compiler_params=_cparams(("parallel",)),
    )(u, u)


def pool_window_bwd(dd):
    t, c = dd.shape
    tb = _row_block(t)
    nrb = t // tb
    length = tb + POOL_HALO

    def body(d_ref, next_ref, du_ref):
        i = pl.program_id(0)
        dv = d_ref[...]
        ext = jnp.concatenate([dv, jnp.where(i < nrb - 1, next_ref[...], 0.0)], axis=0)
        cnt, grp = _pool_counts(i * tb, length, c)
        s = ext / cnt
        sums = []
        for sh in (1, 2, 4, 8):
            s = s + pltpu.roll(s, length - sh, 0)
            sums.append(s[:tb])
        du_ref[...] = _by_group(grp[:tb], sums) - dv

    spec = pl.BlockSpec((tb, c), lambda i: (i, 0))
    return pl.pallas_call(
        body, name="pool_window_bwd", grid=(nrb,),
        in_specs=[spec, pl.BlockSpec((POOL_HALO, c), lambda i: (jnp.minimum(i + 1, nrb - 1) * (tb // POOL_HALO), 0))],
        out_specs=spec, out_shape=jax.ShapeDtypeStruct((t, c), F32),
        compiler_params=_cparams(("parallel",)),
    )(dd, dd)


def _pool_mix(d, w, scale):
    return mm_nn(d, w) * scale


def pool_mix_fwd(diff, w, scale):
    t = diff.shape[0]
    tb = _row_block(t)
    gw = POOL_GROUP_W

    def body(d_ref, w_ref, s_ref, o_ref):
        o_ref[...] = _pool_mix(d_ref[...], w_ref[...], s_ref[...]).astype(o_ref.dtype)

    spec = pl.BlockSpec((tb, gw), lambda i, g: (i, g))
    return pl.pallas_call(
        body, name="pool_mix_fwd", grid=(t // tb, POOL_W // gw),
        in_specs=[spec, pl.BlockSpec((None, gw, gw), lambda i, g: (g, 0, 0)), pl.BlockSpec((1, gw), lambda i, g: (0, g))],
        out_specs=spec, out_shape=jax.ShapeDtypeStruct((t, POOL_W), BF),
        compiler_params=_cparams(("parallel", "parallel")),
    )(diff, w, scale)


def pool_mix_bwd(diff, w, scale, dp, dp_col0):
    t = diff.shape[0]
    tb = _row_block(t)
    gw = POOL_GROUP_W

    def body(d_ref, w_ref, s_ref, dp_ref, dd_ref, dw_ref, ds_ref):
        @pl.when(pl.program_id(1) == 0)
        def _():
            dw_ref[...] = jnp.zeros_like(dw_ref)
            ds_ref[...] = jnp.zeros_like(ds_ref)

        _, pull = jax.vjp(_pool_mix, d_ref[...], w_ref[...].astype(F32), s_ref[...])
        dd, dw, ds = pull(dp_ref[...])
        dd_ref[...] = dd
        dw_ref[...] += dw
        ds_ref[...] += ds

    spec = pl.BlockSpec((tb, gw), lambda g, i: (i, g))
    wspec = pl.BlockSpec((None, gw, gw), lambda g, i: (g, 0, 0))
    sspec = pl.BlockSpec((1, gw), lambda g, i: (0, g))
    return pl.pallas_call(
        body, name="pool_mix_bwd", grid=(POOL_W // gw, t // tb),
        in_specs=[spec, wspec, sspec, pl.BlockSpec((tb, gw), lambda g, i: (i, dp_col0 + g))],
        out_specs=[spec, wspec, sspec],
        out_shape=[jax.ShapeDtypeStruct((t, POOL_W), F32), jax.ShapeDtypeStruct(w.shape, F32),
                   jax.ShapeDtypeStruct(scale.shape, F32)],
        compiler_params=_cparams(("parallel", "arbitrary")),
    )(diff, w, scale, dp)


def hyb_fwd(x, mix_gain, w_qkvz, w_ab, w_u, conv_w, a_log, dt_bias, out_gain, pool_w, pool_scale, w_out, tag):
    d = x.shape[1]
    ea, eb = _gate_consts()
    nh = GDN_HEADS
    (h,) = rowwise(_rms, [_full(x)], [mix_gain], [(d, d, _col0, BF)], name=f"{tag}_norm")
    p1 = matmul(h, w_qkvz, "nn", name=f"{tag}_in_qkvz")
    ab = matmul(h, w_ab, "nn", name=f"{tag}_in_ab")
    u = matmul(h, w_u, "nn", name=f"{tag}_in_u")
    cv = conv_fwd(p1, conv_w)
    (qk,) = rowwise(_l2_silu, [(cv, HEAD_W, _colj)], [], [(2 * GDN_W, HEAD_W, _colj, F32)], ncol=2 * nh,
                    name=f"{tag}_qk_act")
    (v,) = rowwise(_silu, [(cv, GDN_W, lambda j: 2)], [], [(GDN_W, GDN_W, _col0, F32)], name=f"{tag}_v_act")
    g, beta = rowwise(_gates, [_full(ab)], [a_log, dt_bias, ea, eb],
                      [(GDN_W, GDN_W, _col0, F32), (GDN_W, GDN_W, _col0, F32)], name=f"{tag}_gates")
    o, states = gdn_fwd(qk, v, g, beta)
    (on,) = rowwise(_gdn_post, [(o, HEAD_W, _colj), (p1, HEAD_W, lambda j: 3 * nh + j)], [out_gain],
                    [(GDN_W, HEAD_W, _colj, BF)], ncol=nh, name=f"{tag}_post")
    diff = pool_window_fwd(u)
    pm = pool_mix_fwd(diff, pool_w, pool_scale)
    cat = jnp.concatenate([on, pm], axis=1)
    y = matmul(cat, w_out, "nn", add=x, name=f"{tag}_out")
    return y, (x, h, p1, ab, cv, qk, v, g, beta, o, states, diff, cat)


def hyb_bwd(dy, saved, mix_gain, w_qkvz, w_ab, w_u, conv_w, a_log, dt_bias, out_gain, pool_w, pool_scale, w_out, tag):
    x, h, p1, ab, cv, qk, v, g, beta, o, states, diff, cat = saved
    d = x.shape[1]
    nh = GDN_HEADS
    ea, eb = _gate_consts()
    dcat = matmul(dy, w_out, "nt", name=f"{tag}_out_dx")
    dw_out = matmul(cat, dy, "tn", name=f"{tag}_out_dw")
    (do, dz), (dout_gain,) = rowwise_vjp(
        _gdn_post, [(o, HEAD_W, _colj), (p1, HEAD_W, lambda j: 3 * nh + j)], [out_gain], [(dcat, HEAD_W, _colj)],
        [(GDN_W, HEAD_W, _colj, None), (GDN_W, HEAD_W, _colj, None)], ncol=nh, name=f"{tag}_post_bwd")
    ddiff, dpool_w, dpool_scale = pool_mix_bwd(diff, pool_w, pool_scale, dcat, GDN_W // POOL_GROUP_W)
    du = pool_window_bwd(ddiff)
    dq, dk, dv, dg, dbeta = gdn_bwd(qk, v, g, beta, states, do)
    (dab,), (da_log, ddt_bias) = rowwise_vjp(
        _gates, [_full(ab)], [a_log, dt_bias], [_full(dg), _full(dbeta)], [(HEAD_W, HEAD_W, _col0, None)],
        consts=[ea, eb], name=f"{tag}_gates_bwd")
    (dcq,), _ = rowwise_vjp(_l2_silu, [(cv, HEAD_W, _colj)], [], [(dq, HEAD_W, _colj)],
                            [(GDN_W, HEAD_W, _colj, None)], ncol=nh, par_grads=False, name=f"{tag}_q_act_bwd")
    (dck,), _ = rowwise_vjp(_l2_silu, [(cv, HEAD_W, lambda j: nh + j)], [], [(dk, HEAD_W, _colj)],
                            [(GDN_W, HEAD_W, _colj, None)], ncol=nh, par_grads=False, name=f"{tag}_k_act_bwd")
    (dcv,), _ = rowwise_vjp(_silu, [(cv, GDN_W, lambda j: 2)], [], [_full(dv)],
                            [(GDN_W, GDN_W, _col0, None)], par_grads=False, name=f"{tag}_v_act_bwd")
    dqkv, dconv_w = conv_bwd(p1, jnp.concatenate([dcq, dck, dcv], axis=1), conv_w)
    dp1 = jnp.concatenate([dqkv, dz], axis=1)
    dw_qkvz = matmul(h, dp1, "tn", name=f"{tag}_in_qkvz_dw")
    dw_ab = matmul(h, dab, "tn", name=f"{tag}_in_ab_dw")
    dw_u = matmul(h, du, "tn", name=f"{tag}_in_u_dw")
    dh = matmul(dp1, w_qkvz, "nt", name=f"{tag}_in_qkvz_dx")
    dh = matmul(dab, w_ab, "nt", add=dh, name=f"{tag}_in_ab_dx")
    dh = matmul(du, w_u, "nt", add=dh, name=f"{tag}_in_u_dx")
    (dx,), (dmix,) = rowwise_vjp(_rms, [_full(x)], [mix_gain], [_full(dh)], [(d, d, _col0, dy)],
                                 name=f"{tag}_norm_bwd")
    return dx, (dmix, dw_qkvz, dw_ab, dw_u, dconv_w, da_log, ddt_bias, dout_gain, dpool_w, dpool_scale, dw_out)


def loss_head(y, target):
    t, d = y.shape
    tb = _row_block(t)

    def body(y_ref, t_ref, dy_ref, loss_ref):
        @pl.when(pl.program_id(0) == 0)
        def _():
            loss_ref[...] = jnp.zeros_like(loss_ref)

        e = y_ref[...] - t_ref[...]
        dy_ref[...] = e * (1.0 / d)
        loss_ref[...] += 0.5 * jnp.sum(jnp.mean(e * e, axis=-1, keepdims=True))

    spec = pl.BlockSpec((tb, d), lambda i: (i, 0))
    return pl.pallas_call(
        body, name="loss_head", grid=(t // tb,), in_specs=[spec, spec],
        out_specs=[spec, pl.BlockSpec((8, 128), lambda i: (0, 0))],
        out_shape=[jax.ShapeDtypeStruct((t, d), F32), jax.ShapeDtypeStruct((8, 128), F32)],
        compiler_params=_cparams(("arbitrary",)),
    )(y, target)


ADAM_BLOCK_ELEMS = 256 * 1024


def _adam_rows(rows, cols):
    tb = 1024
    while tb >= 8:
        if rows % tb == 0 and tb * cols <= ADAM_BLOCK_ELEMS:
            return tb
        tb //= 2
    return rows


def adamw(w, g, m, v, name):
    rows, cols = w.shape
    tb = _adam_rows(rows, cols)
    c1 = 1.0 - ADAM_B1 ** ADAM_STEP
    c2 = 1.0 - ADAM_B2 ** ADAM_STEP

    def body(w_ref, g_ref, m_ref, v_ref, d_ref, nm_ref, nv_ref):
        gv = g_ref[...]
        nm = ADAM_B1 * m_ref[...] + (1.0 - ADAM_B1) * gv
        nv = ADAM_B2 * v_ref[...] + (1.0 - ADAM_B2) * (gv * gv)
        d_ref[...] = -ADAM_LR * ((nm / c1) / (jnp.sqrt(nv / c2) + ADAM_EPS) + ADAM_WD * w_ref[...])
        nm_ref[...] = nm
        nv_ref[...] = nv

    spec = pl.BlockSpec((tb, cols), lambda i: (i, 0))
    return pl.pallas_call(
        body, name=name, grid=(rows // tb,), in_specs=[spec] * 4, out_specs=[spec] * 3,
        out_shape=[jax.ShapeDtypeStruct((rows, cols), F32)] * 3,
        compiler_params=_cparams(("parallel",)),
    )(w, g, m, v)


LANES = 1024
HBM = pl.BlockSpec(memory_space=pltpu.HBM)


def _place():
    x, y, c = lax.axis_index("x"), lax.axis_index("y"), lax.axis_index("c")
    others = [(1 - x, y), (x, 1 - y), (1 - x, 1 - y)]
    return x, y, c, 2 * x + y, others


def _comm_call(body, name, out_shape, n_sems, *args):
    return pl.pallas_call(
        body, name=name, out_shape=out_shape, in_specs=[HBM] * len(args),
        out_specs=[HBM] * len(out_shape),
        scratch_shapes=[pltpu.SemaphoreType.DMA((n_sems,)), pltpu.SemaphoreType.DMA((n_sems,))],
        compiler_params=pltpu.CompilerParams(has_side_effects=True),
    )(*args)


AG_CHUNKS = 4


def all_gather_chips(xl):
    two, h, lanes = xl.shape
    assert h % (AG_CHUNKS * 16) == 0
    hc = h // AG_CHUNKS

    def body(x_ref, out_ref, send_sems, recv_sems):
        x, y, c, me, others = _place()
        sib = (x, y, 1 - c)

        def copy(k, src, dst, to):
            return pltpu.make_async_remote_copy(src_ref=src, dst_ref=dst, send_sem=send_sems.at[k],
                                                recv_sem=recv_sems.at[k], device_id=to, device_id_type=MESH)

        first, passed = [], []
        for q in range(AG_CHUNKS):
            rows = pl.ds(q * hc, hc)
            for k, chip in enumerate(others):
                cp = copy(q * 3 + k, x_ref.at[c, rows], out_ref.at[me, c, rows], (*chip, c))
                cp.start()
                first.append(cp)
        base = 3 * AG_CHUNKS
        for q in range(AG_CHUNKS):
            rows = pl.ds(q * hc, hc)
            for k, (cx, cy) in enumerate(others):
                slot = out_ref.at[2 * cx + cy, c, rows]
                copy(q * 3 + k, slot, slot, sib).wait_recv()
                fwd = copy(base + q * 3 + k, slot, slot, sib)
                fwd.start()
                passed.append(fwd)
        for q in range(AG_CHUNKS):
            rows = pl.ds(q * hc, hc)
            for k, (cx, cy) in enumerate(others):
                slot = out_ref.at[2 * cx + cy, 1 - c, rows]
                copy(base + q * 3 + k, slot, slot, sib).wait_recv()
        for cp in first + passed:
            cp.wait_send()

    return _comm_call(body, "all_gather_chips", [jax.ShapeDtypeStruct((N_CHIPS,) + xl.shape, xl.dtype)],
                      6 * AG_CHUNKS, xl)[0]


def pair_swap(gs, name):
    n = len(gs)

    def body(*refs):
        g_refs, r_refs, send_sems, recv_sems = refs[:n], refs[n:2 * n], refs[2 * n], refs[2 * n + 1]
        x, y, c, _, _ = _place()
        cps = [pltpu.make_async_remote_copy(src_ref=g.at[:, 1 - c], dst_ref=r, send_sem=send_sems.at[i],
                                            recv_sem=recv_sems.at[i], device_id=(x, y, 1 - c), device_id_type=MESH)
               for i, (g, r) in enumerate(zip(g_refs, r_refs))]
        for cp in cps:
            cp.start()
        for cp in cps:
            cp.wait()

    out = [jax.ShapeDtypeStruct((g.shape[0],) + g.shape[2:], g.dtype) for g in gs]
    return _comm_call(body, name, out, n, *gs)


def chip_scatter(ss, name):
    n = len(ss)

    def body(*refs):
        s_refs, y_refs, send_sems, recv_sems = refs[:n], refs[n:2 * n], refs[2 * n], refs[2 * n + 1]
        x, y, c, me, others = _place()
        cps = []
        for i, (s, out) in enumerate(zip(s_refs, y_refs)):
            for k, (cx, cy) in enumerate(others):
                cp = pltpu.make_async_remote_copy(src_ref=s.at[2 * cx + cy], dst_ref=out.at[me],
                                                  send_sem=send_sems.at[3 * i + k], recv_sem=recv_sems.at[3 * i + k],
                                                  device_id=(cx, cy, c), device_id_type=MESH)
                cp.start()
                cps.append(cp)
        for i, out in enumerate(y_refs):
            for k, (cx, cy) in enumerate(others):
                slot = out.at[2 * cx + cy]
                pltpu.make_async_remote_copy(src_ref=slot, dst_ref=slot, send_sem=send_sems.at[3 * i + k],
                                             recv_sem=recv_sems.at[3 * i + k], device_id=(cx, cy, c),
                                             device_id_type=MESH).wait_recv()
        for cp in cps:
            cp.wait_send()

    return _comm_call(body, name, [jax.ShapeDtypeStruct(s.shape, s.dtype) for s in ss], 3 * n, *ss)


def pair_join(fs, name):
    n = len(fs)

    def body(*refs):
        f_refs, o_refs, send_sems, recv_sems = refs[:n], refs[n:2 * n], refs[2 * n], refs[2 * n + 1]
        x, y, c, _, _ = _place()
        cps = [pltpu.make_async_remote_copy(src_ref=f, dst_ref=o, send_sem=send_sems.at[i], recv_sem=recv_sems.at[i],
                                            device_id=(x, y, 1 - c), device_id_type=MESH)
               for i, (f, o) in enumerate(zip(f_refs, o_refs))]
        for cp in cps:
            cp.start()
        for cp in cps:
            cp.wait()

    return _comm_call(body, name, [jax.ShapeDtypeStruct(f.shape, f.dtype) for f in fs], n, *fs)


SUM_ROWS = 256


def _tile_rows(h, cap):
    tb = cap
    while tb > 8 and h % tb:
        tb //= 2
    assert h % tb == 0, (h, tb)
    return tb


def sum_blocks(a, name):
    n, h, lanes = a.shape
    tb = _tile_rows(h, SUM_ROWS)

    def body(a_ref, o_ref):
        acc = a_ref[0]
        for k in range(1, n):
            acc = acc + a_ref[k]
        o_ref[...] = acc

    return pl.pallas_call(
        body, name=name, grid=(h // tb,), in_specs=[pl.BlockSpec((n, tb, lanes), lambda i: (0, i, 0))],
        out_specs=pl.BlockSpec((tb, lanes), lambda i: (i, 0)), out_shape=jax.ShapeDtypeStruct((h, lanes), a.dtype),
        compiler_params=_cparams(("parallel",)),
    )(a)


def add2(a, b, name):
    r, lanes = a.shape
    tb = _tile_rows(r, SUM_ROWS)

    def body(a_ref, b_ref, o_ref):
        o_ref[...] = a_ref[...] + b_ref[...]

    spec = pl.BlockSpec((tb, lanes), lambda i: (i, 0))
    return pl.pallas_call(
        body, name=name, grid=(r // tb,), in_specs=[spec, spec], out_specs=spec,
        out_shape=jax.ShapeDtypeStruct((r, lanes), a.dtype), compiler_params=_cparams(("parallel",)),
    )(a, b)


def reduce_scatter_chips(gs, tag):
    c = lax.axis_index("c")
    me = 2 * lax.axis_index("x") + lax.axis_index("y")
    got = pair_swap(gs, f"pair_swap_{tag}")
    sums = []
    for i, (g, r) in enumerate(zip(gs, got)):
        n, _, h, lanes = g.shape
        mine = lax.dynamic_index_in_dim(g, c, axis=1, keepdims=False)
        sums.append(add2(mine.reshape(n * h, lanes), r.reshape(n * h, lanes), f"pair_sum_{tag}_{i}").reshape(n, h, lanes))
    ys = chip_scatter(sums, f"chip_scatter_{tag}")
    fs = []
    for i, (s, y) in enumerate(zip(sums, ys)):
        own = lax.dynamic_index_in_dim(s, me, axis=0, keepdims=True)
        fs.append(sum_blocks(lax.dynamic_update_slice_in_dim(y, own, me, axis=0), f"chip_sum_{tag}_{i}"))
    others = pair_join(fs, f"pair_join_{tag}")
    return [jnp.concatenate([jnp.where(c == 0, f, o), jnp.where(c == 0, o, f)]).reshape(-1)
            for f, o in zip(fs, others)]


SHARDED = {
    "ffn1_w_gate": 2, "ffn1_w_up": 2, "ffn1_w_down": 1, "ffn2_w_gate": 2, "ffn2_w_up": 2, "ffn2_w_down": 1,
    "hyb_w_in": 2, "gdn_conv": 2, "pool_w": 2, "hyb_w_out": 1, "mla_w_in": 1, "mla_q_norm": 1, "mla_kv_norm": 1,
    "mla_w_q_up": 2, "mla_w_kv_up": 2, "mla_w_out": 1,
}
EXACT = ("gdn_conv", "mla_q_norm", "mla_kv_norm")
EVEN_ONLY = ("hyb_w_in", "gdn_conv", "gdn_a_log", "gdn_dt_bias", "gdn_out_norm", "pool_w", "pool_scale", "hyb_w_out")
WEIGHTS = ["ffn1_norm", "ffn1_w_gate", "ffn1_w_up", "ffn1_w_down", "mix_norm", "ffn2_norm", "ffn2_w_gate",
           "ffn2_w_up", "ffn2_w_down", "hyb_w_in", "gdn_conv", "gdn_a_log", "gdn_dt_bias", "gdn_out_norm", "pool_w",
           "pool_scale", "hyb_w_out", "mla_w_in", "mla_q_norm", "mla_kv_norm", "mla_w_q_up", "mla_w_kv_up",
           "mla_q_head_norm", "mla_k_head_norm", "mla_w_out"]


def _pad_rows(flat, mult):
    n = flat.shape[0]
    rows = -(-n // LANES)
    rows = -(-rows // mult) * mult
    return jnp.pad(flat, (0, rows * LANES - n)), rows


def gather_weights(w):
    parts = []
    for name in SHARDED:
        a = w[name]
        parts.append(lax.bitcast_convert_type(a, BF).reshape(-1) if name in EXACT else a.astype(BF).reshape(-1))
    flat, rows = _pad_rows(jnp.concatenate(parts), 2 * 16 * AG_CHUNKS)
    mine = flat.reshape(2, rows // 2, LANES)
    me = 2 * lax.axis_index("x") + lax.axis_index("y")
    got = lax.dynamic_update_slice_in_dim(all_gather_chips(mine), mine[None], me, axis=0)
    got = got.reshape(N_CHIPS, rows * LANES)
    full, off = {}, 0
    for name, axis in SHARDED.items():
        a = w[name]
        n = a.size * (2 if name in EXACT else 1)
        seg = got[:, off:off + n]
        off += n
        if name in EXACT:
            seg = lax.bitcast_convert_type(seg.reshape((N_CHIPS,) + a.shape + (2,)), F32)
        else:
            seg = seg.reshape((N_CHIPS,) + a.shape)
        seg = jnp.moveaxis(seg, 0, axis)
        full[name] = seg.reshape(a.shape[:axis] + (N_CHIPS * a.shape[axis],) + a.shape[axis + 1:])
    return full


OWN_OPERAND = 256 * 1024
EXCHANGE_UNIT = 2 * 8 * LANES


def shard_major(g, axis):
    size = g.shape[axis] // N_CHIPS
    return jnp.moveaxis(g.reshape(g.shape[:axis] + (N_CHIPS, size) + g.shape[axis + 1:]), axis, 0)


def _exchange_operand(flat):
    n = flat.shape[1]
    padded = -(-n // EXCHANGE_UNIT) * EXCHANGE_UNIT
    return jnp.pad(flat, ((0, 0), (0, padded - n))).reshape(N_CHIPS, 2, padded // (2 * LANES), LANES)


def reduce_grads(per_layer, loss_tile):
    big, small = [], []
    for name in WEIGHTS:
        for layer, g in enumerate(per_layer[name]):
            if name in SHARDED:
                flat, shape = g.reshape(N_CHIPS, -1), g.shape[1:]
            else:
                flat, shape = jnp.broadcast_to(g.reshape(1, -1), (N_CHIPS, g.size)), g.shape
            (big if name in SHARDED and flat.shape[1] >= OWN_OPERAND else small).append((name, layer, flat, shape))
    small.append(("loss", 0, jnp.broadcast_to(loss_tile.reshape(1, -1), (N_CHIPS, loss_tile.size)), loss_tile.shape))
    misc = jnp.concatenate([flat for _, _, flat, _ in small], axis=1)
    red = reduce_scatter_chips([_exchange_operand(flat) for _, _, flat, _ in big] + [_exchange_operand(misc)], "grads")
    out = {name: [None] * len(per_layer[name]) for name in WEIGHTS}
    loss = None
    for (name, layer, flat, shape), r in zip(big, red):
        out[name][layer] = r[:flat.shape[1]].reshape(shape)
    off = 0
    for name, layer, flat, shape in small:
        piece = red[-1][off:off + flat.shape[1]].reshape(shape)
        off += flat.shape[1]
        if name == "loss":
            loss = piece[0, 0]
        else:
            out[name][layer] = piece
    return {name: jnp.stack(out[name]) for name in WEIGHTS}, loss


def _as2d(a):
    return a.reshape(-1, a.shape[-1])


def kernel(x, positions, ffn1_norm, ffn1_w_gate, ffn1_w_up, ffn1_w_down, mix_norm, ffn2_norm, ffn2_w_gate, ffn2_w_up, ffn2_w_down, hyb_w_in, gdn_conv, gdn_a_log, gdn_dt_bias, gdn_out_norm, pool_w, pool_scale, hyb_w_out, mla_w_in, mla_q_norm, mla_kv_norm, mla_w_q_up, mla_w_kv_up, mla_q_head_norm, mla_k_head_norm, mla_w_out, loss_target, m_ffn1_norm, m_ffn1_w_gate, m_ffn1_w_up, m_ffn1_w_down, m_mix_norm, m_ffn2_norm, m_ffn2_w_gate, m_ffn2_w_up, m_ffn2_w_down, m_hyb_w_in, m_gdn_conv, m_gdn_a_log, m_gdn_dt_bias, m_gdn_out_norm, m_pool_w, m_pool_scale, m_hyb_w_out, m_mla_w_in, m_mla_q_norm, m_mla_kv_norm, m_mla_w_q_up, m_mla_w_kv_up, m_mla_q_head_norm, m_mla_k_head_norm, m_mla_w_out, v_ffn1_norm, v_ffn1_w_gate, v_ffn1_w_up, v_ffn1_w_down, v_mix_norm, v_ffn2_norm, v_ffn2_w_gate, v_ffn2_w_up, v_ffn2_w_down, v_hyb_w_in, v_gdn_conv, v_gdn_a_log, v_gdn_dt_bias, v_gdn_out_norm, v_pool_w, v_pool_scale, v_hyb_w_out, v_mla_w_in, v_mla_q_norm, v_mla_kv_norm, v_mla_w_q_up, v_mla_w_kv_up, v_mla_q_head_norm, v_mla_k_head_norm, v_mla_w_out):
    given = dict(locals())
    w = {n: given[n] for n in WEIGHTS}
    moments_m = {n: given["m_" + n] for n in WEIGHTS}
    moments_v = {n: given["v_" + n] for n in WEIGHTS}
    t = x.shape[1]
    xs = x.reshape(t, D_MODEL)
    full = gather_weights(w)
    n_even = hyb_w_in.shape[0]
    n_odd = mla_w_in.shape[0]

    _, sign, inv_freq = _rope_consts()
    cos, sin = rope_tables(positions.reshape(t, 1), inv_freq, sign)

    def ffn_args(which, layer):
        return (w[f"{which}_norm"][layer][None], full[f"{which}_w_gate"][layer], full[f"{which}_w_up"][layer],
                full[f"{which}_w_down"][layer])

    def hyb_args(i):
        win = full["hyb_w_in"][i]
        cut = 4 * GDN_W
        w_ab = jnp.pad(win[:, cut:cut + 2 * GDN_HEADS], ((0, 0), (0, HEAD_W - 2 * GDN_HEADS)))
        return (w["mix_norm"][2 * i][None], win[:, :cut], w_ab, win[:, cut + 2 * GDN_HEADS:],
                jnp.pad(full["gdn_conv"][i], ((0, HALO - CONV_K), (0, 0))), jnp.repeat(w["gdn_a_log"][i], HEAD_W)[None],
                jnp.repeat(w["gdn_dt_bias"][i], HEAD_W)[None], w["gdn_out_norm"][i][None], full["pool_w"][i],
                w["pool_scale"][i][None], full["hyb_w_out"][i])

    def mla_args(i):
        w_in = jnp.pad(full["mla_w_in"][i], ((0, 0), (0, ODD_IN_PAD - ODD_IN)))
        w_q = jnp.pad(full["mla_w_q_up"][i].reshape(LORA, MLA_HEADS, QK_HEAD),
                      ((0, 0), (0, 0), (0, 2 * HEAD_W - QK_HEAD))).reshape(LORA, MLA_HEADS * 2 * HEAD_W)
        return (cos, sin, w["mix_norm"][2 * i + 1][None], w_in, full["mla_q_norm"][i][None], full["mla_kv_norm"][i][None],
                w_q, full["mla_w_kv_up"][i], w["mla_q_head_norm"][i], w["mla_k_head_norm"][i], full["mla_w_out"][i])

    saved = []
    h = xs
    for layer in range(DEPTH):
        i = layer // 2
        h, s1 = ffn_fwd(h, *ffn_args("ffn1", layer), f"l{layer}_ffn1")
        if layer % 2 == 0:
            h, s2 = hyb_fwd(h, *hyb_args(i), f"l{layer}_hyb")
        else:
            h, s2 = mla_fwd(h, *mla_args(i), f"l{layer}_mla")
        h, s3 = ffn_fwd(h, *ffn_args("ffn2", layer), f"l{layer}_ffn2")
        saved.append((s1, s2, s3))

    dh, loss_tile = loss_head(h, loss_target.reshape(t, D_MODEL))

    per_layer = {n: [None] * (DEPTH if n.startswith(("ffn", "mix")) else (n_even if n in EVEN_ONLY else n_odd))
                 for n in WEIGHTS}
    for layer in reversed(range(DEPTH)):
        i = layer // 2
        s1, s2, s3 = saved[layer]
        dh, dg, dwg, dwu, dwd = ffn_bwd(dh, s3, *ffn_args("ffn2", layer), f"l{layer}_ffn2")
        per_layer["ffn2_norm"][layer], per_layer["ffn2_w_gate"][layer] = dg[0], dwg
        per_layer["ffn2_w_up"][layer], per_layer["ffn2_w_down"][layer] = dwu, dwd
        if layer % 2 == 0:
            dh, g = hyb_bwd(dh, s2, *hyb_args(i), f"l{layer}_hyb")
            dmix, dw_qkvz, dw_ab, dw_u, dconv, da_log, ddt, dog, dpw, dps, dwo = g
            per_layer["hyb_w_in"][i] = jnp.concatenate([dw_qkvz, dw_ab[:, :2 * GDN_HEADS], dw_u], axis=1)
            per_layer["gdn_conv"][i] = dconv[:CONV_K]
            per_layer["gdn_a_log"][i] = da_log.reshape(GDN_HEADS, HEAD_W).sum(axis=1)
            per_layer["gdn_dt_bias"][i] = ddt.reshape(GDN_HEADS, HEAD_W).sum(axis=1)
            per_layer["gdn_out_norm"][i], per_layer["pool_w"][i] = dog[0], dpw
            per_layer["pool_scale"][i], per_layer["hyb_w_out"][i] = dps[0], dwo
        else:
            dh, g = mla_bwd(dh, s2, *mla_args(i), f"l{layer}_mla")
            dmix, dw_in, dqg, dkvg, dwq, dwkv, dqh, dkh, dwo = g
            per_layer["mla_w_in"][i] = dw_in[:, :ODD_IN]
            per_layer["mla_q_norm"][i], per_layer["mla_kv_norm"][i] = dqg[0], dkvg[0]
            per_layer["mla_w_q_up"][i] = dwq.reshape(LORA, MLA_HEADS, 2 * HEAD_W)[:, :, :QK_HEAD].reshape(LORA, -1)
            per_layer["mla_w_kv_up"][i] = dwkv
            per_layer["mla_q_head_norm"][i], per_layer["mla_k_head_norm"][i] = dqh, dkh
            per_layer["mla_w_out"][i] = dwo
        per_layer["mix_norm"][layer] = dmix[0]
        dh, dg, dwg, dwu, dwd = ffn_bwd(dh, s1, *ffn_args("ffn1", layer), f"l{layer}_ffn1")
        per_layer["ffn1_norm"][layer], per_layer["ffn1_w_gate"][layer] = dg[0], dwg
        per_layer["ffn1_w_up"][layer], per_layer["ffn1_w_down"][layer] = dwu, dwd

    for n, axis in SHARDED.items():
        if not n.endswith(("w_gate", "w_up")):
            per_layer[n] = [shard_major(g, axis - 1) for g in per_layer[n]]
    grads, loss = reduce_grads(per_layer, loss_tile)

    deltas, new_m, new_v = {}, {}, {}
    for n in WEIGHTS:
        d2, m2, v2 = adamw(_as2d(w[n]), _as2d(grads[n]), _as2d(moments_m[n]), _as2d(moments_v[n]), f"adamw_{n}")
        deltas[n], new_m[n], new_v[n] = d2.reshape(w[n].shape), m2.reshape(w[n].shape), v2.reshape(w[n].shape)
    return (loss, dh.reshape(x.shape), *[grads[n] for n in WEIGHTS], *[deltas[n] for n in WEIGHTS],
            *[new_m[n] for n in WEIGHTS], *[new_v[n] for n in WEIGHTS])
```

```python
import functools
import math

import jax
import jax.numpy as jnp
import numpy as np
from jax import lax
from jax.experimental import pallas as pl
from jax.experimental.pallas import tpu as pltpu

F32 = jnp.float32
BF = jnp.bfloat16
HI = lax.Precision.HIGHEST
MESH = pl.DeviceIdType.MESH

D_MODEL = 2048
D_FF = 4096
DEPTH = 4
GDN_HEADS = 8
HEAD_W = 128
GDN_W = GDN_HEADS * HEAD_W
CONV_K = 4
CHUNK = 64
POOL_WINDOWS = (2, 4, 8, 16)
POOL_W = 1024
POOL_GROUP_W = 256
EVEN_IN = 5136
MLA_HEADS = 16
LORA = 512
ROPE = 64
QK_HEAD = HEAD_W + ROPE
ODD_IN = 2 * LORA + ROPE
ODD_IN_PAD = 2 * LORA + HEAD_W
ROPE_THETA = 10000.0
EPS = 1e-6
N_CHIPS = 4

ADAM_LR = 0.001
ADAM_B1 = 0.9
ADAM_B2 = 0.999
ADAM_EPS = 1e-08
ADAM_WD = 0.01
ADAM_STEP = 10

ROW_BLOCK = 256
ROW_BLOCK_ELEMS = 256 * 1024
COL_BLOCK = 1024
MM_TILE = 1024
MM_TILE_K = 2048
VMEM_LIMIT = 56 * 1024 * 1024


def _cparams(sem=None):
    return pltpu.CompilerParams(dimension_semantics=sem, vmem_limit_bytes=VMEM_LIMIT)


def _bdot(a, b, ca, cb):
    return lax.dot_general(a.astype(BF), b.astype(BF), (((ca,), (cb,)), ((), ())),
                           preferred_element_type=F32)


@jax.custom_vjp
def mm_nn(a, b):
    return _bdot(a, b, 1, 0)


@jax.custom_vjp
def mm_nt(a, b):
    return _bdot(a, b, 1, 1)


@jax.custom_vjp
def mm_tn(a, b):
    return _bdot(a, b, 0, 0)


mm_nn.defvjp(lambda a, b: (mm_nn(a, b), (a, b)), lambda r, g: (mm_nt(g, r[1]), mm_tn(r[0], g)))
mm_nt.defvjp(lambda a, b: (mm_nt(a, b), (a, b)), lambda r, g: (mm_nn(g, r[1]), mm_tn(g, r[0])))
mm_tn.defvjp(lambda a, b: (mm_tn(a, b), (a, b)), lambda r, g: (mm_nt(r[1], g), mm_nn(r[0], g)))


def hdot(a, b):
    return lax.dot_general(a, b, (((1,), (0,)), ((), ())), precision=HI, preferred_element_type=F32)


def hdot_nt(a, b):
    return lax.dot_general(a, b, (((1,), (1,)), ((), ())), precision=HI, preferred_element_type=F32)


def _sigmoid(x):
    return 1.0 / (1.0 + jnp.exp(-x))


def _silu(x):
    return x * _sigmoid(x)


def _softplus(x):
    return jnp.maximum(x, 0.0) + jnp.log(1.0 + jnp.exp(-jnp.abs(x)))


def _tile(dim, cap):
    if dim <= cap:
        return dim
    t = (cap // 128) * 128
    while t >= 128:
        if dim % t == 0:
            return t
        t -= 128
    raise ValueError(f"no tile for {dim}")


def _as_tuple(r):
    return tuple(r) if isinstance(r, (tuple, list)) else (r,)


def _row_block(t, cols=None):
    rows = ROW_BLOCK if cols is None else max(ROW_BLOCK, ROW_BLOCK_ELEMS // cols)
    rows = min(rows, t)
    assert t % rows == 0, (t, rows)
    return rows


def matmul(a, b, mode, *, name, alpha=1.0, add=None, out_dtype=F32, out_shards=1):
    if mode == "nn":
        (m, k), (k2, n) = a.shape, b.shape
    elif mode == "nt":
        (m, k), (n, k2) = a.shape, b.shape
    else:
        (k, m), (k2, n) = a.shape, b.shape
    assert k == k2, (a.shape, b.shape, mode)
    tm, tn, tk = _tile(m, MM_TILE), _tile(n, MM_TILE), _tile(k, MM_TILE_K)
    nk = k // tk
    ca = 0 if mode == "tn" else 1
    cb = 1 if mode == "nt" else 0
    a_spec = (pl.BlockSpec((tk, tm), lambda i, j, kk: (kk, i)) if mode == "tn"
              else pl.BlockSpec((tm, tk), lambda i, j, kk: (i, kk)))
    b_spec = (pl.BlockSpec((tn, tk), lambda i, j, kk: (j, kk)) if mode == "nt"
              else pl.BlockSpec((tk, tn), lambda i, j, kk: (kk, j)))
    o_spec = pl.BlockSpec((tm, tn), lambda i, j, kk: (i, j))
    has_add = add is not None

    def body(*refs):
        if has_add:
            a_ref, b_ref, add_ref, o_ref, acc_ref = refs
        else:
            a_ref, b_ref, o_ref, acc_ref = refs
        kk = pl.program_id(2)

        @pl.when(kk == 0)
        def _():
            acc_ref[...] = jnp.zeros_like(acc_ref)

        acc_ref[...] += _bdot(a_ref[...], b_ref[...], ca, cb)

        @pl.when(kk == nk - 1)
        def _():
            r = acc_ref[...]
            if alpha != 1.0:
                r = r * alpha
            if has_add:
                r = r + add_ref[...].astype(F32)
            o_ref[...] = r.astype(out_dtype)

    in_specs = [a_spec, b_spec] + ([o_spec] if has_add else [])
    args = (a, b) + ((add,) if has_add else ())
    out_spec, out_shape = o_spec, (m, n)
    if out_shards > 1:
        assert not has_add and (n // out_shards) % tn == 0
        per = n // out_shards // tn
        out_spec = pl.BlockSpec((None, tm, tn), lambda i, j, kk: (j // per, i, j % per))
        out_shape = (out_shards, m, n // out_shards)
    return pl.pallas_call(
        body, name=name, grid=(m // tm, n // tn, nk), in_specs=in_specs, out_specs=out_spec,
        out_shape=jax.ShapeDtypeStruct(out_shape, out_dtype),
        scratch_shapes=[pltpu.VMEM((tm, tn), F32)],
        compiler_params=_cparams(("parallel", "parallel", "arbitrary")),
    )(*args)


def _row_spec(tb, bc, cf):
    return pl.BlockSpec((tb, bc), lambda i, j, cf=cf: (i, cf(j)))


def _par_spec(p):
    return pl.BlockSpec(p.shape, lambda i, j: (0, 0))


def rowwise(f, rows, pars, outs, *, ncol=1, name):
    t = rows[0][0].shape[0]
    tb = _row_block(t, max([bc for _, bc, _ in rows] + [bc for _, bc, _, _ in outs]))
    nr = len(rows)

    def body(*refs):
        vals = [r[...].astype(F32) for r in refs[:nr + len(pars)]]
        res = _as_tuple(f(*vals))
        for o_ref, r in zip(refs[nr + len(pars):], res):
            o_ref[...] = r.astype(o_ref.dtype)

    return pl.pallas_call(
        body, name=name, grid=(t // tb, ncol),
        in_specs=[_row_spec(tb, bc, cf) for _, bc, cf in rows] + [_par_spec(p) for p in pars],
        out_specs=[_row_spec(tb, bc, cf) for _, bc, cf, _ in outs],
        out_shape=[jax.ShapeDtypeStruct((t, tc), dt) for tc, _, _, dt in outs],
        compiler_params=_cparams(("parallel", "arbitrary")),
    )(*[r[0] for r in rows], *pars)


def rowwise_vjp(f, rows, pars, cts, row_grads, *, ncol=1, name, par_grads=True, consts=(), grad_dtype=F32):
    t = rows[0][0].shape[0]
    tb = _row_block(t, max([bc for _, bc, _ in rows + cts] + [g[1] for g in row_grads if g is not None]))
    consts = list(consts)
    nr, npar, nct, ncon = len(rows), len(pars), len(cts), len(consts)
    diff_rows = [i for i, g in enumerate(row_grads) if g is not None]
    adds = [row_grads[i][3] for i in diff_rows]
    add_idx = [i for i, a in enumerate(adds) if a is not None]

    def body(*refs):
        pos = 0
        row_refs = refs[pos:pos + nr]; pos += nr
        par_refs = refs[pos:pos + npar]; pos += npar
        con_refs = refs[pos:pos + ncon]; pos += ncon
        ct_refs = refs[pos:pos + nct]; pos += nct
        add_refs = refs[pos:pos + len(add_idx)]; pos += len(add_idx)
        grow_refs = refs[pos:pos + len(diff_rows)]; pos += len(diff_rows)
        gpar_refs = refs[pos:]
        row_vals = [r[...].astype(F32) for r in row_refs]
        par_vals = [r[...].astype(F32) for r in par_refs]
        con_vals = [r[...].astype(F32) for r in con_refs]

        def g(*dvals):
            rv = list(row_vals)
            for i, v in zip(diff_rows, dvals[:len(diff_rows)]):
                rv[i] = v
            pv = dvals[len(diff_rows):] if par_grads else par_vals
            return _as_tuple(f(*rv, *pv, *con_vals))

        prim = [row_vals[i] for i in diff_rows] + (par_vals if par_grads else [])
        _, pull = jax.vjp(g, *prim)
        grads = pull(tuple(c[...].astype(F32) for c in ct_refs))
        for n, ref in enumerate(grow_refs):
            gr = grads[n]
            if n in add_idx:
                gr = gr + add_refs[add_idx.index(n)][...]
            ref[...] = gr.astype(ref.dtype)
        if par_grads:
            first = jnp.logical_and(pl.program_id(0) == 0, pl.program_id(1) == 0)
            for ref, gr in zip(gpar_refs, grads[len(diff_rows):]):
                @pl.when(first)
                def _(ref=ref):
                    ref[...] = jnp.zeros_like(ref)
                ref[...] += gr

    gspecs = [row_grads[i] for i in diff_rows]
    in_specs = ([_row_spec(tb, bc, cf) for _, bc, cf in rows] + [_par_spec(p) for p in pars + consts]
                + [_row_spec(tb, bc, cf) for _, bc, cf in cts]
                + [_row_spec(tb, gspecs[i][1], gspecs[i][2]) for i in add_idx])
    out_specs = [_row_spec(tb, bc, cf) for _, bc, cf, _ in gspecs]
    out_shape = [jax.ShapeDtypeStruct((t, tc), grad_dtype) for tc, _, _, _ in gspecs]
    if par_grads:
        out_specs += [_par_spec(p) for p in pars]
        out_shape += [jax.ShapeDtypeStruct(p.shape, F32) for p in pars]
    res = pl.pallas_call(
        body, name=name, grid=(t // tb, ncol), in_specs=in_specs, out_specs=out_specs,
        out_shape=out_shape, compiler_params=_cparams(("arbitrary", "arbitrary")),
    )(*[r[0] for r in rows], *pars, *consts, *[c[0] for c in cts], *[adds[i] for i in add_idx])
    return list(res[:len(diff_rows)]), list(res[len(diff_rows):])


def _col0(j):
    return 0


def _colj(j):
    return j


def _full(a):
    return (a, a.shape[1], _col0)


def _rms(x, gain):
    return x * lax.rsqrt(jnp.mean(x * x, axis=-1, keepdims=True) + EPS) * gain


def _swiglu_act(g, u):
    return _silu(g) * u


def ffn_fwd(x, gain, wg, wu, wd, tag):
    d = x.shape[1]
    (h,) = rowwise(_rms, [_full(x)], [gain], [(d, d, _col0, BF)], name=f"{tag}_norm")
    g = matmul(h, wg, "nn", name=f"{tag}_gate")
    u = matmul(h, wu, "nn", name=f"{tag}_up")
    f = g.shape[1]
    cb = _tile(f, COL_BLOCK)
    (a,) = rowwise(_swiglu_act, [(g, cb, _colj), (u, cb, _colj)], [], [(f, cb, _colj, BF)], ncol=f // cb,
                   name=f"{tag}_act")
    y = matmul(a, wd, "nn", alpha=0.5, add=x, name=f"{tag}_down")
    return y, (x, h, g, u, a)


def ffn_bwd(dy, saved, gain, wg, wu, wd, tag):
    x, h, g, u, a = saved
    d, f = x.shape[1], g.shape[1]
    da = matmul(dy, wd, "nt", alpha=0.5, name=f"{tag}_down_dx")
    dwd = matmul(a, dy, "tn", alpha=0.5, name=f"{tag}_down_dw")
    cb = _tile(f, COL_BLOCK)
    (dg, du), _ = rowwise_vjp(_swiglu_act, [(g, cb, _colj), (u, cb, _colj)], [], [(da, cb, _colj)],
                              [(f, cb, _colj, None), (f, cb, _colj, None)], ncol=f // cb,
                              name=f"{tag}_act_bwd", par_grads=False, grad_dtype=BF)
    dwg = matmul(h, dg, "tn", name=f"{tag}_gate_dw", out_shards=N_CHIPS)
    dwu = matmul(h, du, "tn", name=f"{tag}_up_dw", out_shards=N_CHIPS)
    dh = matmul(dg, wg, "nt", name=f"{tag}_gate_dx")
    dh = matmul(du, wu, "nt", add=dh, name=f"{tag}_up_dx")
    (dx,), (dgain,) = rowwise_vjp(_rms, [_full(x)], [gain], [_full(dh)], [(d, d, _col0, dy)],
                                  name=f"{tag}_norm_bwd")
    return dx, dgain, dwg, dwu, dwd


def rope_tables(positions, inv_freq, sign):
    t = positions.shape[0]
    tb = _row_block(t)

    def body(pos_ref, f_ref, s_ref, c_ref, sn_ref):
        ang = pos_ref[...].astype(F32) * f_ref[...]
        live = jnp.abs(s_ref[...])
        c_ref[...] = jnp.cos(ang) * live
        sn_ref[...] = jnp.sin(ang) * s_ref[...]

    return pl.pallas_call(
        body, name="rope_tables", grid=(t // tb,),
        in_specs=[pl.BlockSpec((tb, 1), lambda i: (i, 0)), pl.BlockSpec((1, HEAD_W), lambda i: (0, 0)),
                  pl.BlockSpec((1, HEAD_W), lambda i: (0, 0))],
        out_specs=[pl.BlockSpec((tb, HEAD_W), lambda i: (i, 0))] * 2,
        out_shape=[jax.ShapeDtypeStruct((t, HEAD_W), F32)] * 2,
        compiler_params=_cparams(("parallel",)),
    )(positions, inv_freq, sign)


def _rope(p, c, s, swap):
    return p * c + hdot(p, swap) * s


def _pe_norm(pe, gp):
    return pe * lax.rsqrt(jnp.sum(pe * pe, axis=-1, keepdims=True) * (1.0 / ROPE) + EPS) * gp


def _q_head(nope, pe, c, s, gn, gp, swap):
    return _rms(nope, gn), _rope(_pe_norm(pe, gp), c, s, swap)


def _k_head(nope, v, gn):
    return _rms(nope, gn), v


def _kpe_head(pe, c, s, gp, swap):
    return _rope(_pe_norm(pe, gp), c, s, swap)


ATT_BLOCK = 1024
ATT_SCALE = QK_HEAD ** -0.5
NEG = float(np.finfo(np.float32).min)


def _att_block(t):
    return min(ATT_BLOCK, t)


def _scores(qn, qp, kn, kp, diag):
    q = jnp.concatenate([qn, qp], axis=1)
    k = jnp.concatenate([kn, kp], axis=1)
    s = lax.dot_general(q, k, (((1,), (1,)), ((), ())), preferred_element_type=F32) * ATT_SCALE
    if diag:
        rows = lax.broadcasted_iota(jnp.int32, s.shape, 0)
        cols = lax.broadcasted_iota(jnp.int32, s.shape, 1)
        s = jnp.where(rows >= cols, s, NEG)
    return s, q, k


def _below_or_on_diagonal(i, j, step):
    @pl.when(j < i)
    def _():
        step(False)

    @pl.when(j == i)
    def _():
        step(True)


def attention_fwd(qn, qp, kn, kp, v):
    t = qn.shape[0]
    h = qn.shape[1] // HEAD_W
    tq = _att_block(t)
    nq = t // tq

    def body(qn_ref, qp_ref, kn_ref, kp_ref, v_ref, o_ref, lse_ref, m_ref, l_ref, acc_ref):
        i, j = pl.program_id(1), pl.program_id(2)

        @pl.when(j == 0)
        def _():
            m_ref[...] = jnp.full_like(m_ref, NEG)
            l_ref[...] = jnp.zeros_like(l_ref)
            acc_ref[...] = jnp.zeros_like(acc_ref)

        def step(diag):
            s, _, _ = _scores(qn_ref[...], qp_ref[...], kn_ref[...], kp_ref[...], diag)
            m_new = jnp.maximum(m_ref[...], jnp.max(s, axis=-1, keepdims=True))
            a = jnp.exp(m_ref[...] - m_new)
            p = jnp.exp(s - m_new)
            l_ref[...] = a * l_ref[...] + jnp.sum(p, axis=-1, keepdims=True)
            acc_ref[...] = a * acc_ref[...] + jnp.dot(p.astype(BF), v_ref[...], preferred_element_type=F32)
            m_ref[...] = m_new

        _below_or_on_diagonal(i, j, step)

        @pl.when(j == nq - 1)
        def _():
            o_ref[...] = acc_ref[...] / l_ref[...]
            lse_ref[...] = m_ref[...] + jnp.log(l_ref[...])

    qspec = pl.BlockSpec((tq, HEAD_W), lambda hh, i, j: (i, hh))
    kspec = pl.BlockSpec((tq, HEAD_W), lambda hh, i, j: (jnp.minimum(i, j), hh))
    kpspec = pl.BlockSpec((tq, HEAD_W), lambda hh, i, j: (jnp.minimum(i, j), 0))
    return pl.pallas_call(
        body, name="attention_fwd", grid=(h, nq, nq),
        in_specs=[qspec, qspec, kspec, kpspec, kspec],
        out_specs=[qspec, pl.BlockSpec((None, tq, 1), lambda hh, i, j: (hh, i, 0))],
        out_shape=[jax.ShapeDtypeStruct((t, h * HEAD_W), F32), jax.ShapeDtypeStruct((h, t, 1), F32)],
        scratch_shapes=[pltpu.VMEM((tq, 1), F32), pltpu.VMEM((tq, 1), F32), pltpu.VMEM((tq, HEAD_W), F32)],
        compiler_params=_cparams(("parallel", "parallel", "arbitrary")),
    )(qn, qp, kn, kp, v)


def attention_delta(o, do):
    t = o.shape[0]
    h = o.shape[1] // HEAD_W
    tq = _att_block(t)

    def body(o_ref, do_ref, d_ref):
        d_ref[...] = jnp.sum(o_ref[...] * do_ref[...], axis=-1, keepdims=True)

    spec = pl.BlockSpec((tq, HEAD_W), lambda hh, i: (i, hh))
    return pl.pallas_call(
        body, name="attention_delta", grid=(h, t // tq), in_specs=[spec, spec],
        out_specs=pl.BlockSpec((None, tq, 1), lambda hh, i: (hh, i, 0)),
        out_shape=jax.ShapeDtypeStruct((h, t, 1), F32),
        compiler_params=_cparams(("parallel", "parallel")),
    )(o, do)


def _att_grads(qn_ref, qp_ref, kn_ref, kp_ref, v_ref, do_ref, lse_ref, dl_ref, diag):
    s, q, k = _scores(qn_ref[...], qp_ref[...], kn_ref[...], kp_ref[...], diag)
    p = jnp.exp(s - lse_ref[...])
    do = do_ref[...].astype(BF)
    dp = lax.dot_general(do, v_ref[...], (((1,), (1,)), ((), ())), preferred_element_type=F32)
    ds = p * (dp - dl_ref[...]) * ATT_SCALE
    return p, ds, q, k, do


def attention_bwd_q(qn, qp, kn, kp, v, do, lse, delta):
    t = qn.shape[0]
    h = qn.shape[1] // HEAD_W
    tq = _att_block(t)
    nq = t // tq

    def body(qn_ref, qp_ref, kn_ref, kp_ref, v_ref, do_ref, lse_ref, dl_ref, dqn_ref, dqp_ref, acc_ref):
        i, j = pl.program_id(1), pl.program_id(2)

        @pl.when(j == 0)
        def _():
            acc_ref[...] = jnp.zeros_like(acc_ref)

        def step(diag):
            _, ds, _, k, _ = _att_grads(qn_ref, qp_ref, kn_ref, kp_ref, v_ref, do_ref, lse_ref, dl_ref, diag)
            acc_ref[...] += jnp.dot(ds.astype(BF), k, preferred_element_type=F32)

        _below_or_on_diagonal(i, j, step)

        @pl.when(j == nq - 1)
        def _():
            dqn_ref[...] = acc_ref[:, :HEAD_W]
            dqp_ref[...] = acc_ref[:, HEAD_W:]

    qspec = pl.BlockSpec((tq, HEAD_W), lambda hh, i, j: (i, hh))
    kspec = pl.BlockSpec((tq, HEAD_W), lambda hh, i, j: (jnp.minimum(i, j), hh))
    kpspec = pl.BlockSpec((tq, HEAD_W), lambda hh, i, j: (jnp.minimum(i, j), 0))
    vec = pl.BlockSpec((None, tq, 1), lambda hh, i, j: (hh, i, 0))
    return pl.pallas_call(
        body, name="attention_bwd_q", grid=(h, nq, nq),
        in_specs=[qspec, qspec, kspec, kpspec, kspec, qspec, vec, vec],
        out_specs=[qspec, qspec],
        out_shape=[jax.ShapeDtypeStruct((t, h * HEAD_W), F32)] * 2,
        scratch_shapes=[pltpu.VMEM((tq, 2 * HEAD_W), F32)],
        compiler_params=_cparams(("parallel", "parallel", "arbitrary")),
    )(qn, qp, kn, kp, v, do, lse, delta)


def attention_bwd_kv(qn, qp, kn, kp, v, do, lse, delta):
    t = qn.shape[0]
    h = qn.shape[1] // HEAD_W
    tq = _att_block(t)
    nq = t // tq

    def body(qn_ref, qp_ref, kn_ref, kp_ref, v_ref, do_ref, lse_ref, dl_ref, dkn_ref, dkp_ref, dv_ref,
             dk_acc, dv_acc):
        j, hh, i = pl.program_id(0), pl.program_id(1), pl.program_id(2)

        @pl.when(i == 0)
        def _():
            dk_acc[...] = jnp.zeros_like(dk_acc)
            dv_acc[...] = jnp.zeros_like(dv_acc)

        @pl.when(jnp.logical_and(i == 0, hh == 0))
        def _():
            dkp_ref[...] = jnp.zeros_like(dkp_ref)

        def step(diag):
            p, ds, q, _, do = _att_grads(qn_ref, qp_ref, kn_ref, kp_ref, v_ref, do_ref, lse_ref, dl_ref, diag)
            dv_acc[...] += lax.dot_general(p.astype(BF), do, (((0,), (0,)), ((), ())), preferred_element_type=F32)
            dk_acc[...] += lax.dot_general(ds.astype(BF), q, (((0,), (0,)), ((), ())), preferred_element_type=F32)

        _below_or_on_diagonal(i, j, step)

        @pl.when(i == nq - 1)
        def _():
            dkn_ref[...] = dk_acc[:, :HEAD_W]
            dkp_ref[...] += dk_acc[:, HEAD_W:]
            dv_ref[...] = dv_acc[...]

    qspec = pl.BlockSpec((tq, HEAD_W), lambda j, hh, i: (jnp.maximum(i, j), hh))
    kspec = pl.BlockSpec((tq, HEAD_W), lambda j, hh, i: (j, hh))
    kpspec = pl.BlockSpec((tq, HEAD_W), lambda j, hh, i: (j, 0))
    vec = pl.BlockSpec((None, tq, 1), lambda j, hh, i: (hh, jnp.maximum(i, j), 0))
    return pl.pallas_call(
        body, name="attention_bwd_kv", grid=(nq, h, nq),
        in_specs=[qspec, qspec, kspec, kpspec, kspec, qspec, vec, vec],
        out_specs=[kspec, kpspec, kspec],
        out_shape=[jax.ShapeDtypeStruct((t, h * HEAD_W), F32), jax.ShapeDtypeStruct((t, HEAD_W), F32),
                   jax.ShapeDtypeStruct((t, h * HEAD_W), F32)],
        scratch_shapes=[pltpu.VMEM((tq, 2 * HEAD_W), F32), pltpu.VMEM((tq, HEAD_W), F32)],
        compiler_params=_cparams(("parallel", "arbitrary", "arbitrary")),
    )(qn, qp, kn, kp, v, do, lse, delta)


def _even(j):
    return 2 * j


def _odd(j):
    return 2 * j + 1


def _rope_consts():
    lane = np.arange(HEAD_W)
    half = ROPE // 2
    swap = np.zeros((HEAD_W, HEAD_W), np.float32)
    swap[lane[:half] + half, lane[:half]] = 1.0
    swap[lane[:half], lane[:half] + half] = 1.0
    sign = np.where(lane < half, -1.0, np.where(lane < ROPE, 1.0, 0.0)).astype(np.float32)[None]
    inv_freq = ROPE_THETA ** (-jnp.arange(0, ROPE, 2, dtype=F32) / ROPE)
    inv_freq = jnp.concatenate([inv_freq, inv_freq, jnp.zeros((HEAD_W - ROPE,), F32)])[None]
    return jnp.asarray(swap), jnp.asarray(sign), inv_freq


def _pad_gain(g):
    return g[None, :HEAD_W], jnp.pad(g[HEAD_W:], (0, HEAD_W - ROPE))[None]


def mla_fwd(x, cos, sin, mix_gain, w_in, q_gain, kv_gain, w_q, w_kv, qh_gain, kh_gain, w_out, tag):
    d = x.shape[1]
    nh = MLA_HEADS
    swap = _rope_consts()[0]
    (h,) = rowwise(_rms, [_full(x)], [mix_gain], [(d, d, _col0, BF)], name=f"{tag}_norm")
    proj = matmul(h, w_in, "nn", name=f"{tag}_in")
    (qlat,) = rowwise(_rms, [(proj, LORA, _col0)], [q_gain], [(LORA, LORA, _col0, BF)], name=f"{tag}_qnorm")
    (kvlat,) = rowwise(_rms, [(proj, LORA, lambda j: 1)], [kv_gain], [(LORA, LORA, _col0, BF)],
                       name=f"{tag}_kvnorm")
    q = matmul(qlat, w_q, "nn", name=f"{tag}_qup")
    kv = matmul(kvlat, w_kv, "nn", name=f"{tag}_kvup")
    qgn, qgp = _pad_gain(qh_gain)
    kgn, kgp = _pad_gain(kh_gain)
    w = nh * HEAD_W
    qn, qp = rowwise(_q_head, [(q, HEAD_W, _even), (q, HEAD_W, _odd), _full(cos), _full(sin)], [qgn, qgp, swap],
                     [(w, HEAD_W, _colj, BF), (w, HEAD_W, _colj, BF)], ncol=nh, name=f"{tag}_qhead")
    kn, v = rowwise(_k_head, [(kv, HEAD_W, _even), (kv, HEAD_W, _odd)], [kgn],
                    [(w, HEAD_W, _colj, BF), (w, HEAD_W, _colj, BF)], ncol=nh, name=f"{tag}_khead")
    (kp,) = rowwise(_kpe_head, [(proj, HEAD_W, lambda j: 2 * LORA // HEAD_W), _full(cos), _full(sin)], [kgp, swap],
                    [(HEAD_W, HEAD_W, _col0, BF)], name=f"{tag}_kpe")
    o, lse = attention_fwd(qn, qp, kn, kp, v)
    y = matmul(o, w_out, "nn", add=x, name=f"{tag}_out")
    return y, (x, h, proj, qlat, kvlat, q, kv, qn, qp, kn, kp, v, o, lse)


def mla_bwd(dy, saved, cos, sin, mix_gain, w_in, q_gain, kv_gain, w_q, w_kv, qh_gain, kh_gain, w_out, tag):
    x, h, proj, qlat, kvlat, q, kv, qn, qp, kn, kp, v, o, lse = saved
    d = x.shape[1]
    nh = MLA_HEADS
    w = nh * HEAD_W
    swap = _rope_consts()[0]
    qgn, qgp = _pad_gain(qh_gain)
    kgn, kgp = _pad_gain(kh_gain)
    do = matmul(dy, w_out, "nt", name=f"{tag}_out_dx")
    dw_out = matmul(o, dy, "tn", name=f"{tag}_out_dw")
    delta = attention_delta(o, do)
    dqn, dqp = attention_bwd_q(qn, qp, kn, kp, v, do, lse, delta)
    dkn, dkp, dv = attention_bwd_kv(qn, qp, kn, kp, v, do, lse, delta)
    (dq_a, dq_b), (dqgn, dqgp) = rowwise_vjp(
        _q_head, [(q, HEAD_W, _even), (q, HEAD_W, _odd), _full(cos), _full(sin)], [qgn, qgp],
        [(dqn, HEAD_W, _colj), (dqp, HEAD_W, _colj)],
        [(w, HEAD_W, _colj, None), (w, HEAD_W, _colj, None), None, None], ncol=nh, consts=[swap],
        name=f"{tag}_qhead_bwd")
    dq = _interleave(dq_a, dq_b)
    (dkv_a, dkv_b), (dkgn,) = rowwise_vjp(
        _k_head, [(kv, HEAD_W, _even), (kv, HEAD_W, _odd)], [kgn], [(dkn, HEAD_W, _colj), (dv, HEAD_W, _colj)],
        [(w, HEAD_W, _colj, None), (w, HEAD_W, _colj, None)], ncol=nh, name=f"{tag}_khead_bwd")
    dkv = _interleave(dkv_a, dkv_b)
    (dpe,), (dkgp,) = rowwise_vjp(
        _kpe_head, [(proj, HEAD_W, lambda j: 2 * LORA // HEAD_W), _full(cos), _full(sin)], [kgp], [_full(dkp)],
        [(HEAD_W, HEAD_W, _col0, None), None, None], consts=[swap], name=f"{tag}_kpe_bwd")
    dw_q = matmul(qlat, dq, "tn", name=f"{tag}_qup_dw")
    dw_kv = matmul(kvlat, dkv, "tn", name=f"{tag}_kvup_dw")
    dqlat = matmul(dq, w_q, "nt", name=f"{tag}_qup_dx")
    dkvlat = matmul(dkv, w_kv, "nt", name=f"{tag}_kvup_dx")
    (dpq,), (dq_gain,) = rowwise_vjp(_rms, [(proj, LORA, _col0)], [q_gain], [_full(dqlat)],
                                     [(LORA, LORA, _col0, None)], name=f"{tag}_qnorm_bwd")
    (dpkv,), (dkv_gain,) = rowwise_vjp(_rms, [(proj, LORA, lambda j: 1)], [kv_gain], [_full(dkvlat)],
                                       [(LORA, LORA, _col0, None)], name=f"{tag}_kvnorm_bwd")
    dproj = jnp.concatenate([dpq, dpkv, dpe], axis=1)
    dw_in = matmul(h, dproj, "tn", name=f"{tag}_in_dw")
    dh = matmul(dproj, w_in, "nt", name=f"{tag}_in_dx")
    (dx,), (dmix,) = rowwise_vjp(_rms, [_full(x)], [mix_gain], [_full(dh)], [(d, d, _col0, dy)],
                                 name=f"{tag}_norm_bwd")
    dqh = jnp.concatenate([dqgn[0], dqgp[0, :ROPE]])
    dkh = jnp.concatenate([dkgn[0], dkgp[0, :ROPE]])
    return dx, (dmix, dw_in, dq_gain, dkv_gain, dw_q, dw_kv, dqh, dkh, dw_out)


def _interleave(a, b):
    t, w = a.shape
    n = w // HEAD_W
    return jnp.stack([a.reshape(t, n, HEAD_W), b.reshape(t, n, HEAD_W)], axis=2).reshape(t, 2 * w)


GDN_HEADS_PER_STEP = 4
CONV_COLS = 512
HALO = 8


def conv_fwd(x, w):
    t = x.shape[0]
    c = w.shape[1]
    tb = _row_block(t)
    assert t % tb == 0

    def body(x_ref, prev_ref, w_ref, y_ref):
        i = pl.program_id(1)
        xv = x_ref[...]
        prev = jnp.where(i > 0, prev_ref[...], 0.0)
        ext = jnp.concatenate([prev, xv], axis=0)
        acc = xv * w_ref[CONV_K - 1:CONV_K, :]
        for s in range(1, CONV_K):
            acc = acc + pltpu.roll(ext, s, 0)[HALO:] * w_ref[CONV_K - 1 - s:CONV_K - s, :]
        y_ref[...] = acc

    spec = pl.BlockSpec((tb, CONV_COLS), lambda j, i: (i, j))
    return pl.pallas_call(
        body, name="conv_fwd", grid=(c // CONV_COLS, t // tb),
        in_specs=[spec, pl.BlockSpec((HALO, CONV_COLS), lambda j, i: (jnp.maximum(i * (tb // HALO) - 1, 0), j)),
                  pl.BlockSpec((HALO, CONV_COLS), lambda j, i: (0, j))],
        out_specs=spec, out_shape=jax.ShapeDtypeStruct((t, c), F32),
        compiler_params=_cparams(("parallel", "parallel")),
    )(x, x, w)


def conv_bwd(x, dy, w):
    t = x.shape[0]
    c = w.shape[1]
    tb = _row_block(t)
    nrb = t // tb

    def body(x_ref, prev_ref, dy_ref, next_ref, w_ref, dx_ref, dw_ref):
        i = pl.program_id(1)
        ext_x = jnp.concatenate([jnp.where(i > 0, prev_ref[...], 0.0), x_ref[...]], axis=0)
        dyv = dy_ref[...]
        ext_dy = jnp.concatenate([dyv, jnp.where(i < nrb - 1, next_ref[...], 0.0)], axis=0)

        @pl.when(i == 0)
        def _():
            dw_ref[...] = jnp.zeros_like(dw_ref)

        acc = dyv * w_ref[CONV_K - 1:CONV_K, :]
        dw_ref[CONV_K - 1:CONV_K, :] += jnp.sum(dyv * x_ref[...], axis=0, keepdims=True)
        for s in range(1, CONV_K):
            acc = acc + pltpu.roll(ext_dy, tb + HALO - s, 0)[:tb] * w_ref[CONV_K - 1 - s:CONV_K - s, :]
            dw_ref[CONV_K - 1 - s:CONV_K - s, :] += jnp.sum(dyv * pltpu.roll(ext_x, s, 0)[HALO:], axis=0, keepdims=True)
        dx_ref[...] = acc

    spec = pl.BlockSpec((tb, CONV_COLS), lambda j, i: (i, j))
    wspec = pl.BlockSpec((HALO, CONV_COLS), lambda j, i: (0, j))
    return pl.pallas_call(
        body, name="conv_bwd", grid=(c // CONV_COLS, nrb),
        in_specs=[spec, pl.BlockSpec((HALO, CONV_COLS), lambda j, i: (jnp.maximum(i * (tb // HALO) - 1, 0), j)),
                  spec, pl.BlockSpec((HALO, CONV_COLS), lambda j, i: (jnp.minimum(i + 1, nrb - 1) * (tb // HALO), j)),
                  wspec],
        out_specs=[spec, wspec],
        out_shape=[jax.ShapeDtypeStruct((t, c), F32), jax.ShapeDtypeStruct((HALO, c), F32)],
        compiler_params=_cparams(("parallel", "arbitrary")),
    )(x, x, dy, dy, w)


def _l2_silu(c):
    a = _silu(c)
    return a * lax.rsqrt(jnp.sum(a * a, axis=-1, keepdims=True) + EPS)


def _gates(ab, a_log, dt_bias, ea, eb):
    g = -jnp.exp(a_log) * _softplus(hdot(ab, ea) + dt_bias)
    return g, _sigmoid(hdot(ab, eb))


def _gate_consts():
    ea = np.zeros((HEAD_W, GDN_W), np.float32)
    eb = np.zeros((HEAD_W, GDN_W), np.float32)
    for h in range(GDN_HEADS):
        ea[h, h * HEAD_W:(h + 1) * HEAD_W] = 1.0
        eb[GDN_HEADS + h, h * HEAD_W:(h + 1) * HEAD_W] = 1.0
    return jnp.asarray(ea), jnp.asarray(eb)


def _batched_dots(precision, to_bf16):
    def raw(a, b, ca, cb):
        if to_bf16:
            a, b = a.astype(BF), b.astype(BF)
        return lax.dot_general(a, b, (((ca,), (cb,)), ((0,), (0,))), precision=precision,
                               preferred_element_type=F32)

    nn = jax.custom_vjp(lambda a, b: raw(a, b, 2, 1))
    nt = jax.custom_vjp(lambda a, b: raw(a, b, 2, 2))
    tn = jax.custom_vjp(lambda a, b: raw(a, b, 1, 1))
    nn.defvjp(lambda a, b: (nn(a, b), (a, b)), lambda r, g: (nt(g, r[1]), tn(r[0], g)))
    nt.defvjp(lambda a, b: (nt(a, b), (a, b)), lambda r, g: (nn(g, r[1]), tn(g, r[0])))
    tn.defvjp(lambda a, b: (tn(a, b), (a, b)), lambda r, g: (nt(r[1], g), nn(r[0], g)))
    return nn, nt, tn


bmm_nn, bmm_nt, bmm_tn = _batched_dots(None, True)
bh_nn, bh_nt, bh_tn = _batched_dots(HI, False)


def _gdn_chunk(q, k, v, g, beta, state):
    n = CHUNK
    nb = q.shape[0]
    ii = lax.broadcasted_iota(jnp.int32, (nb, n, n), 1)
    jj = lax.broadcasted_iota(jnp.int32, (nb, n, n), 2)
    causal = ii >= jj
    eye = (ii == jj).astype(F32)
    gc = bh_nn(causal.astype(F32), g)
    mean_w = jnp.full((nb, n, HEAD_W), 1.0 / HEAD_W, F32)
    gc_i = bh_nt(gc, mean_w)
    gc_j = bh_nt(mean_w, gc)
    decay = jnp.exp(jnp.where(causal, gc_i - gc_j, -1e30))
    kb = k * beta
    vb = v * beta
    m = jnp.where(ii > jj, bmm_nt(kb, k) * decay, 0.0)
    inv = eye - m
    pw = m
    for _ in range(5):
        pw = bh_nn(pw, pw)
        inv = bh_nn(inv, eye + pw)
    eg = jnp.exp(gc)
    u = bh_nn(inv, vb)
    w = bh_nn(inv, kb * eg)
    qs = q * (HEAD_W ** -0.5)
    attn = bmm_nt(qs, k) * decay
    g_last = bh_nn((jj == n - 1).astype(F32), gc)
    k_dec = k * jnp.exp(g_last - gc)
    v_new = u - bmm_nn(w, state)
    out = bmm_nn(qs * eg, state) + bmm_nn(attn, v_new)
    new_state = state * jnp.exp(jnp.concatenate([g_last, g_last], axis=1)) + bmm_tn(k_dec, v_new)
    return out, new_state


def _heads(ref, hb):
    return jnp.stack([ref[:, j * HEAD_W:(j + 1) * HEAD_W] for j in range(hb)])


def gdn_fwd(qk, v, g, beta):
    t = v.shape[0]
    n = t // CHUNK
    nh = GDN_HEADS

    hb = GDN_HEADS_PER_STEP
    ng = nh // hb

    def body(q_ref, k_ref, v_ref, g_ref, b_ref, o_ref, s_ref, state):
        @pl.when(pl.program_id(1) == 0)
        def _():
            state[...] = jnp.zeros_like(state)

        old = state[...]
        out, new = _gdn_chunk(*[_heads(r, hb) for r in (q_ref, k_ref, v_ref, g_ref, b_ref)], old)
        s_ref[:, 0] = old
        for j in range(hb):
            o_ref[:, j * HEAD_W:(j + 1) * HEAD_W] = out[j]
        state[...] = new

    spec = pl.BlockSpec((CHUNK, hb * HEAD_W), lambda h, c: (c, h))
    return pl.pallas_call(
        body, name="gdn_fwd", grid=(ng, n),
        in_specs=[spec, pl.BlockSpec((CHUNK, hb * HEAD_W), lambda h, c: (c, ng + h)), spec, spec, spec],
        out_specs=[spec, pl.BlockSpec((hb, 1, HEAD_W, HEAD_W), lambda h, c: (h, c, 0, 0))],
        out_shape=[jax.ShapeDtypeStruct((t, nh * HEAD_W), F32), jax.ShapeDtypeStruct((nh, n, HEAD_W, HEAD_W), F32)],
        scratch_shapes=[pltpu.VMEM((hb, HEAD_W, HEAD_W), F32)],
        compiler_params=_cparams(("parallel", "arbitrary")),
    )(qk, qk, v, g, beta)


def gdn_bwd(qk, v, g, beta, states, do):
    t = v.shape[0]
    n = t // CHUNK
    nh = GDN_HEADS

    hb = GDN_HEADS_PER_STEP
    ng = nh // hb

    def body(q_ref, k_ref, v_ref, g_ref, b_ref, s_ref, do_ref, dq_ref, dk_ref, dv_ref, dg_ref, db_ref, dstate):
        @pl.when(pl.program_id(1) == 0)
        def _():
            dstate[...] = jnp.zeros_like(dstate)

        _, pull = jax.vjp(_gdn_chunk, *[_heads(r, hb) for r in (q_ref, k_ref, v_ref, g_ref, b_ref)], s_ref[:, 0])
        grads = pull((_heads(do_ref, hb), dstate[...]))
        for ref, gr in zip((dq_ref, dk_ref, dv_ref, dg_ref, db_ref), grads[:5]):
            for j in range(hb):
                ref[:, j * HEAD_W:(j + 1) * HEAD_W] = gr[j]
        dstate[...] = grads[5]

    spec = pl.BlockSpec((CHUNK, hb * HEAD_W), lambda h, c: (n - 1 - c, h))
    return pl.pallas_call(
        body, name="gdn_bwd", grid=(ng, n),
        in_specs=[spec, pl.BlockSpec((CHUNK, hb * HEAD_W), lambda h, c: (n - 1 - c, ng + h)), spec, spec, spec,
                  pl.BlockSpec((hb, 1, HEAD_W, HEAD_W), lambda h, c: (h, n - 1 - c, 0, 0)), spec],
        out_specs=[spec] * 5,
        out_shape=[jax.ShapeDtypeStruct((t, nh * HEAD_W), F32)] * 5,
        scratch_shapes=[pltpu.VMEM((hb, HEAD_W, HEAD_W), F32)],
        compiler_params=_cparams(("parallel", "arbitrary")),
    )(qk, qk, v, g, beta, states, do)


def _gdn_post(o, z, gain):
    return _rms(o, gain) * _silu(z)


POOL_HALO = 16


def _pool_counts(t0, rows, cols):
    tt = t0 + lax.broadcasted_iota(jnp.int32, (rows, cols), 0) + 1
    grp = lax.broadcasted_iota(jnp.int32, (rows, cols), 1) // POOL_GROUP_W
    win = jnp.left_shift(2, grp)
    return jnp.minimum(tt, win).astype(F32), grp


def _by_group(grp, parts):
    out = parts[-1]
    for gi in range(len(parts) - 2, -1, -1):
        out = jnp.where(grp == gi, parts[gi], out)
    return out


def pool_window_fwd(u):
    t, c = u.shape
    tb = _row_block(t)

    def body(u_ref, prev_ref, d_ref):
        i = pl.program_id(0)
        xv = u_ref[...]
        ext = jnp.concatenate([jnp.where(i > 0, prev_ref[...], 0.0), xv], axis=0)
        sums = []
        s = ext
        for sh in (1, 2, 4, 8):
            s = s + pltpu.roll(s, sh, 0)
            sums.append(s[POOL_HALO:])
        cnt, grp = _pool_counts(i * tb, tb, c)
        d_ref[...] = _by_group(grp, sums) / cnt - xv

    spec = pl.BlockSpec((tb, c), lambda i: (i, 0))
    return pl.pallas_call(
        body, name="pool_window_fwd", grid=(t // tb,),
        in_specs=[spec, pl.BlockSpec((POOL_HALO, c), lambda i: (jnp.maximum(i * (tb // POOL_HALO) - 1, 0), 0))],
        out_specs=spec, out_shape=jax.ShapeDtypeStruct((t, c), F32),
        compiler_params=_cparams(("parallel",)),
    )(u, u)


def pool_window_bwd(dd):
    t, c = dd.shape
    tb = _row_block(t)
    nrb = t // tb
    length = tb + POOL_HALO

    def body(d_ref, next_ref, du_ref):
        i = pl.program_id(0)
        dv = d_ref[...]
        ext = jnp.concatenate([dv, jnp.where(i < nrb - 1, next_ref[...], 0.0)], axis=0)
        cnt, grp = _pool_counts(i * tb, length, c)
        s = ext / cnt
        sums = []
        for sh in (1, 2, 4, 8):
            s = s + pltpu.roll(s, length - sh, 0)
            sums.append(s[:tb])
        du_ref[...] = _by_group(grp[:tb], sums) - dv

    spec = pl.BlockSpec((tb, c), lambda i: (i, 0))
    return pl.pallas_call(
        body, name="pool_window_bwd", grid=(nrb,),
        in_specs=[spec, pl.BlockSpec((POOL_HALO, c), lambda i: (jnp.minimum(i + 1, nrb - 1) * (tb // POOL_HALO), 0))],
        out_specs=spec, out_shape=jax.ShapeDtypeStruct((t, c), F32),
        compiler_params=_cparams(("parallel",)),
    )(dd, dd)


def _pool_mix(d, w, scale):
    return mm_nn(d, w) * scale


def pool_mix_fwd(diff, w, scale):
    t = diff.shape[0]
    tb = _row_block(t)
    gw = POOL_GROUP_W

    def body(d_ref, w_ref, s_ref, o_ref):
        o_ref[...] = _pool_mix(d_ref[...], w_ref[...], s_ref[...]).astype(o_ref.dtype)

    spec = pl.BlockSpec((tb, gw), lambda i, g: (i, g))
    return pl.pallas_call(
        body, name="pool_mix_fwd", grid=(t // tb, POOL_W // gw),
        in_specs=[spec, pl.BlockSpec((None, gw, gw), lambda i, g: (g, 0, 0)), pl.BlockSpec((1, gw), lambda i, g: (0, g))],
        out_specs=spec, out_shape=jax.ShapeDtypeStruct((t, POOL_W), BF),
        compiler_params=_cparams(("parallel", "parallel")),
    )(diff, w, scale)


def pool_mix_bwd(diff, w, scale, dp, dp_col0):
    t = diff.shape[0]
    tb = _row_block(t)
    gw = POOL_GROUP_W

    def body(d_ref, w_ref, s_ref, dp_ref, dd_ref, dw_ref, ds_ref):
        @pl.when(pl.program_id(1) == 0)
        def _():
            dw_ref[...] = jnp.zeros_like(dw_ref)
            ds_ref[...] = jnp.zeros_like(ds_ref)

        _, pull = jax.vjp(_pool_mix, d_ref[...], w_ref[...].astype(F32), s_ref[...])
        dd, dw, ds = pull(dp_ref[...])
        dd_ref[...] = dd
        dw_ref[...] += dw
        ds_ref[...] += ds

    spec = pl.BlockSpec((tb, gw), lambda g, i: (i, g))
    wspec = pl.BlockSpec((None, gw, gw), lambda g, i: (g, 0, 0))
    sspec = pl.BlockSpec((1, gw), lambda g, i: (0, g))
    return pl.pallas_call(
        body, name="pool_mix_bwd", grid=(POOL_W // gw, t // tb),
        in_specs=[spec, wspec, sspec, pl.BlockSpec((tb, gw), lambda g, i: (i, dp_col0 + g))],
        out_specs=[spec, wspec, sspec],
        out_shape=[jax.ShapeDtypeStruct((t, POOL_W), F32), jax.ShapeDtypeStruct(w.shape, F32),
                   jax.ShapeDtypeStruct(scale.shape, F32)],
        compiler_params=_cparams(("parallel", "arbitrary")),
    )(diff, w, scale, dp)


def hyb_fwd(x, mix_gain, w_qkvz, w_ab, w_u, conv_w, a_log, dt_bias, out_gain, pool_w, pool_scale, w_out, tag):
    d = x.shape[1]
    ea, eb = _gate_consts()
    nh = GDN_HEADS
    (h,) = rowwise(_rms, [_full(x)], [mix_gain], [(d, d, _col0, BF)], name=f"{tag}_norm")
    p1 = matmul(h, w_qkvz, "nn", name=f"{tag}_in_qkvz")
    ab = matmul(h, w_ab, "nn", name=f"{tag}_in_ab")
    u = matmul(h, w_u, "nn", name=f"{tag}_in_u")
    cv = conv_fwd(p1, conv_w)
    (qk,) = rowwise(_l2_silu, [(cv, HEAD_W, _colj)], [], [(2 * GDN_W, HEAD_W, _colj, F32)], ncol=2 * nh,
                    name=f"{tag}_qk_act")
    (v,) = rowwise(_silu, [(cv, GDN_W, lambda j: 2)], [], [(GDN_W, GDN_W, _col0, F32)], name=f"{tag}_v_act")
    g, beta = rowwise(_gates, [_full(ab)], [a_log, dt_bias, ea, eb],
                      [(GDN_W, GDN_W, _col0, F32), (GDN_W, GDN_W, _col0, F32)], name=f"{tag}_gates")
    o, states = gdn_fwd(qk, v, g, beta)
    (on,) = rowwise(_gdn_post, [(o, HEAD_W, _colj), (p1, HEAD_W, lambda j: 3 * nh + j)], [out_gain],
                    [(GDN_W, HEAD_W, _colj, BF)], ncol=nh, name=f"{tag}_post")
    diff = pool_window_fwd(u)
    pm = pool_mix_fwd(diff, pool_w, pool_scale)
    cat = jnp.concatenate([on, pm], axis=1)
    y = matmul(cat, w_out, "nn", add=x, name=f"{tag}_out")
    return y, (x, h, p1, ab, cv, qk, v, g, beta, o, states, diff, cat)


def hyb_bwd(dy, saved, mix_gain, w_qkvz, w_ab, w_u, conv_w, a_log, dt_bias, out_gain, pool_w, pool_scale, w_out, tag):
    x, h, p1, ab, cv, qk, v, g, beta, o, states, diff, cat = saved
    d = x.shape[1]
    nh = GDN_HEADS
    ea, eb = _gate_consts()
    dcat = matmul(dy, w_out, "nt", name=f"{tag}_out_dx")
    dw_out = matmul(cat, dy, "tn", name=f"{tag}_out_dw")
    (do, dz), (dout_gain,) = rowwise_vjp(
        _gdn_post, [(o, HEAD_W, _colj), (p1, HEAD_W, lambda j: 3 * nh + j)], [out_gain], [(dcat, HEAD_W, _colj)],
        [(GDN_W, HEAD_W, _colj, None), (GDN_W, HEAD_W, _colj, None)], ncol=nh, name=f"{tag}_post_bwd")
    ddiff, dpool_w, dpool_scale = pool_mix_bwd(diff, pool_w, pool_scale, dcat, GDN_W // POOL_GROUP_W)
    du = pool_window_bwd(ddiff)
    dq, dk, dv, dg, dbeta = gdn_bwd(qk, v, g, beta, states, do)
    (dab,), (da_log, ddt_bias) = rowwise_vjp(
        _gates, [_full(ab)], [a_log, dt_bias], [_full(dg), _full(dbeta)], [(HEAD_W, HEAD_W, _col0, None)],
        consts=[ea, eb], name=f"{tag}_gates_bwd")
    (dcq,), _ = rowwise_vjp(_l2_silu, [(cv, HEAD_W, _colj)], [], [(dq, HEAD_W, _colj)],
                            [(GDN_W, HEAD_W, _colj, None)], ncol=nh, par_grads=False, name=f"{tag}_q_act_bwd")
    (dck,), _ = rowwise_vjp(_l2_silu, [(cv, HEAD_W, lambda j: nh + j)], [], [(dk, HEAD_W, _colj)],
                            [(GDN_W, HEAD_W, _colj, None)], ncol=nh, par_grads=False, name=f"{tag}_k_act_bwd")
    (dcv,), _ = rowwise_vjp(_silu, [(cv, GDN_W, lambda j: 2)], [], [_full(dv)],
                            [(GDN_W, GDN_W, _col0, None)], par_grads=False, name=f"{tag}_v_act_bwd")
    dqkv, dconv_w = conv_bwd(p1, jnp.concatenate([dcq, dck, dcv], axis=1), conv_w)
    dp1 = jnp.concatenate([dqkv, dz], axis=1)
    dw_qkvz = matmul(h, dp1, "tn", name=f"{tag}_in_qkvz_dw")
    dw_ab = matmul(h, dab, "tn", name=f"{tag}_in_ab_dw")
    dw_u = matmul(h, du, "tn", name=f"{tag}_in_u_dw")
    dh = matmul(dp1, w_qkvz, "nt", name=f"{tag}_in_qkvz_dx")
    dh = matmul(dab, w_ab, "nt", add=dh, name=f"{tag}_in_ab_dx")
    dh = matmul(du, w_u, "nt", add=dh, name=f"{tag}_in_u_dx")
    (dx,), (dmix,) = rowwise_vjp(_rms, [_full(x)], [mix_gain], [_full(dh)], [(d, d, _col0, dy)],
                                 name=f"{tag}_norm_bwd")
    return dx, (dmix, dw_qkvz, dw_ab, dw_u, dconv_w, da_log, ddt_bias, dout_gain, dpool_w, dpool_scale, dw_out)


def loss_head(y, target):
    t, d = y.shape
    tb = _row_block(t)

    def body(y_ref, t_ref, dy_ref, loss_ref):
        @pl.when(pl.program_id(0) == 0)
        def _():
            loss_ref[...] = jnp.zeros_like(loss_ref)

        e = y_ref[...] - t_ref[...]
        dy_ref[...] = e * (1.0 / d)
        loss_ref[...] += 0.5 * jnp.sum(jnp.mean(e * e, axis=-1, keepdims=True))

    spec = pl.BlockSpec((tb, d), lambda i: (i, 0))
    return pl.pallas_call(
        body, name="loss_head", grid=(t // tb,), in_specs=[spec, spec],
        out_specs=[spec, pl.BlockSpec((8, 128), lambda i: (0, 0))],
        out_shape=[jax.ShapeDtypeStruct((t, d), F32), jax.ShapeDtypeStruct((8, 128), F32)],
        compiler_params=_cparams(("arbitrary",)),
    )(y, target)


ADAM_BLOCK_ELEMS = 256 * 1024


def _adam_rows(rows, cols):
    tb = 1024
    while tb >= 8:
        if rows % tb == 0 and tb * cols <= ADAM_BLOCK_ELEMS:
            return tb
        tb //= 2
    return rows


def adamw(w, g, m, v, name):
    rows, cols = w.shape
    tb = _adam_rows(rows, cols)
    c1 = 1.0 - ADAM_B1 ** ADAM_STEP
    c2 = 1.0 - ADAM_B2 ** ADAM_STEP

    def body(w_ref, g_ref, m_ref, v_ref, d_ref, nm_ref, nv_ref):
        gv = g_ref[...]
        nm = ADAM_B1 * m_ref[...] + (1.0 - ADAM_B1) * gv
        nv = ADAM_B2 * v_ref[...] + (1.0 - ADAM_B2) * (gv * gv)
        d_ref[...] = -ADAM_LR * ((nm / c1) / (jnp.sqrt(nv / c2) + ADAM_EPS) + ADAM_WD * w_ref[...])
        nm_ref[...] = nm
        nv_ref[...] = nv

    spec = pl.BlockSpec((tb, cols), lambda i: (i, 0))
    return pl.pallas_call(
        body, name=name, grid=(rows // tb,), in_specs=[spec] * 4, out_specs=[spec] * 3,
        out_shape=[jax.ShapeDtypeStruct((rows, cols), F32)] * 3,
        compiler_params=_cparams(("parallel",)),
    )(w, g, m, v)


LANES = 1024
HBM = pl.BlockSpec(memory_space=pltpu.HBM)


def _place():
    x, y, c = lax.axis_index("x"), lax.axis_index("y"), lax.axis_index("c")
    others = [(1 - x, y), (x, 1 - y), (1 - x, 1 - y)]
    return x, y, c, 2 * x + y, others


def _comm_call(body, name, out_shape, n_sems, *args):
    return pl.pallas_call(
        body, name=name, out_shape=out_shape, in_specs=[HBM] * len(args),
        out_specs=[HBM] * len(out_shape),
        scratch_shapes=[pltpu.SemaphoreType.DMA((n_sems,)), pltpu.SemaphoreType.DMA((n_sems,))],
        compiler_params=pltpu.CompilerParams(has_side_effects=True),
    )(*args)


AG_CHUNKS = 4


def all_gather_chips(xl):
    two, h, lanes = xl.shape
    assert h % (AG_CHUNKS * 16) == 0
    hc = h // AG_CHUNKS

    def body(x_ref, out_ref, send_sems, recv_sems):
        x, y, c, me, others = _place()
        sib = (x, y, 1 - c)

        def copy(k, src, dst, to):
            return pltpu.make_async_remote_copy(src_ref=src, dst_ref=dst, send_sem=send_sems.at[k],
                                                recv_sem=recv_sems.at[k], device_id=to, device_id_type=MESH)

        first, passed = [], []
        for q in range(AG_CHUNKS):
            rows = pl.ds(q * hc, hc)
            for k, chip in enumerate(others):
                cp = copy(q * 3 + k, x_ref.at[c, rows], out_ref.at[me, c, rows], (*chip, c))
                cp.start()
                first.append(cp)
        base = 3 * AG_CHUNKS
        for q in range(AG_CHUNKS):
            rows = pl.ds(q * hc, hc)
            for k, (cx, cy) in enumerate(others):
                slot = out_ref.at[2 * cx + cy, c, rows]
                copy(q * 3 + k, slot, slot, sib).wait_recv()
                fwd = copy(base + q * 3 + k, slot, slot, sib)
                fwd.start()
                passed.append(fwd)
        for q in range(AG_CHUNKS):
            rows = pl.ds(q * hc, hc)
            for k, (cx, cy) in enumerate(others):
                slot = out_ref.at[2 * cx + cy, 1 - c, rows]
                copy(base + q * 3 + k, slot, slot, sib).wait_recv()
        for cp in first + passed:
            cp.wait_send()

    return _comm_call(body, "all_gather_chips", [jax.ShapeDtypeStruct((N_CHIPS,) + xl.shape, xl.dtype)],
                      6 * AG_CHUNKS, xl)[0]


def pair_swap(gs, name):
    n = len(gs)

    def body(*refs):
        g_refs, r_refs, send_sems, recv_sems = refs[:n], refs[n:2 * n], refs[2 * n], refs[2 * n + 1]
        x, y, c, _, _ = _place()
        cps = [pltpu.make_async_remote_copy(src_ref=g.at[:, 1 - c], dst_ref=r, send_sem=send_sems.at[i],
                                            recv_sem=recv_sems.at[i], device_id=(x, y, 1 - c), device_id_type=MESH)
               for i, (g, r) in enumerate(zip(g_refs, r_refs))]
        for cp in cps:
            cp.start()
        for cp in cps:
            cp.wait()

    out = [jax.ShapeDtypeStruct((g.shape[0],) + g.shape[2:], g.dtype) for g in gs]
    return _comm_call(body, name, out, n, *gs)


def chip_scatter(ss, name):
    n = len(ss)

    def body(*refs):
        s_refs, y_refs, send_sems, recv_sems = refs[:n], refs[n:2 * n], refs[2 * n], refs[2 * n + 1]
        x, y, c, me, others = _place()
        cps = []
        for i, (s, out) in enumerate(zip(s_refs, y_refs)):
            for k, (cx, cy) in enumerate(others):
                cp = pltpu.make_async_remote_copy(src_ref=s.at[2 * cx + cy], dst_ref=out.at[me],
                                                  send_sem=send_sems.at[3 * i + k], recv_sem=recv_sems.at[3 * i + k],
                                                  device_id=(cx, cy, c), device_id_type=MESH)
                cp.start()
                cps.append(cp)
        for i, out in enumerate(y_refs):
            for k, (cx, cy) in enumerate(others):
                slot = out.at[2 * cx + cy]
                pltpu.make_async_remote_copy(src_ref=slot, dst_ref=slot, send_sem=send_sems.at[3 * i + k],
                                             recv_sem=recv_sems.at[3 * i + k], device_id=(cx, cy, c),
                                             device_id_type=MESH).wait_recv()
        for cp in cps:
            cp.wait_send()

    return _comm_call(body, name, [jax.ShapeDtypeStruct(s.shape, s.dtype) for s in ss], 3 * n, *ss)


def pair_join(fs, name):
    n = len(fs)

    def body(*refs):
        f_refs, o_refs, send_sems, recv_sems = refs[:n], refs[n:2 * n], refs[2 * n], refs[2 * n + 1]
        x, y, c, _, _ = _place()
        cps = [pltpu.make_async_remote_copy(src_ref=f, dst_ref=o, send_sem=send_sems.at[i], recv_sem=recv_sems.at[i],
                                            device_id=(x, y, 1 - c), device_id_type=MESH)
               for i, (f, o) in enumerate(zip(f_refs, o_refs))]
        for cp in cps:
            cp.start()
        for cp in cps:
            cp.wait()

    return _comm_call(body, name, [jax.ShapeDtypeStruct(f.shape, f.dtype) for f in fs], n, *fs)


SUM_ROWS = 256


def _tile_rows(h, cap):
    tb = cap
    while tb > 8 and h % tb:
        tb //= 2
    assert h % tb == 0, (h, tb)
    return tb


def sum_blocks(a, name):
    n, h, lanes = a.shape
    tb = _tile_rows(h, SUM_ROWS)

    def body(a_ref, o_ref):
        acc = a_ref[0]
        for k in range(1, n):
            acc = acc + a_ref[k]
        o_ref[...] = acc

    return pl.pallas_call(
        body, name=name, grid=(h // tb,), in_specs=[pl.BlockSpec((n, tb, lanes), lambda i: (0, i, 0))],
        out_specs=pl.BlockSpec((tb, lanes), lambda i: (i, 0)), out_shape=jax.ShapeDtypeStruct((h, lanes), a.dtype),
        compiler_params=_cparams(("parallel",)),
    )(a)


def add2(a, b, name):
    r, lanes = a.shape
    tb = _tile_rows(r, SUM_ROWS)

    def body(a_ref, b_ref, o_ref):
        o_ref[...] = a_ref[...] + b_ref[...]

    spec = pl.BlockSpec((tb, lanes), lambda i: (i, 0))
    return pl.pallas_call(
        body, name=name, grid=(r // tb,), in_specs=[spec, spec], out_specs=spec,
        out_shape=jax.ShapeDtypeStruct((r, lanes), a.dtype), compiler_params=_cparams(("parallel",)),
    )(a, b)


def reduce_scatter_chips(gs, tag):
    c = lax.axis_index("c")
    me = 2 * lax.axis_index("x") + lax.axis_index("y")
    got = pair_swap(gs, f"pair_swap_{tag}")
    sums = []
    for i, (g, r) in enumerate(zip(gs, got)):
        n, _, h, lanes = g.shape
        mine = lax.dynamic_index_in_dim(g, c, axis=1, keepdims=False)
        sums.append(add2(mine.reshape(n * h, lanes), r.reshape(n * h, lanes), f"pair_sum_{tag}_{i}").reshape(n, h, lanes))
    ys = chip_scatter(sums, f"chip_scatter_{tag}")
    fs = []
    for i, (s, y) in enumerate(zip(sums, ys)):
        own = lax.dynamic_index_in_dim(s, me, axis=0, keepdims=True)
        fs.append(sum_blocks(lax.dynamic_update_slice_in_dim(y, own, me, axis=0), f"chip_sum_{tag}_{i}"))
    others = pair_join(fs, f"pair_join_{tag}")
    return [jnp.concatenate([jnp.where(c == 0, f, o), jnp.where(c == 0, o, f)]).reshape(-1)
            for f, o in zip(fs, others)]


SHARDED = {
    "ffn1_w_gate": 2, "ffn1_w_up": 2, "ffn1_w_down": 1, "ffn2_w_gate": 2, "ffn2_w_up": 2, "ffn2_w_down": 1,
    "hyb_w_in": 2, "gdn_conv": 2, "pool_w": 2, "hyb_w_out": 1, "mla_w_in": 1, "mla_q_norm": 1, "mla_kv_norm": 1,
    "mla_w_q_up": 2, "mla_w_kv_up": 2, "mla_w_out": 1,
}
EXACT = ("gdn_conv", "mla_q_norm", "mla_kv_norm")
EVEN_ONLY = ("hyb_w_in", "gdn_conv", "gdn_a_log", "gdn_dt_bias", "gdn_out_norm", "pool_w", "pool_scale", "hyb_w_out")
WEIGHTS = ["ffn1_norm", "ffn1_w_gate", "ffn1_w_up", "ffn1_w_down", "mix_norm", "ffn2_norm", "ffn2_w_gate",
           "ffn2_w_up", "ffn2_w_down", "hyb_w_in", "gdn_conv", "gdn_a_log", "gdn_dt_bias", "gdn_out_norm", "pool_w",
           "pool_scale", "hyb_w_out", "mla_w_in", "mla_q_norm", "mla_kv_norm", "mla_w_q_up", "mla_w_kv_up",
           "mla_q_head_norm", "mla_k_head_norm", "mla_w_out"]


def _pad_rows(flat, mult):
    n = flat.shape[0]
    rows = -(-n // LANES)
    rows = -(-rows // mult) * mult
    return jnp.pad(flat, (0, rows * LANES - n)), rows


def gather_weights(w):
    parts = []
    for name in SHARDED:
        a = w[name]
        parts.append(lax.bitcast_convert_type(a, BF).reshape(-1) if name in EXACT else a.astype(BF).reshape(-1))
    flat, rows = _pad_rows(jnp.concatenate(parts), 2 * 16 * AG_CHUNKS)
    mine = flat.reshape(2, rows // 2, LANES)
    me = 2 * lax.axis_index("x") + lax.axis_index("y")
    got = lax.dynamic_update_slice_in_dim(all_gather_chips(mine), mine[None], me, axis=0)
    got = got.reshape(N_CHIPS, rows * LANES)
    full, off = {}, 0
    for name, axis in SHARDED.items():
        a = w[name]
        n = a.size * (2 if name in EXACT else 1)
        seg = got[:, off:off + n]
        off += n
        if name in EXACT:
            seg = lax.bitcast_convert_type(seg.reshape((N_CHIPS,) + a.shape + (2,)), F32)
        else:
            seg = seg.reshape((N_CHIPS,) + a.shape)
        seg = jnp.moveaxis(seg, 0, axis)
        full[name] = seg.reshape(a.shape[:axis] + (N_CHIPS * a.shape[axis],) + a.shape[axis + 1:])
    return full


OWN_OPERAND = 256 * 1024
EXCHANGE_UNIT = 2 * 8 * LANES


def shard_major(g, axis):
    size = g.shape[axis] // N_CHIPS
    return jnp.moveaxis(g.reshape(g.shape[:axis] + (N_CHIPS, size) + g.shape[axis + 1:]), axis, 0)


def _exchange_operand(flat):
    n = flat.shape[1]
    padded = -(-n // EXCHANGE_UNIT) * EXCHANGE_UNIT
    return jnp.pad(flat, ((0, 0), (0, padded - n))).reshape(N_CHIPS, 2, padded // (2 * LANES), LANES)


def reduce_grads(per_layer, loss_tile):
    big, small = [], []
    for name in WEIGHTS:
        for layer, g in enumerate(per_layer[name]):
            if name in SHARDED:
                flat, shape = g.reshape(N_CHIPS, -1), g.shape[1:]
            else:
                flat, shape = jnp.broadcast_to(g.reshape(1, -1), (N_CHIPS, g.size)), g.shape
            (big if name in SHARDED and flat.shape[1] >= OWN_OPERAND else small).append((name, layer, flat, shape))
    small.append(("loss", 0, jnp.broadcast_to(loss_tile.reshape(1, -1), (N_CHIPS, loss_tile.size)), loss_tile.shape))
    misc = jnp.concatenate([flat for _, _, flat, _ in small], axis=1)
    red = reduce_scatter_chips([_exchange_operand(flat) for _, _, flat, _ in big] + [_exchange_operand(misc)], "grads")
    out = {name: [None] * len(per_layer[name]) for name in WEIGHTS}
    loss = None
    for (name, layer, flat, shape), r in zip(big, red):
        out[name][layer] = r[:flat.shape[1]].reshape(shape)
    off = 0
    for name, layer, flat, shape in small:
        piece = red[-1][off:off + flat.shape[1]].reshape(shape)
        off += flat.shape[1]
        if name == "loss":
            loss = piece[0, 0]
        else:
            out[name][layer] = piece
    return {name: jnp.stack(out[name]) for name in WEIGHTS}, loss


def _as2d(a):
    return a.reshape(-1, a.shape[-1])


def kernel(x, positions, ffn1_norm, ffn1_w_gate, ffn1_w_up, ffn1_w_down, mix_norm, ffn2_norm, ffn2_w_gate, ffn2_w_up, ffn2_w_down, hyb_w_in, gdn_conv, gdn_a_log, gdn_dt_bias, gdn_out_norm, pool_w, pool_scale, hyb_w_out, mla_w_in, mla_q_norm, mla_kv_norm, mla_w_q_up, mla_w_kv_up, mla_q_head_norm, mla_k_head_norm, mla_w_out, loss_target, m_ffn1_norm, m_ffn1_w_gate, m_ffn1_w_up, m_ffn1_w_down, m_mix_norm, m_ffn2_norm, m_ffn2_w_gate, m_ffn2_w_up, m_ffn2_w_down, m_hyb_w_in, m_gdn_conv, m_gdn_a_log, m_gdn_dt_bias, m_gdn_out_norm, m_pool_w, m_pool_scale, m_hyb_w_out, m_mla_w_in, m_mla_q_norm, m_mla_kv_norm, m_mla_w_q_up, m_mla_w_kv_up, m_mla_q_head_norm, m_mla_k_head_norm, m_mla_w_out, v_ffn1_norm, v_ffn1_w_gate, v_ffn1_w_up, v_ffn1_w_down, v_mix_norm, v_ffn2_norm, v_ffn2_w_gate, v_ffn2_w_up, v_ffn2_w_down, v_hyb_w_in, v_gdn_conv, v_gdn_a_log, v_gdn_dt_bias, v_gdn_out_norm, v_pool_w, v_pool_scale, v_hyb_w_out, v_mla_w_in, v_mla_q_norm, v_mla_kv_norm, v_mla_w_q_up, v_mla_w_kv_up, v_mla_q_head_norm, v_mla_k_head_norm, v_mla_w_out):
    given = dict(locals())
    w = {n: given[n] for n in WEIGHTS}
    moments_m = {n: given["m_" + n] for n in WEIGHTS}
    moments_v = {n: given["v_" + n] for n in WEIGHTS}
    t = x.shape[1]
    xs = x.reshape(t, D_MODEL)
    full = gather_weights(w)
    n_even = hyb_w_in.shape[0]
    n_odd = mla_w_in.shape[0]

    _, sign, inv_freq = _rope_consts()
    cos, sin = rope_tables(positions.reshape(t, 1), inv_freq, sign)

    def ffn_args(which, layer):
        return (w[f"{which}_norm"][layer][None], full[f"{which}_w_gate"][layer], full[f"{which}_w_up"][layer],
                full[f"{which}_w_down"][layer])

    def hyb_args(i):
        win = full["hyb_w_in"][i]
        cut = 4 * GDN_W
        w_ab = jnp.pad(win[:, cut:cut + 2 * GDN_HEADS], ((0, 0), (0, HEAD_W - 2 * GDN_HEADS)))
        return (w["mix_norm"][2 * i][None], win[:, :cut], w_ab, win[:, cut + 2 * GDN_HEADS:],
                jnp.pad(full["gdn_conv"][i], ((0, HALO - CONV_K), (0, 0))), jnp.repeat(w["gdn_a_log"][i], HEAD_W)[None],
                jnp.repeat(w["gdn_dt_bias"][i], HEAD_W)[None], w["gdn_out_norm"][i][None], full["pool_w"][i],
                w["pool_scale"][i][None], full["hyb_w_out"][i])

    def mla_args(i):
        w_in = jnp.pad(full["mla_w_in"][i], ((0, 0), (0, ODD_IN_PAD - ODD_IN)))
        w_q = jnp.pad(full["mla_w_q_up"][i].reshape(LORA, MLA_HEADS, QK_HEAD),
                      ((0, 0), (0, 0), (0, 2 * HEAD_W - QK_HEAD))).reshape(LORA, MLA_HEADS * 2 * HEAD_W)
        return (cos, sin, w["mix_norm"][2 * i + 1][None], w_in, full["mla_q_norm"][i][None], full["mla_kv_norm"][i][None],
                w_q, full["mla_w_kv_up"][i], w["mla_q_head_norm"][i], w["mla_k_head_norm"][i], full["mla_w_out"][i])

    saved = []
    h = xs
    for layer in range(DEPTH):
        i = layer // 2
        h, s1 = ffn_fwd(h, *ffn_args("ffn1", layer), f"l{layer}_ffn1")
        if layer % 2 == 0:
            h, s2 = hyb_fwd(h, *hyb_args(i), f"l{layer}_hyb")
        else:
            h, s2 = mla_fwd(h, *mla_args(i), f"l{layer}_mla")
        h, s3 = ffn_fwd(h, *ffn_args("ffn2", layer), f"l{layer}_ffn2")
        saved.append((s1, s2, s3))

    dh, loss_tile = loss_head(h, loss_target.reshape(t, D_MODEL))

    per_layer = {n: [None] * (DEPTH if n.startswith(("ffn", "mix")) else (n_even if n in EVEN_ONLY else n_odd))
                 for n in WEIGHTS}
    for layer in reversed(range(DEPTH)):
        i = layer // 2
        s1, s2, s3 = saved[layer]
        dh, dg, dwg, dwu, dwd = ffn_bwd(dh, s3, *ffn_args("ffn2", layer), f"l{layer}_ffn2")
        per_layer["ffn2_norm"][layer], per_layer["ffn2_w_gate"][layer] = dg[0], dwg
        per_layer["ffn2_w_up"][layer], per_layer["ffn2_w_down"][layer] = dwu, dwd
        if layer % 2 == 0:
            dh, g = hyb_bwd(dh, s2, *hyb_args(i), f"l{layer}_hyb")
            dmix, dw_qkvz, dw_ab, dw_u, dconv, da_log, ddt, dog, dpw, dps, dwo = g
            per_layer["hyb_w_in"][i] = jnp.concatenate([dw_qkvz, dw_ab[:, :2 * GDN_HEADS], dw_u], axis=1)
            per_layer["gdn_conv"][i] = dconv[:CONV_K]
            per_layer["gdn_a_log"][i] = da_log.reshape(GDN_HEADS, HEAD_W).sum(axis=1)
            per_layer["gdn_dt_bias"][i] = ddt.reshape(GDN_HEADS, HEAD_W).sum(axis=1)
            per_layer["gdn_out_norm"][i], per_layer["pool_w"][i] = dog[0], dpw
            per_layer["pool_scale"][i], per_layer["hyb_w_out"][i] = dps[0], dwo
        else:
            dh, g = mla_bwd(dh, s2, *mla_args(i), f"l{layer}_mla")
            dmix, dw_in, dqg, dkvg, dwq, dwkv, dqh, dkh, dwo = g
            per_layer["mla_w_in"][i] = dw_in[:, :ODD_IN]
            per_layer["mla_q_norm"][i], per_layer["mla_kv_norm"][i] = dqg[0], dkvg[0]
            per_layer["mla_w_q_up"][i] = dwq.reshape(LORA, MLA_HEADS, 2 * HEAD_W)[:, :, :QK_HEAD].reshape(LORA, -1)
            per_layer["mla_w_kv_up"][i] = dwkv
            per_layer["mla_q_head_norm"][i], per_layer["mla_k_head_norm"][i] = dqh, dkh
            per_layer["mla_w_out"][i] = dwo
        per_layer["mix_norm"][layer] = dmix[0]
        dh, dg, dwg, dwu, dwd = ffn_bwd(dh, s1, *ffn_args("ffn1", layer), f"l{layer}_ffn1")
        per_layer["ffn1_norm"][layer], per_layer["ffn1_w_gate"][layer] = dg[0], dwg
        per_layer["ffn1_w_up"][layer], per_layer["ffn1_w_down"][layer] = dwu, dwd

    for n, axis in SHARDED.items():
        if not n.endswith(("w_gate", "w_up")):
            per_layer[n] = [shard_major(g, axis - 1) for g in per_layer[n]]
    grads, loss = reduce_grads(per_layer, loss_tile)

    deltas, new_m, new_v = {}, {}, {}
    for n in WEIGHTS:
        d2, m2, v2 = adamw(_as2d(w[n]), _as2d(grads[n]), _as2d(moments_m[n]), _as2d(moments_v[n]), f"adamw_{n}")
        deltas[n], new_m[n], new_v[n] = d2.reshape(w[n].shape), m2.reshape(w[n].shape), v2.reshape(w[n].shape)
    return (loss, dh.reshape(x.shape), *[grads[n] for n in WEIGHTS], *[deltas[n] for n in WEIGHTS],
            *[new_m[n] for n in WEIGHTS], *[new_v[n] for n in WEIGHTS])
```

```python
import functools
import math

import jax
import jax.numpy as jnp
import numpy as np
from jax import lax
from jax.experimental import pallas as pl
from jax.experimental.pallas import tpu as pltpu

F32 = jnp.float32
BF = jnp.bfloat16
HI = lax.Precision.HIGHEST
MESH = pl.DeviceIdType.MESH

D_MODEL = 2048
D_FF = 4096
DEPTH = 4
GDN_HEADS = 8
HEAD_W = 128
GDN_W = GDN_HEADS * HEAD_W
CONV_K = 4
CHUNK = 64
POOL_WINDOWS = (2, 4, 8, 16)
POOL_W = 1024
POOL_GROUP_W = 256
EVEN_IN = 5136
MLA_HEADS = 16
LORA = 512
ROPE = 64
QK_HEAD = HEAD_W + ROPE
ODD_IN = 2 * LORA + ROPE
ODD_IN_PAD = 2 * LORA + HEAD_W
ROPE_THETA = 10000.0
EPS = 1e-6
N_CHIPS = 4

ADAM_LR = 0.001
ADAM_B1 = 0.9
ADAM_B2 = 0.999
ADAM_EPS = 1e-08
ADAM_WD = 0.01
ADAM_STEP = 10

ROW_BLOCK = 256
ROW_BLOCK_ELEMS = 256 * 1024
COL_BLOCK = 1024
MM_TILE = 1024
MM_TILE_K = 2048
VMEM_LIMIT = 56 * 1024 * 1024


def _cparams(sem=None):
    return pltpu.CompilerParams(dimension_semantics=sem, vmem_limit_bytes=VMEM_LIMIT)


def _bdot(a, b, ca, cb):
    return lax.dot_general(a.astype(BF), b.astype(BF), (((ca,), (cb,)), ((), ())),
                           preferred_element_type=F32)


@jax.custom_vjp
def mm_nn(a, b):
    return _bdot(a, b, 1, 0)


@jax.custom_vjp
def mm_nt(a, b):
    return _bdot(a, b, 1, 1)


@jax.custom_vjp
def mm_tn(a, b):
    return _bdot(a, b, 0, 0)


mm_nn.defvjp(lambda a, b: (mm_nn(a, b), (a, b)), lambda r, g: (mm_nt(g, r[1]), mm_tn(r[0], g)))
mm_nt.defvjp(lambda a, b: (mm_nt(a, b), (a, b)), lambda r, g: (mm_nn(g, r[1]), mm_tn(g, r[0])))
mm_tn.defvjp(lambda a, b: (mm_tn(a, b), (a, b)), lambda r, g: (mm_nt(r[1], g), mm_nn(r[0], g)))


def hdot(a, b):
    return lax.dot_general(a, b, (((1,), (0,)), ((), ())), precision=HI, preferred_element_type=F32)


def hdot_nt(a, b):
    return lax.dot_general(a, b, (((1,), (1,)), ((), ())), precision=HI, preferred_element_type=F32)


def _sigmoid(x):
    return 1.0 / (1.0 + jnp.exp(-x))


def _silu(x):
    return x * _sigmoid(x)


def _softplus(x):
    return jnp.maximum(x, 0.0) + jnp.log(1.0 + jnp.exp(-jnp.abs(x)))


def _tile(dim, cap):
    if dim <= cap:
        return dim
    t = (cap // 128) * 128
    while t >= 128:
        if dim % t == 0:
            return t
        t -= 128
    raise ValueError(f"no tile for {dim}")


def _as_tuple(r):
    return tuple(r) if isinstance(r, (tuple, list)) else (r,)


def _row_block(t, cols=None):
    rows = ROW_BLOCK if cols is None else max(ROW_BLOCK, ROW_BLOCK_ELEMS // cols)
    rows = min(rows, t)
    assert t % rows == 0, (t, rows)
    return rows


def matmul(a, b, mode, *, name, alpha=1.0, add=None, out_dtype=F32, out_shards=1):
    if mode == "nn":
        (m, k), (k2, n) = a.shape, b.shape
    elif mode == "nt":
        (m, k), (n, k2) = a.shape, b.shape
    else:
        (k, m), (k2, n) = a.shape, b.shape
    assert k == k2, (a.shape, b.shape, mode)
    tm, tn, tk = _tile(m, MM_TILE), _tile(n, MM_TILE), _tile(k, MM_TILE_K)
    nk = k // tk
    ca = 0 if mode == "tn" else 1
    cb = 1 if mode == "nt" else 0
    a_spec = (pl.BlockSpec((tk, tm), lambda i, j, kk: (kk, i)) if mode == "tn"
              else pl.BlockSpec((tm, tk), lambda i, j, kk: (i, kk)))
    b_spec = (pl.BlockSpec((tn, tk), lambda i, j, kk: (j, kk)) if mode == "nt"
              else pl.BlockSpec((tk, tn), lambda i, j, kk: (kk, j)))
    o_spec = pl.BlockSpec((tm, tn), lambda i, j, kk: (i, j))
    has_add = add is not None

    def body(*refs):
        if has_add:
            a_ref, b_ref, add_ref, o_ref, acc_ref = refs
        else:
            a_ref, b_ref, o_ref, acc_ref = refs
        kk = pl.program_id(2)

        @pl.when(kk == 0)
        def _():
            acc_ref[...] = jnp.zeros_like(acc_ref)

        acc_ref[...] += _bdot(a_ref[...], b_ref[...], ca, cb)

        @pl.when(kk == nk - 1)
        def _():
            r = acc_ref[...]
            if alpha != 1.0:
                r = r * alpha
            if has_add:
                r = r + add_ref[...].astype(F32)
            o_ref[...] = r.astype(out_dtype)

    in_specs = [a_spec, b_spec] + ([o_spec] if has_add else [])
    args = (a, b) + ((add,) if has_add else ())
    out_spec, out_shape = o_spec, (m, n)
    if out_shards > 1:
        assert not has_add and (n // out_shards) % tn == 0
        per = n // out_shards // tn
        out_spec = pl.BlockSpec((None, tm, tn), lambda i, j, kk: (j // per, i, j % per))
        out_shape = (out_shards, m, n // out_shards)
    return pl.pallas_call(
        body, name=name, grid=(m // tm, n // tn, nk), in_specs=in_specs, out_specs=out_spec,
        out_shape=jax.ShapeDtypeStruct(out_shape, out_dtype),
        scratch_shapes=[pltpu.VMEM((tm, tn), F32)],
        compiler_params=_cparams(("parallel", "parallel", "arbitrary")),
    )(*args)


def _row_spec(tb, bc, cf):
    return pl.BlockSpec((tb, bc), lambda i, j, cf=cf: (i, cf(j)))


def _par_spec(p):
    return pl.BlockSpec(p.shape, lambda i, j: (0, 0))


def rowwise(f, rows, pars, outs, *, ncol=1, name):
    t = rows[0][0].shape[0]
    tb = _row_block(t, max([bc for _, bc, _ in rows] + [bc for _, bc, _, _ in outs]))
    nr = len(rows)

    def body(*refs):
        vals = [r[...].astype(F32) for r in refs[:nr + len(pars)]]
        res = _as_tuple(f(*vals))
        for o_ref, r in zip(refs[nr + len(pars):], res):
            o_ref[...] = r.astype(o_ref.dtype)

    return pl.pallas_call(
        body, name=name, grid=(t // tb, ncol),
        in_specs=[_row_spec(tb, bc, cf) for _, bc, cf in rows] + [_par_spec(p) for p in pars],
        out_specs=[_row_spec(tb, bc, cf) for _, bc, cf, _ in outs],
        out_shape=[jax.ShapeDtypeStruct((t, tc), dt) for tc, _, _, dt in outs],
        compiler_params=_cparams(("parallel", "arbitrary")),
    )(*[r[0] for r in rows], *pars)


def rowwise_vjp(f, rows, pars, cts, row_grads, *, ncol=1, name, par_grads=True, consts=(), grad_dtype=F32):
    t = rows[0][0].shape[0]
    tb = _row_block(t, max([bc for _, bc, _ in rows + cts] + [g[1] for g in row_grads if g is not None]))
    consts = list(consts)
    nr, npar, nct, ncon = len(rows), len(pars), len(cts), len(consts)
    diff_rows = [i for i, g in enumerate(row_grads) if g is not None]
    adds = [row_grads[i][3] for i in diff_rows]
    add_idx = [i for i, a in enumerate(adds) if a is not None]

    def body(*refs):
        pos = 0
        row_refs = refs[pos:pos + nr]; pos += nr
        par_refs = refs[pos:pos + npar]; pos += npar
        con_refs = refs[pos:pos + ncon]; pos += ncon
        ct_refs = refs[pos:pos + nct]; pos += nct
        add_refs = refs[pos:pos + len(add_idx)]; pos += len(add_idx)
        grow_refs = refs[pos:pos + len(diff_rows)]; pos += len(diff_rows)
        gpar_refs = refs[pos:]
        row_vals = [r[...].astype(F32) for r in row_refs]
        par_vals = [r[...].astype(F32) for r in par_refs]
        con_vals = [r[...].astype(F32) for r in con_refs]

        def g(*dvals):
            rv = list(row_vals)
            for i, v in zip(diff_rows, dvals[:len(diff_rows)]):
                rv[i] = v
            pv = dvals[len(diff_rows):] if par_grads else par_vals
            return _as_tuple(f(*rv, *pv, *con_vals))

        prim = [row_vals[i] for i in diff_rows] + (par_vals if par_grads else [])
        _, pull = jax.vjp(g, *prim)
        grads = pull(tuple(c[...].astype(F32) for c in ct_refs))
        for n, ref in enumerate(grow_refs):
            gr = grads[n]
            if n in add_idx:
                gr = gr + add_refs[add_idx.index(n)][...]
            ref[...] = gr.astype(ref.dtype)
        if par_grads:
            first = jnp.logical_and(pl.program_id(0) == 0, pl.program_id(1) == 0)
            for ref, gr in zip(gpar_refs, grads[len(diff_rows):]):
                @pl.when(first)
                def _(ref=ref):
                    ref[...] = jnp.zeros_like(ref)
                ref[...] += gr

    gspecs = [row_grads[i] for i in diff_rows]
    in_specs = ([_row_spec(tb, bc, cf) for _, bc, cf in rows] + [_par_spec(p) for p in pars + consts]
                + [_row_spec(tb, bc, cf) for _, bc, cf in cts]
                + [_row_spec(tb, gspecs[i][1], gspecs[i][2]) for i in add_idx])
    out_specs = [_row_spec(tb, bc, cf) for _, bc, cf, _ in gspecs]
    out_shape = [jax.ShapeDtypeStruct((t, tc), grad_dtype) for tc, _, _, _ in gspecs]
    if par_grads:
        out_specs += [_par_spec(p) for p in pars]
        out_shape += [jax.ShapeDtypeStruct(p.shape, F32) for p in pars]
    res = pl.pallas_call(
        body, name=name, grid=(t // tb, ncol), in_specs=in_specs, out_specs=out_specs,
        out_shape=out_shape, compiler_params=_cparams(("arbitrary", "arbitrary")),
    )(*[r[0] for r in rows], *pars, *consts, *[c[0] for c in cts], *[adds[i] for i in add_idx])
    return list(res[:len(diff_rows)]), list(res[len(diff_rows):])


def _col0(j):
    return 0


def _colj(j):
    return j


def _full(a):
    return (a, a.shape[1], _col0)


def _rms(x, gain):
    return x * lax.rsqrt(jnp.mean(x * x, axis=-1, keepdims=True) + EPS) * gain


def _swiglu_act(g, u):
    return _silu(g) * u


def ffn_fwd(x, gain, wg, wu, wd, tag):
    d = x.shape[1]
    (h,) = rowwise(_rms, [_full(x)], [gain], [(d, d, _col0, BF)], name=f"{tag}_norm")
    g = matmul(h, wg, "nn", name=f"{tag}_gate")
    u = matmul(h, wu, "nn", name=f"{tag}_up")
    f = g.shape[1]
    cb = _tile(f, COL_BLOCK)
    (a,) = rowwise(_swiglu_act, [(g, cb, _colj), (u, cb, _colj)], [], [(f, cb, _colj, BF)], ncol=f // cb,
                   name=f"{tag}_act")
    y = matmul(a, wd, "nn", alpha=0.5, add=x, name=f"{tag}_down")
    return y, (x, h, g, u, a)


def ffn_bwd(dy, saved, gain, wg, wu, wd, tag):
    x, h, g, u, a = saved
    d, f = x.shape[1], g.shape[1]
    da = matmul(dy, wd, "nt", alpha=0.5, name=f"{tag}_down_dx")
    dwd = matmul(a, dy, "tn", alpha=0.5, name=f"{tag}_down_dw")
    cb = _tile(f, COL_BLOCK)
    (dg, du), _ = rowwise_vjp(_swiglu_act, [(g, cb, _colj), (u, cb, _colj)], [], [(da, cb, _colj)],
                              [(f, cb, _colj, None), (f, cb, _colj, None)], ncol=f // cb,
                              name=f"{tag}_act_bwd", par_grads=False, grad_dtype=BF)
    dwg = matmul(h, dg, "tn", name=f"{tag}_gate_dw", out_shards=N_CHIPS)
    dwu = matmul(h, du, "tn", name=f"{tag}_up_dw", out_shards=N_CHIPS)
    dh = matmul(dg, wg, "nt", name=f"{tag}_gate_dx")
    dh = matmul(du, wu, "nt", add=dh, name=f"{tag}_up_dx")
    (dx,), (dgain,) = rowwise_vjp(_rms, [_full(x)], [gain], [_full(dh)], [(d, d, _col0, dy)],
                                  name=f"{tag}_norm_bwd")
    return dx, dgain, dwg, dwu, dwd


def rope_tables(positions, inv_freq, sign):
    t = positions.shape[0]
    tb = _row_block(t)

    def body(pos_ref, f_ref, s_ref, c_ref, sn_ref):
        ang = pos_ref[...].astype(F32) * f_ref[...]
        live = jnp.abs(s_ref[...])
        c_ref[...] = jnp.cos(ang) * live
        sn_ref[...] = jnp.sin(ang) * s_ref[...]

    return pl.pallas_call(
        body, name="rope_tables", grid=(t // tb,),
        in_specs=[pl.BlockSpec((tb, 1), lambda i: (i, 0)), pl.BlockSpec((1, HEAD_W), lambda i: (0, 0)),
                  pl.BlockSpec((1, HEAD_W), lambda i: (0, 0))],
        out_specs=[pl.BlockSpec((tb, HEAD_W), lambda i: (i, 0))] * 2,
        out_shape=[jax.ShapeDtypeStruct((t, HEAD_W), F32)] * 2,
        compiler_params=_cparams(("parallel",)),
    )(positions, inv_freq, sign)


def _rope(p, c, s, swap):
    return p * c + hdot(p, swap) * s


def _pe_norm(pe, gp):
    return pe * lax.rsqrt(jnp.sum(pe * pe, axis=-1, keepdims=True) * (1.0 / ROPE) + EPS) * gp


def _q_head(nope, pe, c, s, gn, gp, swap):
    return _rms(nope, gn), _rope(_pe_norm(pe, gp), c, s, swap)


def _k_head(nope, v, gn):
    return _rms(nope, gn), v


def _kpe_head(pe, c, s, gp, swap):
    return _rope(_pe_norm(pe, gp), c, s, swap)


ATT_BLOCK = 1024
ATT_SCALE = QK_HEAD ** -0.5
NEG = float(np.finfo(np.float32).min)


def _att_block(t):
    return min(ATT_BLOCK, t)


def _scores(qn, qp, kn, kp, diag):
    q = jnp.concatenate([qn, qp], axis=1)
    k = jnp.concatenate([kn, kp], axis=1)
    s = lax.dot_general(q, k, (((1,), (1,)), ((), ())), preferred_element_type=F32) * ATT_SCALE
    if diag:
        rows = lax.broadcasted_iota(jnp.int32, s.shape, 0)
        cols = lax.broadcasted_iota(jnp.int32, s.shape, 1)
        s = jnp.where(rows >= cols, s, NEG)
    return s, q, k


def _below_or_on_diagonal(i, j, step):
    @pl.when(j < i)
    def _():
        step(False)

    @pl.when(j == i)
    def _():
        step(True)


def attention_fwd(qn, qp, kn, kp, v):
    t = qn.shape[0]
    h = qn.shape[1] // HEAD_W
    tq = _att_block(t)
    nq = t // tq

    def body(qn_ref, qp_ref, kn_ref, kp_ref, v_ref, o_ref, lse_ref, m_ref, l_ref, acc_ref):
        i, j = pl.program_id(1), pl.program_id(2)

        @pl.when(j == 0)
        def _():
            m_ref[...] = jnp.full_like(m_ref, NEG)
            l_ref[...] = jnp.zeros_like(l_ref)
            acc_ref[...] = jnp.zeros_like(acc_ref)

        def step(diag):
            s, _, _ = _scores(qn_ref[...], qp_ref[...], kn_ref[...], kp_ref[...], diag)
            m_new = jnp.maximum(m_ref[...], jnp.max(s, axis=-1, keepdims=True))
            a = jnp.exp(m_ref[...] - m_new)
            p = jnp.exp(s - m_new)
            l_ref[...] = a * l_ref[...] + jnp.sum(p, axis=-1, keepdims=True)
            acc_ref[...] = a * acc_ref[...] + jnp.dot(p.astype(BF), v_ref[...], preferred_element_type=F32)
            m_ref[...] = m_new

        _below_or_on_diagonal(i, j, step)

        @pl.when(j == nq - 1)
        def _():
            o_ref[...] = acc_ref[...] / l_ref[...]
            lse_ref[...] = m_ref[...] + jnp.log(l_ref[...])

    qspec = pl.BlockSpec((tq, HEAD_W), lambda hh, i, j: (i, hh))
    kspec = pl.BlockSpec((tq, HEAD_W), lambda hh, i, j: (jnp.minimum(i, j), hh))
    kpspec = pl.BlockSpec((tq, HEAD_W), lambda hh, i, j: (jnp.minimum(i, j), 0))
    return pl.pallas_call(
        body, name="attention_fwd", grid=(h, nq, nq),
        in_specs=[qspec, qspec, kspec, kpspec, kspec],
        out_specs=[qspec, pl.BlockSpec((None, tq, 1), lambda hh, i, j: (hh, i, 0))],
        out_shape=[jax.ShapeDtypeStruct((t, h * HEAD_W), F32), jax.ShapeDtypeStruct((h, t, 1), F32)],
        scratch_shapes=[pltpu.VMEM((tq, 1), F32), pltpu.VMEM((tq, 1), F32), pltpu.VMEM((tq, HEAD_W), F32)],
        compiler_params=_cparams(("parallel", "parallel", "arbitrary")),
    )(qn, qp, kn, kp, v)


def attention_delta(o, do):
    t = o.shape[0]
    h = o.shape[1] // HEAD_W
    tq = _att_block(t)

    def body(o_ref, do_ref, d_ref):
        d_ref[...] = jnp.sum(o_ref[...] * do_ref[...], axis=-1, keepdims=True)

    spec = pl.BlockSpec((tq, HEAD_W), lambda hh, i: (i, hh))
    return pl.pallas_call(
        body, name="attention_delta", grid=(h, t // tq), in_specs=[spec, spec],
        out_specs=pl.BlockSpec((None, tq, 1), lambda hh, i: (hh, i, 0)),
        out_shape=jax.ShapeDtypeStruct((h, t, 1), F32),
        compiler_params=_cparams(("parallel", "parallel")),
    )(o, do)


def _att_grads(qn_ref, qp_ref, kn_ref, kp_ref, v_ref, do_ref, lse_ref, dl_ref, diag):
    s, q, k = _scores(qn_ref[...], qp_ref[...], kn_ref[...], kp_ref[...], diag)
    p = jnp.exp(s - lse_ref[...])
    do = do_ref[...].astype(BF)
    dp = lax.dot_general(do, v_ref[...], (((1,), (1,)), ((), ())), preferred_element_type=F32)
    ds = p * (dp - dl_ref[...]) * ATT_SCALE
    return p, ds, q, k, do


def attention_bwd_q(qn, qp, kn, kp, v, do, lse, delta, name, carry=()):
    t = qn.shape[0]
    h = qn.shape[1] // HEAD_W
    tq = _att_block(t)
    nq = t // tq

    def body(qn_ref, qp_ref, kn_ref, kp_ref, v_ref, do_ref, lse_ref, dl_ref, dqn_ref, dqp_ref, acc_ref):
        i, j = pl.program_id(1), pl.program_id(2)

        @pl.when(j == 0)
        def _():
            acc_ref[...] = jnp.zeros_like(acc_ref)

        def step(diag):
            _, ds, _, k, _ = _att_grads(qn_ref, qp_ref, kn_ref, kp_ref, v_ref, do_ref, lse_ref, dl_ref, diag)
            acc_ref[...] += jnp.dot(ds.astype(BF), k, preferred_element_type=F32)

        _below_or_on_diagonal(i, j, step)

        @pl.when(j == nq - 1)
        def _():
            dqn_ref[...] = acc_ref[:, :HEAD_W]
            dqp_ref[...] = acc_ref[:, HEAD_W:]

    qspec = pl.BlockSpec((tq, HEAD_W), lambda hh, i, j: (i, hh))
    kspec = pl.BlockSpec((tq, HEAD_W), lambda hh, i, j: (jnp.minimum(i, j), hh))
    kpspec = pl.BlockSpec((tq, HEAD_W), lambda hh, i, j: (jnp.minimum(i, j), 0))
    vec = pl.BlockSpec((None, tq, 1), lambda hh, i, j: (hh, i, 0))
    call = dict(
        name=name, grid=(h, nq, nq),
        in_specs=[qspec, qspec, kspec, kpspec, kspec, qspec, vec, vec],
        out_specs=[qspec, qspec],
        out_shape=[jax.ShapeDtypeStruct((t, h * HEAD_W), F32)] * 2,
        scratch_shapes=[pltpu.VMEM((tq, 2 * HEAD_W), F32)],
    )
    args = (qn, qp, kn, kp, v, do, lse, delta)
    if carry:
        res = carried_call(body, carry, args=args, **call)
        return res[:2], res[2:]
    return pl.pallas_call(body, compiler_params=_cparams(("parallel", "parallel", "arbitrary")), **call)(*args), []


def attention_bwd_kv(qn, qp, kn, kp, v, do, lse, delta):
    t = qn.shape[0]
    h = qn.shape[1] // HEAD_W
    tq = _att_block(t)
    nq = t // tq

    def body(qn_ref, qp_ref, kn_ref, kp_ref, v_ref, do_ref, lse_ref, dl_ref, dkn_ref, dkp_ref, dv_ref,
             dk_acc, dv_acc):
        j, hh, i = pl.program_id(0), pl.program_id(1), pl.program_id(2)

        @pl.when(i == 0)
        def _():
            dk_acc[...] = jnp.zeros_like(dk_acc)
            dv_acc[...] = jnp.zeros_like(dv_acc)

        @pl.when(jnp.logical_and(i == 0, hh == 0))
        def _():
            dkp_ref[...] = jnp.zeros_like(dkp_ref)

        def step(diag):
            p, ds, q, _, do = _att_grads(qn_ref, qp_ref, kn_ref, kp_ref, v_ref, do_ref, lse_ref, dl_ref, diag)
            dv_acc[...] += lax.dot_general(p.astype(BF), do, (((0,), (0,)), ((), ())), preferred_element_type=F32)
            dk_acc[...] += lax.dot_general(ds.astype(BF), q, (((0,), (0,)), ((), ())), preferred_element_type=F32)

        _below_or_on_diagonal(i, j, step)

        @pl.when(i == nq - 1)
        def _():
            dkn_ref[...] = dk_acc[:, :HEAD_W]
            dkp_ref[...] += dk_acc[:, HEAD_W:]
            dv_ref[...] = dv_acc[...]

    qspec = pl.BlockSpec((tq, HEAD_W), lambda j, hh, i: (jnp.maximum(i, j), hh))
    kspec = pl.BlockSpec((tq, HEAD_W), lambda j, hh, i: (j, hh))
    kpspec = pl.BlockSpec((tq, HEAD_W), lambda j, hh, i: (j, 0))
    vec = pl.BlockSpec((None, tq, 1), lambda j, hh, i: (hh, jnp.maximum(i, j), 0))
    return pl.pallas_call(
        body, name="attention_bwd_kv", grid=(nq, h, nq),
        in_specs=[qspec, qspec, kspec, kpspec, kspec, qspec, vec, vec],
        out_specs=[kspec, kpspec, kspec],
        out_shape=[jax.ShapeDtypeStruct((t, h * HEAD_W), F32), jax.ShapeDtypeStruct((t, HEAD_W), F32),
                   jax.ShapeDtypeStruct((t, h * HEAD_W), F32)],
        scratch_shapes=[pltpu.VMEM((tq, 2 * HEAD_W), F32), pltpu.VMEM((tq, HEAD_W), F32)],
        compiler_params=_cparams(("parallel", "arbitrary", "arbitrary")),
    )(qn, qp, kn, kp, v, do, lse, delta)


def _even(j):
    return 2 * j


def _odd(j):
    return 2 * j + 1


def _rope_consts():
    lane = np.arange(HEAD_W)
    half = ROPE // 2
    swap = np.zeros((HEAD_W, HEAD_W), np.float32)
    swap[lane[:half] + half, lane[:half]] = 1.0
    swap[lane[:half], lane[:half] + half] = 1.0
    sign = np.where(lane < half, -1.0, np.where(lane < ROPE, 1.0, 0.0)).astype(np.float32)[None]
    inv_freq = ROPE_THETA ** (-jnp.arange(0, ROPE, 2, dtype=F32) / ROPE)
    inv_freq = jnp.concatenate([inv_freq, inv_freq, jnp.zeros((HEAD_W - ROPE,), F32)])[None]
    return jnp.asarray(swap), jnp.asarray(sign), inv_freq


def _pad_gain(g):
    return g[None, :HEAD_W], jnp.pad(g[HEAD_W:], (0, HEAD_W - ROPE))[None]


def mla_fwd(x, cos, sin, mix_gain, w_in, q_gain, kv_gain, w_q, w_kv, qh_gain, kh_gain, w_out, tag):
    d = x.shape[1]
    nh = MLA_HEADS
    swap = _rope_consts()[0]
    (h,) = rowwise(_rms, [_full(x)], [mix_gain], [(d, d, _col0, BF)], name=f"{tag}_norm")
    proj = matmul(h, w_in, "nn", name=f"{tag}_in")
    (qlat,) = rowwise(_rms, [(proj, LORA, _col0)], [q_gain], [(LORA, LORA, _col0, BF)], name=f"{tag}_qnorm")
    (kvlat,) = rowwise(_rms, [(proj, LORA, lambda j: 1)], [kv_gain], [(LORA, LORA, _col0, BF)],
                       name=f"{tag}_kvnorm")
    q = matmul(qlat, w_q, "nn", name=f"{tag}_qup")
    kv = matmul(kvlat, w_kv, "nn", name=f"{tag}_kvup")
    qgn, qgp = _pad_gain(qh_gain)
    kgn, kgp = _pad_gain(kh_gain)
    w = nh * HEAD_W
    qn, qp = rowwise(_q_head, [(q, HEAD_W, _even), (q, HEAD_W, _odd), _full(cos), _full(sin)], [qgn, qgp, swap],
                     [(w, HEAD_W, _colj, BF), (w, HEAD_W, _colj, BF)], ncol=nh, name=f"{tag}_qhead")
    kn, v = rowwise(_k_head, [(kv, HEAD_W, _even), (kv, HEAD_W, _odd)], [kgn],
                    [(w, HEAD_W, _colj, BF), (w, HEAD_W, _colj, BF)], ncol=nh, name=f"{tag}_khead")
    (kp,) = rowwise(_kpe_head, [(proj, HEAD_W, lambda j: 2 * LORA // HEAD_W), _full(cos), _full(sin)], [kgp, swap],
                    [(HEAD_W, HEAD_W, _col0, BF)], name=f"{tag}_kpe")
    o, lse = attention_fwd(qn, qp, kn, kp, v)
    y = matmul(o, w_out, "nn", add=x, name=f"{tag}_out")
    return y, (x, h, proj, qlat, kvlat, q, kv, qn, qp, kn, kp, v, o, lse)


def mla_bwd(dy, saved, cos, sin, mix_gain, w_in, q_gain, kv_gain, w_q, w_kv, qh_gain, kh_gain, w_out, tag, carry=()):
    x, h, proj, qlat, kvlat, q, kv, qn, qp, kn, kp, v, o, lse = saved
    d = x.shape[1]
    nh = MLA_HEADS
    w = nh * HEAD_W
    swap = _rope_consts()[0]
    qgn, qgp = _pad_gain(qh_gain)
    kgn, kgp = _pad_gain(kh_gain)
    do = matmul(dy, w_out, "nt", name=f"{tag}_out_dx")
    dw_out = matmul(o, dy, "tn", name=f"{tag}_out_dw")
    delta = attention_delta(o, do)
    (dqn, dqp), carried = attention_bwd_q(qn, qp, kn, kp, v, do, lse, delta, f"{tag}_attention_bwd_q", carry)
    dkn, dkp, dv = attention_bwd_kv(qn, qp, kn, kp, v, do, lse, delta)
    (dq_a, dq_b), (dqgn, dqgp) = rowwise_vjp(
        _q_head, [(q, HEAD_W, _even), (q, HEAD_W, _odd), _full(cos), _full(sin)], [qgn, qgp],
        [(dqn, HEAD_W, _colj), (dqp, HEAD_W, _colj)],
        [(w, HEAD_W, _colj, None), (w, HEAD_W, _colj, None), None, None], ncol=nh, consts=[swap],
        name=f"{tag}_qhead_bwd")
    dq = _interleave(dq_a, dq_b)
    (dkv_a, dkv_b), (dkgn,) = rowwise_vjp(
        _k_head, [(kv, HEAD_W, _even), (kv, HEAD_W, _odd)], [kgn], [(dkn, HEAD_W, _colj), (dv, HEAD_W, _colj)],
        [(w, HEAD_W, _colj, None), (w, HEAD_W, _colj, None)], ncol=nh, name=f"{tag}_khead_bwd")
    dkv = _interleave(dkv_a, dkv_b)
    (dpe,), (dkgp,) = rowwise_vjp(
        _kpe_head, [(proj, HEAD_W, lambda j: 2 * LORA // HEAD_W), _full(cos), _full(sin)], [kgp], [_full(dkp)],
        [(HEAD_W, HEAD_W, _col0, None), None, None], consts=[swap], name=f"{tag}_kpe_bwd")
    dw_q = matmul(qlat, dq, "tn", name=f"{tag}_qup_dw")
    dw_kv = matmul(kvlat, dkv, "tn", name=f"{tag}_kvup_dw")
    dqlat = matmul(dq, w_q, "nt", name=f"{tag}_qup_dx")
    dkvlat = matmul(dkv, w_kv, "nt", name=f"{tag}_kvup_dx")
    (dpq,), (dq_gain,) = rowwise_vjp(_rms, [(proj, LORA, _col0)], [q_gain], [_full(dqlat)],
                                     [(LORA, LORA, _col0, None)], name=f"{tag}_qnorm_bwd")
    (dpkv,), (dkv_gain,) = rowwise_vjp(_rms, [(proj, LORA, lambda j: 1)], [kv_gain], [_full(dkvlat)],
                                       [(LORA, LORA, _col0, None)], name=f"{tag}_kvnorm_bwd")
    dproj = jnp.concatenate([dpq, dpkv, dpe], axis=1)
    dw_in = matmul(h, dproj, "tn", name=f"{tag}_in_dw")
    dh = matmul(dproj, w_in, "nt", name=f"{tag}_in_dx")
    (dx,), (dmix,) = rowwise_vjp(_rms, [_full(x)], [mix_gain], [_full(dh)], [(d, d, _col0, dy)],
                                 name=f"{tag}_norm_bwd")
    dqh = jnp.concatenate([dqgn[0], dqgp[0, :ROPE]])
    dkh = jnp.concatenate([dkgn[0], dkgp[0, :ROPE]])
    return dx, (dmix, dw_in, dq_gain, dkv_gain, dw_q, dw_kv, dqh, dkh, dw_out), carried


def _interleave(a, b):
    t, w = a.shape
    n = w // HEAD_W
    return jnp.stack([a.reshape(t, n, HEAD_W), b.reshape(t, n, HEAD_W)], axis=2).reshape(t, 2 * w)


GDN_HEADS_PER_STEP = 4
CONV_COLS = 512
HALO = 8


def conv_fwd(x, w):
    t = x.shape[0]
    c = w.shape[1]
    tb = _row_block(t)
    assert t % tb == 0

    def body(x_ref, prev_ref, w_ref, y_ref):
        i = pl.program_id(1)
        xv = x_ref[...]
        prev = jnp.where(i > 0, prev_ref[...], 0.0)
        ext = jnp.concatenate([prev, xv], axis=0)
        acc = xv * w_ref[CONV_K - 1:CONV_K, :]
        for s in range(1, CONV_K):
            acc = acc + pltpu.roll(ext, s, 0)[HALO:] * w_ref[CONV_K - 1 - s:CONV_K - s, :]
        y_ref[...] = acc

    spec = pl.BlockSpec((tb, CONV_COLS), lambda j, i: (i, j))
    return pl.pallas_call(
        body, name="conv_fwd", grid=(c // CONV_COLS, t // tb),
        in_specs=[spec, pl.BlockSpec((HALO, CONV_COLS), lambda j, i: (jnp.maximum(i * (tb // HALO) - 1, 0), j)),
                  pl.BlockSpec((HALO, CONV_COLS), lambda j, i: (0, j))],
        out_specs=spec, out_shape=jax.ShapeDtypeStruct((t, c), F32),
        compiler_params=_cparams(("parallel", "parallel")),
    )(x, x, w)


def conv_bwd(x, dy, w):
    t = x.shape[0]
    c = w.shape[1]
    tb = _row_block(t)
    nrb = t // tb

    def body(x_ref, prev_ref, dy_ref, next_ref, w_ref, dx_ref, dw_ref):
        i = pl.program_id(1)
        ext_x = jnp.concatenate([jnp.where(i > 0, prev_ref[...], 0.0), x_ref[...]], axis=0)
        dyv = dy_ref[...]
        ext_dy = jnp.concatenate([dyv, jnp.where(i < nrb - 1, next_ref[...], 0.0)], axis=0)

        @pl.when(i == 0)
        def _():
            dw_ref[...] = jnp.zeros_like(dw_ref)

        acc = dyv * w_ref[CONV_K - 1:CONV_K, :]
        dw_ref[CONV_K - 1:CONV_K, :] += jnp.sum(dyv * x_ref[...], axis=0, keepdims=True)
        for s in range(1, CONV_K):
            acc = acc + pltpu.roll(ext_dy, tb + HALO - s, 0)[:tb] * w_ref[CONV_K - 1 - s:CONV_K - s, :]
            dw_ref[CONV_K - 1 - s:CONV_K - s, :] += jnp.sum(dyv * pltpu.roll(ext_x, s, 0)[HALO:], axis=0, keepdims=True)
        dx_ref[...] = acc

    spec = pl.BlockSpec((tb, CONV_COLS), lambda j, i: (i, j))
    wspec = pl.BlockSpec((HALO, CONV_COLS), lambda j, i: (0, j))
    return pl.pallas_call(
        body, name="conv_bwd", grid=(c // CONV_COLS, nrb),
        in_specs=[spec, pl.BlockSpec((HALO, CONV_COLS), lambda j, i: (jnp.maximum(i * (tb // HALO) - 1, 0), j)),
                  spec, pl.BlockSpec((HALO, CONV_COLS), lambda j, i: (jnp.minimum(i + 1, nrb - 1) * (tb // HALO), j)),
                  wspec],
        out_specs=[spec, wspec],
        out_shape=[jax.ShapeDtypeStruct((t, c), F32), jax.ShapeDtypeStruct((HALO, c), F32)],
        compiler_params=_cparams(("parallel", "arbitrary")),
    )(x, x, dy, dy, w)


def _l2_silu(c):
    a = _silu(c)
    return a * lax.rsqrt(jnp.sum(a * a, axis=-1, keepdims=True) + EPS)


def _gates(ab, a_log, dt_bias, ea, eb):
    g = -jnp.exp(a_log) * _softplus(hdot(ab, ea) + dt_bias)
    return g, _sigmoid(hdot(ab, eb))


def _gate_consts():
    ea = np.zeros((HEAD_W, GDN_W), np.float32)
    eb = np.zeros((HEAD_W, GDN_W), np.float32)
    for h in range(GDN_HEADS):
        ea[h, h * HEAD_W:(h + 1) * HEAD_W] = 1.0
        eb[GDN_HEADS + h, h * HEAD_W:(h + 1) * HEAD_W] = 1.0
    return jnp.asarray(ea), jnp.asarray(eb)


def _batched_dots(precision, to_bf16):
    def raw(a, b, ca, cb):
        if to_bf16:
            a, b = a.astype(BF), b.astype(BF)
        return lax.dot_general(a, b, (((ca,), (cb,)), ((0,), (0,))), precision=precision,
                               preferred_element_type=F32)

    nn = jax.custom_vjp(lambda a, b: raw(a, b, 2, 1))
    nt = jax.custom_vjp(lambda a, b: raw(a, b, 2, 2))
    tn = jax.custom_vjp(lambda a, b: raw(a, b, 1, 1))
    nn.defvjp(lambda a, b: (nn(a, b), (a, b)), lambda r, g: (nt(g, r[1]), tn(r[0], g)))
    nt.defvjp(lambda a, b: (nt(a, b), (a, b)), lambda r, g: (nn(g, r[1]), tn(g, r[0])))
    tn.defvjp(lambda a, b: (tn(a, b), (a, b)), lambda r, g: (nt(r[1], g), nn(r[0], g)))
    return nn, nt, tn


bmm_nn, bmm_nt, bmm_tn = _batched_dots(None, True)
bh_nn, bh_nt, bh_tn = _batched_dots(HI, False)


def _gdn_chunk(q, k, v, g, beta, state):
    n = CHUNK
    nb = q.shape[0]
    ii = lax.broadcasted_iota(jnp.int32, (nb, n, n), 1)
    jj = lax.broadcasted_iota(jnp.int32, (nb, n, n), 2)
    causal = ii >= jj
    eye = (ii == jj).astype(F32)
    gc = bh_nn(causal.astype(F32), g)
    mean_w = jnp.full((nb, n, HEAD_W), 1.0 / HEAD_W, F32)
    gc_i = bh_nt(gc, mean_w)
    gc_j = bh_nt(mean_w, gc)
    decay = jnp.exp(jnp.where(causal, gc_i - gc_j, -1e30))
    kb = k * beta
    vb = v * beta
    m = jnp.where(ii > jj, bmm_nt(kb, k) * decay, 0.0)
    inv = eye - m
    pw = m
    for _ in range(5):
        pw = bh_nn(pw, pw)
        inv = bh_nn(inv, eye + pw)
    eg = jnp.exp(gc)
    u = bh_nn(inv, vb)
    w = bh_nn(inv, kb * eg)
    qs = q * (HEAD_W ** -0.5)
    attn = bmm_nt(qs, k) * decay
    g_last = bh_nn((jj == n - 1).astype(F32), gc)
    k_dec = k * jnp.exp(g_last - gc)
    v_new = u - bmm_nn(w, state)
    out = bmm_nn(qs * eg, state) + bmm_nn(attn, v_new)
    new_state = state * jnp.exp(jnp.concatenate([g_last, g_last], axis=1)) + bmm_tn(k_dec, v_new)
    return out, new_state


def _heads(ref, hb):
    return jnp.stack([ref[:, j * HEAD_W:(j + 1) * HEAD_W] for j in range(hb)])


def gdn_fwd(qk, v, g, beta):
    t = v.shape[0]
    n = t // CHUNK
    nh = GDN_HEADS

    hb = GDN_HEADS_PER_STEP
    ng = nh // hb

    def body(q_ref, k_ref, v_ref, g_ref, b_ref, o_ref, s_ref, state):
        @pl.when(pl.program_id(1) == 0)
        def _():
            state[...] = jnp.zeros_like(state)

        old = state[...]
        out, new = _gdn_chunk(*[_heads(r, hb) for r in (q_ref, k_ref, v_ref, g_ref, b_ref)], old)
        s_ref[:, 0] = old
        for j in range(hb):
            o_ref[:, j * HEAD_W:(j + 1) * HEAD_W] = out[j]
        state[...] = new

    spec = pl.BlockSpec((CHUNK, hb * HEAD_W), lambda h, c: (c, h))
    return pl.pallas_call(
        body, name="gdn_fwd", grid=(ng, n),
        in_specs=[spec, pl.BlockSpec((CHUNK, hb * HEAD_W), lambda h, c: (c, ng + h)), spec, spec, spec],
        out_specs=[spec, pl.BlockSpec((hb, 1, HEAD_W, HEAD_W), lambda h, c: (h, c, 0, 0))],
        out_shape=[jax.ShapeDtypeStruct((t, nh * HEAD_W), F32), jax.ShapeDtypeStruct((nh, n, HEAD_W, HEAD_W), F32)],
        scratch_shapes=[pltpu.VMEM((hb, HEAD_W, HEAD_W), F32)],
        compiler_params=_cparams(("parallel", "arbitrary")),
    )(qk, qk, v, g, beta)


def gdn_bwd(qk, v, g, beta, states, do, name, carry=()):
    t = v.shape[0]
    n = t // CHUNK
    nh = GDN_HEADS

    hb = GDN_HEADS_PER_STEP
    ng = nh // hb

    def body(q_ref, k_ref, v_ref, g_ref, b_ref, s_ref, do_ref, dq_ref, dk_ref, dv_ref, dg_ref, db_ref, dstate):
        @pl.when(pl.program_id(1) == 0)
        def _():
            dstate[...] = jnp.zeros_like(dstate)

        _, pull = jax.vjp(_gdn_chunk, *[_heads(r, hb) for r in (q_ref, k_ref, v_ref, g_ref, b_ref)], s_ref[:, 0])
        grads = pull((_heads(do_ref, hb), dstate[...]))
        for ref, gr in zip((dq_ref, dk_ref, dv_ref, dg_ref, db_ref), grads[:5]):
            for j in range(hb):
                ref[:, j * HEAD_W:(j + 1) * HEAD_W] = gr[j]
        dstate[...] = grads[5]

    spec = pl.BlockSpec((CHUNK, hb * HEAD_W), lambda h, c: (n - 1 - c, h))
    call = dict(
        name=name, grid=(ng, n),
        in_specs=[spec, pl.BlockSpec((CHUNK, hb * HEAD_W), lambda h, c: (n - 1 - c, ng + h)), spec, spec, spec,
                  pl.BlockSpec((hb, 1, HEAD_W, HEAD_W), lambda h, c: (h, n - 1 - c, 0, 0)), spec],
        out_specs=[spec] * 5,
        out_shape=[jax.ShapeDtypeStruct((t, nh * HEAD_W), F32)] * 5,
        scratch_shapes=[pltpu.VMEM((hb, HEAD_W, HEAD_W), F32)],
    )
    args = (qk, qk, v, g, beta, states, do)
    if carry:
        res = carried_call(body, carry, args=args, **call)
        return res[:5], res[5:]
    return pl.pallas_call(body, compiler_params=_cparams(("parallel", "arbitrary")), **call)(*args), []


def _gdn_post(o, z, gain):
    return _rms(o, gain) * _silu(z)


POOL_HALO = 16


def _pool_counts(t0, rows, cols):
    tt = t0 + lax.broadcasted_iota(jnp.int32, (rows, cols), 0) + 1
    grp = lax.broadcasted_iota(jnp.int32, (rows, cols), 1) // POOL_GROUP_W
    win = jnp.left_shift(2, grp)
    return jnp.minimum(tt, win).astype(F32), grp


def _by_group(grp, parts):
    out = parts[-1]
    for gi in range(len(parts) - 2, -1, -1):
        out = jnp.where(grp == gi, parts[gi], out)
    return out


def pool_window_fwd(u):
    t, c = u.shape
    tb = _row_block(t)

    def body(u_ref, prev_ref, d_ref):
        i = pl.program_id(0)
        xv = u_ref[...]
        ext = jnp.concatenate([jnp.where(i > 0, prev_ref[...], 0.0), xv], axis=0)
        sums = []
        s = ext
        for sh in (1, 2, 4, 8):
            s = s + pltpu.roll(s, sh, 0)
            sums.append(s[POOL_HALO:])
        cnt, grp = _pool_counts(i * tb, tb, c)
        d_ref[...] = _by_group(grp, sums) / cnt - xv

    spec = pl.BlockSpec((tb, c), lambda i: (i, 0))
    return pl.pallas_call(
        body, name="pool_window_fwd", grid=(t // tb,),
        in_specs=[spec, pl.BlockSpec((POOL_HALO, c), lambda i: (jnp.maximum(i * (tb // POOL_HALO) - 1, 0), 0))],
        out_specs=spec, out_shape=jax.ShapeDtypeStruct((t, c), F32),
        compiler_params=_cparams(("parallel",)),
    )(u, u)


def pool_window_bwd(dd):
    t, c = dd.shape
    tb = _row_block(t)
    nrb = t // tb
    length = tb + POOL_HALO

    def body(d_ref, next_ref, du_ref):
        i = pl.program_id(0)
        dv = d_ref[...]
        ext = jnp.concatenate([dv, jnp.where(i < nrb - 1, next_ref[...], 0.0)], axis=0)
        cnt, grp = _pool_counts(i * tb, length, c)
        s = ext / cnt
        sums = []
        for sh in (1, 2, 4, 8):
            s = s + pltpu.roll(s, length - sh, 0)
            sums.append(s[:tb])
        du_ref[...] = _by_group(grp[:tb], sums) - dv

    spec = pl.BlockSpec((tb, c), lambda i: (i, 0))
    return pl.pallas_call(
        body, name="pool_window_bwd", grid=(nrb,),
        in_specs=[spec, pl.BlockSpec((POOL_HALO, c), lambda i: (jnp.minimum(i + 1, nrb - 1) * (tb // POOL_HALO), 0))],
        out_specs=spec, out_shape=jax.ShapeDtypeStruct((t, c), F32),
        compiler_params=_cparams(("parallel",)),
    )(dd, dd)


def _pool_mix(d, w, scale):
    return mm_nn(d, w) * scale


def pool_mix_fwd(diff, w, scale):
    t = diff.shape[0]
    tb = _row_block(t)
    gw = POOL_GROUP_W

    def body(d_ref, w_ref, s_ref, o_ref):
        o_ref[...] = _pool_mix(d_ref[...], w_ref[...], s_ref[...]).astype(o_ref.dtype)

    spec = pl.BlockSpec((tb, gw), lambda i, g: (i, g))
    return pl.pallas_call(
        body, name="pool_mix_fwd", grid=(t // tb, POOL_W // gw),
        in_specs=[spec, pl.BlockSpec((None, gw, gw), lambda i, g: (g, 0, 0)), pl.BlockSpec((1, gw), lambda i, g: (0, g))],
        out_specs=spec, out_shape=jax.ShapeDtypeStruct((t, POOL_W), BF),
        compiler_params=_cparams(("parallel", "parallel")),
    )(diff, w, scale)


def pool_mix_bwd(diff, w, scale, dp, dp_col0):
    t = diff.shape[0]
    tb = _row_block(t)
    gw = POOL_GROUP_W

    def body(d_ref, w_ref, s_ref, dp_ref, dd_ref, dw_ref, ds_ref):
        @pl.when(pl.program_id(1) == 0)
        def _():
            dw_ref[...] = jnp.zeros_like(dw_ref)
            ds_ref[...] = jnp.zeros_like(ds_ref)

        _, pull = jax.vjp(_pool_mix, d_ref[...], w_ref[...].astype(F32), s_ref[...])
        dd, dw, ds = pull(dp_ref[...])
        dd_ref[...] = dd
        dw_ref[...] += dw
        ds_ref[...] += ds

    spec = pl.BlockSpec((tb, gw), lambda g, i: (i, g))
    wspec = pl.BlockSpec((None, gw, gw), lambda g, i: (g, 0, 0))
    sspec = pl.BlockSpec((1, gw), lambda g, i: (0, g))
    return pl.pallas_call(
        body, name="pool_mix_bwd", grid=(POOL_W // gw, t // tb),
        in_specs=[spec, wspec, sspec, pl.BlockSpec((tb, gw), lambda g, i: (i, dp_col0 + g))],
        out_specs=[spec, wspec, sspec],
        out_shape=[jax.ShapeDtypeStruct((t, POOL_W), F32), jax.ShapeDtypeStruct(w.shape, F32),
                   jax.ShapeDtypeStruct(scale.shape, F32)],
        compiler_params=_cparams(("parallel", "arbitrary")),
    )(diff, w, scale, dp)


def hyb_fwd(x, mix_gain, w_qkvz, w_ab, w_u, conv_w, a_log, dt_bias, out_gain, pool_w, pool_scale, w_out, tag):
    d = x.shape[1]
    ea, eb = _gate_consts()
    nh = GDN_HEADS
    (h,) = rowwise(_rms, [_full(x)], [mix_gain], [(d, d, _col0, BF)], name=f"{tag}_norm")
    p1 = matmul(h, w_qkvz, "nn", name=f"{tag}_in_qkvz")
    ab = matmul(h, w_ab, "nn", name=f"{tag}_in_ab")
    u = matmul(h, w_u, "nn", name=f"{tag}_in_u")
    cv = conv_fwd(p1, conv_w)
    (qk,) = rowwise(_l2_silu, [(cv, HEAD_W, _colj)], [], [(2 * GDN_W, HEAD_W, _colj, F32)], ncol=2 * nh,
                    name=f"{tag}_qk_act")
    (v,) = rowwise(_silu, [(cv, GDN_W, lambda j: 2)], [], [(GDN_W, GDN_W, _col0, F32)], name=f"{tag}_v_act")
    g, beta = rowwise(_gates, [_full(ab)], [a_log, dt_bias, ea, eb],
                      [(GDN_W, GDN_W, _col0, F32), (GDN_W, GDN_W, _col0, F32)], name=f"{tag}_gates")
    o, states = gdn_fwd(qk, v, g, beta)
    (on,) = rowwise(_gdn_post, [(o, HEAD_W, _colj), (p1, HEAD_W, lambda j: 3 * nh + j)], [out_gain],
                    [(GDN_W, HEAD_W, _colj, BF)], ncol=nh, name=f"{tag}_post")
    diff = pool_window_fwd(u)
    pm = pool_mix_fwd(diff, pool_w, pool_scale)
    cat = jnp.concatenate([on, pm], axis=1)
    y = matmul(cat, w_out, "nn", add=x, name=f"{tag}_out")
    return y, (x, h, p1, ab, cv, qk, v, g, beta, o, states, diff, cat)


def hyb_bwd(dy, saved, mix_gain, w_qkvz, w_ab, w_u, conv_w, a_log, dt_bias, out_gain, pool_w, pool_scale, w_out, tag,
            carry=()):
    x, h, p1, ab, cv, qk, v, g, beta, o, states, diff, cat = saved
    d = x.shape[1]
    nh = GDN_HEADS
    ea, eb = _gate_consts()
    dcat = matmul(dy, w_out, "nt", name=f"{tag}_out_dx")
    dw_out = matmul(cat, dy, "tn", name=f"{tag}_out_dw")
    (do, dz), (dout_gain,) = rowwise_vjp(
        _gdn_post, [(o, HEAD_W, _colj), (p1, HEAD_W, lambda j: 3 * nh + j)], [out_gain], [(dcat, HEAD_W, _colj)],
        [(GDN_W, HEAD_W, _colj, None), (GDN_W, HEAD_W, _colj, None)], ncol=nh, name=f"{tag}_post_bwd")
    ddiff, dpool_w, dpool_scale = pool_mix_bwd(diff, pool_w, pool_scale, dcat, GDN_W // POOL_GROUP_W)
    du = pool_window_bwd(ddiff)
    (dq, dk, dv, dg, dbeta), carried = gdn_bwd(qk, v, g, beta, states, do, f"{tag}_gdn_bwd", carry)
    (dab,), (da_log, ddt_bias) = rowwise_vjp(
        _gates, [_full(ab)], [a_log, dt_bias], [_full(dg), _full(dbeta)], [(HEAD_W, HEAD_W, _col0, None)],
        consts=[ea, eb], name=f"{tag}_gates_bwd")
    (dcq,), _ = rowwise_vjp(_l2_silu, [(cv, HEAD_W, _colj)], [], [(dq, HEAD_W, _colj)],
                            [(GDN_W, HEAD_W, _colj, None)], ncol=nh, par_grads=False, name=f"{tag}_q_act_bwd")
    (dck,), _ = rowwise_vjp(_l2_silu, [(cv, HEAD_W, lambda j: nh + j)], [], [(dk, HEAD_W, _colj)],
                            [(GDN_W, HEAD_W, _colj, None)], ncol=nh, par_grads=False, name=f"{tag}_k_act_bwd")
    (dcv,), _ = rowwise_vjp(_silu, [(cv, GDN_W, lambda j: 2)], [], [_full(dv)],
                            [(GDN_W, GDN_W, _col0, None)], par_grads=False, name=f"{tag}_v_act_bwd")
    dqkv, dconv_w = conv_bwd(p1, jnp.concatenate([dcq, dck, dcv], axis=1), conv_w)
    dp1 = jnp.concatenate([dqkv, dz], axis=1)
    dw_qkvz = matmul(h, dp1, "tn", name=f"{tag}_in_qkvz_dw")
    dw_ab = matmul(h, dab, "tn", name=f"{tag}_in_ab_dw")
    dw_u = matmul(h, du, "tn", name=f"{tag}_in_u_dw")
    dh = matmul(dp1, w_qkvz, "nt", name=f"{tag}_in_qkvz_dx")
    dh = matmul(dab, w_ab, "nt", add=dh, name=f"{tag}_in_ab_dx")
    dh = matmul(du, w_u, "nt", add=dh, name=f"{tag}_in_u_dx")
    (dx,), (dmix,) = rowwise_vjp(_rms, [_full(x)], [mix_gain], [_full(dh)], [(d, d, _col0, dy)],
                                 name=f"{tag}_norm_bwd")
    return dx, (dmix, dw_qkvz, dw_ab, dw_u, dconv_w, da_log, ddt_bias, dout_gain, dpool_w, dpool_scale, dw_out), carried


def loss_head(y, target):
    t, d = y.shape
    tb = _row_block(t)

    def body(y_ref, t_ref, dy_ref, loss_ref):
        @pl.when(pl.program_id(0) == 0)
        def _():
            loss_ref[...] = jnp.zeros_like(loss_ref)

        e = y_ref[...] - t_ref[...]
        dy_ref[...] = e * (1.0 / d)
        loss_ref[...] += 0.5 * jnp.sum(jnp.mean(e * e, axis=-1, keepdims=True))

    spec = pl.BlockSpec((tb, d), lambda i: (i, 0))
    return pl.pallas_call(
        body, name="loss_head", grid=(t // tb,), in_specs=[spec, spec],
        out_specs=[spec, pl.BlockSpec((8, 128), lambda i: (0, 0))],
        out_shape=[jax.ShapeDtypeStruct((t, d), F32), jax.ShapeDtypeStruct((8, 128), F32)],
        compiler_params=_cparams(("arbitrary",)),
    )(y, target)


ADAM_BLOCK_ELEMS = 256 * 1024


def _adam_rows(rows, cols):
    tb = 1024
    while tb >= 8:
        if rows % tb == 0 and tb * cols <= ADAM_BLOCK_ELEMS:
            return tb
        tb //= 2
    return rows


def adamw(w, g, m, v, name):
    rows, cols = w.shape
    tb = _adam_rows(rows, cols)
    c1 = 1.0 - ADAM_B1 ** ADAM_STEP
    c2 = 1.0 - ADAM_B2 ** ADAM_STEP

    def body(w_ref, g_ref, m_ref, v_ref, d_ref, nm_ref, nv_ref):
        gv = g_ref[...]
        nm = ADAM_B1 * m_ref[...] + (1.0 - ADAM_B1) * gv
        nv = ADAM_B2 * v_ref[...] + (1.0 - ADAM_B2) * (gv * gv)
        d_ref[...] = -ADAM_LR * ((nm / c1) / (jnp.sqrt(nv / c2) + ADAM_EPS) + ADAM_WD * w_ref[...])
        nm_ref[...] = nm
        nv_ref[...] = nv

    spec = pl.BlockSpec((tb, cols), lambda i: (i, 0))
    return pl.pallas_call(
        body, name=name, grid=(rows // tb,), in_specs=[spec] * 4, out_specs=[spec] * 3,
        out_shape=[jax.ShapeDtypeStruct((rows, cols), F32)] * 3,
        compiler_params=_cparams(("parallel",)),
    )(w, g, m, v)


LANES = 1024
HBM = pl.BlockSpec(memory_space=pltpu.HBM)


def _place():
    x, y, c = lax.axis_index("x"), lax.axis_index("y"), lax.axis_index("c")
    others = [(1 - x, y), (x, 1 - y), (1 - x, 1 - y)]
    return x, y, c, 2 * x + y, others


def _comm_call(body, name, out_shape, n_sems, *args):
    return pl.pallas_call(
        body, name=name, out_shape=out_shape, in_specs=[HBM] * len(args),
        out_specs=[HBM] * len(out_shape),
        scratch_shapes=[pltpu.SemaphoreType.DMA((n_sems,)), pltpu.SemaphoreType.DMA((n_sems,))],
        compiler_params=pltpu.CompilerParams(has_side_effects=True),
    )(*args)


AG_CHUNKS = 4


def all_gather_chips(xs):
    n = len(xs)
    for x in xs:
        assert x.shape[1] % (AG_CHUNKS * 16) == 0
    pieces = [(t, q) for t in range(n) for q in range(AG_CHUNKS)]
    base = 3 * len(pieces)

    def body(*refs):
        x_refs, out_refs, send_sems, recv_sems = refs[:n], refs[n:2 * n], refs[2 * n], refs[2 * n + 1]
        x, y, c, me, others = _place()
        sib = (x, y, 1 - c)

        def copy(k, src, dst, to):
            return pltpu.make_async_remote_copy(src_ref=src, dst_ref=dst, send_sem=send_sems.at[k],
                                                recv_sem=recv_sems.at[k], device_id=to, device_id_type=MESH)

        def rows(t, q):
            hc = x_refs[t].shape[1] // AG_CHUNKS
            return pl.ds(q * hc, hc)

        first, passed = [], []
        for p, (t, q) in enumerate(pieces):
            for k, chip in enumerate(others):
                cp = copy(p * 3 + k, x_refs[t].at[c, rows(t, q)], out_refs[t].at[me, c, rows(t, q)], (*chip, c))
                cp.start()
                first.append(cp)
        for p, (t, q) in enumerate(pieces):
            for k, (cx, cy) in enumerate(others):
                slot = out_refs[t].at[2 * cx + cy, c, rows(t, q)]
                copy(p * 3 + k, slot, slot, sib).wait_recv()
                fwd = copy(base + p * 3 + k, slot, slot, sib)
                fwd.start()
                passed.append(fwd)
        for p, (t, q) in enumerate(pieces):
            for k, (cx, cy) in enumerate(others):
                slot = out_refs[t].at[2 * cx + cy, 1 - c, rows(t, q)]
                copy(base + p * 3 + k, slot, slot, sib).wait_recv()
        for cp in first + passed:
            cp.wait_send()

    return _comm_call(body, "all_gather_chips", [jax.ShapeDtypeStruct((N_CHIPS,) + x.shape, x.dtype) for x in xs],
                      2 * base, *xs)


def pair_swap(gs, name):
    n = len(gs)

    def body(*refs):
        g_refs, r_refs, send_sems, recv_sems = refs[:n], refs[n:2 * n], refs[2 * n], refs[2 * n + 1]
        x, y, c, _, _ = _place()
        cps = [pltpu.make_async_remote_copy(src_ref=g.at[:, 1 - c], dst_ref=r, send_sem=send_sems.at[i],
                                            recv_sem=recv_sems.at[i], device_id=(x, y, 1 - c), device_id_type=MESH)
               for i, (g, r) in enumerate(zip(g_refs, r_refs))]
        for cp in cps:
            cp.start()
        for cp in cps:
            cp.wait()

    out = [jax.ShapeDtypeStruct((g.shape[0],) + g.shape[2:], g.dtype) for g in gs]
    return _comm_call(body, name, out, n, *gs)


def _chip_scatter_ops(s_refs, y_refs, send_sems, recv_sems):
    x, y, c, me, others = _place()

    def copies(to_peer):
        out = []
        for i, (s, dst) in enumerate(zip(s_refs, y_refs)):
            for k, (cx, cy) in enumerate(others):
                slot = dst.at[2 * cx + cy]
                out.append(pltpu.make_async_remote_copy(
                    src_ref=s.at[2 * cx + cy] if to_peer else slot, dst_ref=dst.at[me] if to_peer else slot,
                    send_sem=send_sems.at[3 * i + k], recv_sem=recv_sems.at[3 * i + k],
                    device_id=(cx, cy, c), device_id_type=MESH))
        return out

    def start():
        for cp in copies(True):
            cp.start()

    def wait():
        for cp in copies(False):
            cp.wait_recv()
        for cp in copies(True):
            cp.wait_send()

    return start, wait


def chip_scatter(ss, name):
    n = len(ss)

    def body(*refs):
        start, wait = _chip_scatter_ops(refs[:n], refs[n:2 * n], refs[2 * n], refs[2 * n + 1])
        start()
        wait()

    return _comm_call(body, name, [jax.ShapeDtypeStruct(s.shape, s.dtype) for s in ss], 3 * n, *ss)


def carried_call(body, carry, *, name, grid, in_specs, out_specs, out_shape, scratch_shapes, args):
    m = len(carry)
    n_in, n_out, n_scr = len(in_specs), len(out_specs), len(scratch_shapes)

    def wrapped(*refs):
        pos = [0]

        def take(k):
            part = refs[pos[0]:pos[0] + k]
            pos[0] += k
            return part

        ins, c_in, outs, c_out, scr, (send_sems, recv_sems) = take(n_in), take(m), take(n_out), take(m), take(n_scr), take(2)
        start, wait = _chip_scatter_ops(c_in, c_out, send_sems, recv_sems)
        ids = [pl.program_id(a) for a in range(len(grid))]
        first = functools.reduce(jnp.logical_and, [i == 0 for i in ids])
        last = functools.reduce(jnp.logical_and, [i == g - 1 for i, g in zip(ids, grid)])

        @pl.when(first)
        def _():
            start()

        body(*ins, *outs, *scr)

        @pl.when(last)
        def _():
            wait()

    return pl.pallas_call(
        wrapped, name=name, grid=grid, in_specs=list(in_specs) + [HBM] * m, out_specs=list(out_specs) + [HBM] * m,
        out_shape=list(out_shape) + [jax.ShapeDtypeStruct(s.shape, s.dtype) for s in carry],
        scratch_shapes=list(scratch_shapes) + [pltpu.SemaphoreType.DMA((3 * m,)), pltpu.SemaphoreType.DMA((3 * m,))],
        compiler_params=pltpu.CompilerParams(dimension_semantics=("arbitrary",) * len(grid),
                                             vmem_limit_bytes=VMEM_LIMIT, has_side_effects=True),
    )(*args, *carry)


def pair_join(fs, name):
    n = len(fs)

    def body(*refs):
        f_refs, o_refs, send_sems, recv_sems = refs[:n], refs[n:2 * n], refs[2 * n], refs[2 * n + 1]
        x, y, c, _, _ = _place()
        cps = [pltpu.make_async_remote_copy(src_ref=f, dst_ref=o, send_sem=send_sems.at[i], recv_sem=recv_sems.at[i],
                                            device_id=(x, y, 1 - c), device_id_type=MESH)
               for i, (f, o) in enumerate(zip(f_refs, o_refs))]
        for cp in cps:
            cp.start()
        for cp in cps:
            cp.wait()

    return _comm_call(body, name, [jax.ShapeDtypeStruct(f.shape, f.dtype) for f in fs], n, *fs)


SUM_ROWS = 256


def _tile_rows(h, cap):
    tb = cap
    while tb > 8 and h % tb:
        tb //= 2
    assert h % tb == 0, (h, tb)
    return tb


def sum_blocks(a, name):
    n, h, lanes = a.shape
    tb = _tile_rows(h, SUM_ROWS)

    def body(a_ref, o_ref):
        acc = a_ref[0]
        for k in range(1, n):
            acc = acc + a_ref[k]
        o_ref[...] = acc

    return pl.pallas_call(
        body, name=name, grid=(h // tb,), in_specs=[pl.BlockSpec((n, tb, lanes), lambda i: (0, i, 0))],
        out_specs=pl.BlockSpec((tb, lanes), lambda i: (i, 0)), out_shape=jax.ShapeDtypeStruct((h, lanes), a.dtype),
        compiler_params=_cparams(("parallel",)),
    )(a)


def add2(a, b, name):
    r, lanes = a.shape
    tb = _tile_rows(r, SUM_ROWS)

    def body(a_ref, b_ref, o_ref):
        o_ref[...] = a_ref[...] + b_ref[...]

    spec = pl.BlockSpec((tb, lanes), lambda i: (i, 0))
    return pl.pallas_call(
        body, name=name, grid=(r // tb,), in_specs=[spec, spec], out_specs=spec,
        out_shape=jax.ShapeDtypeStruct((r, lanes), a.dtype), compiler_params=_cparams(("parallel",)),
    )(a, b)


def pair_reduce(gs, tag):
    c = lax.axis_index("c")
    got = pair_swap(gs, f"pair_swap_{tag}")
    sums = []
    for i, (g, r) in enumerate(zip(gs, got)):
        n, _, h, lanes = g.shape
        mine = lax.dynamic_index_in_dim(g, c, axis=1, keepdims=False)
        sums.append(add2(mine.reshape(n * h, lanes), r.reshape(n * h, lanes), f"pair_sum_{tag}_{i}").reshape(n, h, lanes))
    return sums


def chip_reduce(sums, ys, tag):
    me = 2 * lax.axis_index("x") + lax.axis_index("y")
    fs = []
    for i, (s, y) in enumerate(zip(sums, ys)):
        own = lax.dynamic_index_in_dim(s, me, axis=0, keepdims=True)
        fs.append(sum_blocks(lax.dynamic_update_slice_in_dim(y, own, me, axis=0), f"chip_sum_{tag}_{i}"))
    return fs


def pair_finish(fs, tag):
    c = lax.axis_index("c")
    others = pair_join(fs, f"pair_join_{tag}")
    return [jnp.concatenate([jnp.where(c == 0, f, o), jnp.where(c == 0, o, f)]) for f, o in zip(fs, others)]


SHARDED = {
    "ffn1_w_gate": 2, "ffn1_w_up": 2, "ffn1_w_down": 1, "ffn2_w_gate": 2, "ffn2_w_up": 2, "ffn2_w_down": 1,
    "hyb_w_in": 2, "gdn_conv": 2, "pool_w": 2, "hyb_w_out": 1, "mla_w_in": 1, "mla_q_norm": 1, "mla_kv_norm": 1,
    "mla_w_q_up": 2, "mla_w_kv_up": 2, "mla_w_out": 1,
}
EXACT = ("gdn_conv", "mla_q_norm", "mla_kv_norm")
EVEN_ONLY = ("hyb_w_in", "gdn_conv", "gdn_a_log", "gdn_dt_bias", "gdn_out_norm", "pool_w", "pool_scale", "hyb_w_out")
WEIGHTS = ["ffn1_norm", "ffn1_w_gate", "ffn1_w_up", "ffn1_w_down", "mix_norm", "ffn2_norm", "ffn2_w_gate",
           "ffn2_w_up", "ffn2_w_down", "hyb_w_in", "gdn_conv", "gdn_a_log", "gdn_dt_bias", "gdn_out_norm", "pool_w",
           "pool_scale", "hyb_w_out", "mla_w_in", "mla_q_norm", "mla_kv_norm", "mla_w_q_up", "mla_w_kv_up",
           "mla_q_head_norm", "mla_k_head_norm", "mla_w_out"]


def _pad_rows(flat, mult):
    n = flat.shape[0]
    rows = -(-n // LANES)
    rows = -(-rows // mult) * mult
    return jnp.pad(flat, (0, rows * LANES - n)), rows


NATURAL = "hyb_w_in"


def regroup_w_in(shards, layer):
    _, _, rows, width = shards.shape
    tb = _row_block(rows, EVEN_IN)
    cut = 4 * GDN_W
    nab = 2 * GDN_HEADS

    def body(s_ref, qkvz_ref, ab_ref, u_ref, full):
        for k in range(N_CHIPS):
            full[:, k * width:(k + 1) * width] = s_ref[k]
        qkvz_ref[...] = full[:, :cut]
        ab_ref[...] = jnp.zeros_like(ab_ref)
        ab_ref[:, :nab] = full[:, cut:cut + nab]
        u_ref[...] = full[:, cut + nab:]

    return pl.pallas_call(
        body, name=f"regroup_w_in_{layer}", grid=(rows // tb,),
        in_specs=[pl.BlockSpec((N_CHIPS, None, tb, width), lambda i: (0, layer, i, 0))],
        out_specs=[pl.BlockSpec((tb, cut), lambda i: (i, 0)), pl.BlockSpec((tb, HEAD_W), lambda i: (i, 0)),
                   pl.BlockSpec((tb, POOL_W), lambda i: (i, 0))],
        out_shape=[jax.ShapeDtypeStruct((rows, cut), shards.dtype), jax.ShapeDtypeStruct((rows, HEAD_W), shards.dtype),
                   jax.ShapeDtypeStruct((rows, POOL_W), shards.dtype)],
        scratch_shapes=[pltpu.VMEM((tb, N_CHIPS * width), shards.dtype)],
        compiler_params=_cparams(("parallel",)),
    )(shards)


def regroup_dw_in(dw_qkvz, dw_ab, dw_u):
    rows = dw_qkvz.shape[0]
    width = EVEN_IN // N_CHIPS
    tb = _row_block(rows, EVEN_IN)
    cut = 4 * GDN_W
    nab = 2 * GDN_HEADS

    def body(qkvz_ref, ab_ref, u_ref, o_ref, full):
        full[:, :cut] = qkvz_ref[...]
        full[:, cut:cut + nab] = ab_ref[:, :nab]
        full[:, cut + nab:] = u_ref[...]
        for k in range(N_CHIPS):
            o_ref[k] = full[:, k * width:(k + 1) * width]

    return pl.pallas_call(
        body, name="regroup_dw_in", grid=(rows // tb,),
        in_specs=[pl.BlockSpec((tb, cut), lambda i: (i, 0)), pl.BlockSpec((tb, HEAD_W), lambda i: (i, 0)),
                  pl.BlockSpec((tb, POOL_W), lambda i: (i, 0))],
        out_specs=pl.BlockSpec((N_CHIPS, tb, width), lambda i: (0, i, 0)),
        out_shape=jax.ShapeDtypeStruct((N_CHIPS, rows, width), F32),
        scratch_shapes=[pltpu.VMEM((tb, EVEN_IN), F32)],
        compiler_params=_cparams(("parallel",)),
    )(dw_qkvz, dw_ab, dw_u)


def gather_weights(w):
    parts = []
    for name in SHARDED:
        if name == NATURAL:
            continue
        a = w[name]
        parts.append(lax.bitcast_convert_type(a, BF).reshape(-1) if name in EXACT else a.astype(BF).reshape(-1))
    flat, rows = _pad_rows(jnp.concatenate(parts), 2 * 16 * AG_CHUNKS)
    mine = [flat.reshape(2, rows // 2, LANES), w[NATURAL].astype(BF)]
    me = 2 * lax.axis_index("x") + lax.axis_index("y")
    got = [lax.dynamic_update_slice_in_dim(g, m[None], me, axis=0) for g, m in zip(all_gather_chips(mine), mine)]
    full = {NATURAL: got[1]}
    got = got[0].reshape(N_CHIPS, rows * LANES)
    off = 0
    for name, axis in SHARDED.items():
        if name == NATURAL:
            continue
        a = w[name]
        n = a.size * (2 if name in EXACT else 1)
        seg = got[:, off:off + n]
        off += n
        if name in EXACT:
            seg = lax.bitcast_convert_type(seg.reshape((N_CHIPS,) + a.shape + (2,)), F32)
        else:
            seg = seg.reshape((N_CHIPS,) + a.shape)
        seg = jnp.moveaxis(seg, 0, axis)
        full[name] = seg.reshape(a.shape[:axis] + (N_CHIPS * a.shape[axis],) + a.shape[axis + 1:])
    return full


OWN_OPERAND = 256 * 1024
EXCHANGE_UNIT = 2 * 8 * LANES


def shard_major(g, axis):
    size = g.shape[axis] // N_CHIPS
    return jnp.moveaxis(g.reshape(g.shape[:axis] + (N_CHIPS, size) + g.shape[axis + 1:]), axis, 0)


def _exchange_operand(flat):
    n = flat.shape[1]
    padded = -(-n // EXCHANGE_UNIT) * EXCHANGE_UNIT
    return jnp.pad(flat, ((0, 0), (0, padded - n))).reshape(N_CHIPS, 2, padded // (2 * LANES), LANES)


def is_own_operand(name, g):
    return name in SHARDED and g.size // N_CHIPS >= OWN_OPERAND


def grad_operand(name, g):
    if name == NATURAL:
        n4, rows, width = g.shape
        return g.reshape(n4, 2, rows // 2, width), rows * width, g.shape[1:]
    flat = g.reshape(N_CHIPS, -1)
    return _exchange_operand(flat), flat.shape[1], g.shape[1:]


def from_operand(name, r, n, shape):
    return r if name == NATURAL else r.reshape(-1)[:n].reshape(shape)


def misc_operand(small, loss_tile):
    flats, layout = [], []
    for name, layer, g in small + [("loss", 0, loss_tile)]:
        if name in SHARDED:
            flat, shape = g.reshape(N_CHIPS, -1), g.shape[1:]
        else:
            flat, shape = jnp.broadcast_to(g.reshape(1, -1), (N_CHIPS, g.size)), g.shape
        flats.append(flat)
        layout.append((name, layer, flat.shape[1], shape))
    return _exchange_operand(jnp.concatenate(flats, axis=1)), layout


def _as2d(a):
    return a.reshape(-1, a.shape[-1])


def kernel(x, positions, ffn1_norm, ffn1_w_gate, ffn1_w_up, ffn1_w_down, mix_norm, ffn2_norm, ffn2_w_gate, ffn2_w_up, ffn2_w_down, hyb_w_in, gdn_conv, gdn_a_log, gdn_dt_bias, gdn_out_norm, pool_w, pool_scale, hyb_w_out, mla_w_in, mla_q_norm, mla_kv_norm, mla_w_q_up, mla_w_kv_up, mla_q_head_norm, mla_k_head_norm, mla_w_out, loss_target, m_ffn1_norm, m_ffn1_w_gate, m_ffn1_w_up, m_ffn1_w_down, m_mix_norm, m_ffn2_norm, m_ffn2_w_gate, m_ffn2_w_up, m_ffn2_w_down, m_hyb_w_in, m_gdn_conv, m_gdn_a_log, m_gdn_dt_bias, m_gdn_out_norm, m_pool_w, m_pool_scale, m_hyb_w_out, m_mla_w_in, m_mla_q_norm, m_mla_kv_norm, m_mla_w_q_up, m_mla_w_kv_up, m_mla_q_head_norm, m_mla_k_head_norm, m_mla_w_out, v_ffn1_norm, v_ffn1_w_gate, v_ffn1_w_up, v_ffn1_w_down, v_mix_norm, v_ffn2_norm, v_ffn2_w_gate, v_ffn2_w_up, v_ffn2_w_down, v_hyb_w_in, v_gdn_conv, v_gdn_a_log, v_gdn_dt_bias, v_gdn_out_norm, v_pool_w, v_pool_scale, v_hyb_w_out, v_mla_w_in, v_mla_q_norm, v_mla_kv_norm, v_mla_w_q_up, v_mla_w_kv_up, v_mla_q_head_norm, v_mla_k_head_norm, v_mla_w_out):
    given = dict(locals())
    w = {n: given[n] for n in WEIGHTS}
    moments_m = {n: given["m_" + n] for n in WEIGHTS}
    moments_v = {n: given["v_" + n] for n in WEIGHTS}
    t = x.shape[1]
    xs = x.reshape(t, D_MODEL)
    full = gather_weights(w)
    n_even = hyb_w_in.shape[0]
    n_odd = mla_w_in.shape[0]

    _, sign, inv_freq = _rope_consts()
    cos, sin = rope_tables(positions.reshape(t, 1), inv_freq, sign)

    def ffn_args(which, layer):
        return (w[f"{which}_norm"][layer][None], full[f"{which}_w_gate"][layer], full[f"{which}_w_up"][layer],
                full[f"{which}_w_down"][layer])

    w_in_groups = [regroup_w_in(full[NATURAL], i) for i in range(n_even)]

    def hyb_args(i):
        w_qkvz, w_ab, w_u = w_in_groups[i]
        return (w["mix_norm"][2 * i][None], w_qkvz, w_ab, w_u,
                jnp.pad(full["gdn_conv"][i], ((0, HALO - CONV_K), (0, 0))), jnp.repeat(w["gdn_a_log"][i], HEAD_W)[None],
                jnp.repeat(w["gdn_dt_bias"][i], HEAD_W)[None], w["gdn_out_norm"][i][None], full["pool_w"][i],
                w["pool_scale"][i][None], full["hyb_w_out"][i])

    def mla_args(i):
        w_in = jnp.pad(full["mla_w_in"][i], ((0, 0), (0, ODD_IN_PAD - ODD_IN)))
        w_q = jnp.pad(full["mla_w_q_up"][i].reshape(LORA, MLA_HEADS, QK_HEAD),
                      ((0, 0), (0, 0), (0, 2 * HEAD_W - QK_HEAD))).reshape(LORA, MLA_HEADS * 2 * HEAD_W)
        return (cos, sin, w["mix_norm"][2 * i + 1][None], w_in, full["mla_q_norm"][i][None], full["mla_kv_norm"][i][None],
                w_q, full["mla_w_kv_up"][i], w["mla_q_head_norm"][i], w["mla_k_head_norm"][i], full["mla_w_out"][i])

    saved = []
    h = xs
    for layer in range(DEPTH):
        i = layer // 2
        h, s1 = ffn_fwd(h, *ffn_args("ffn1", layer), f"l{layer}_ffn1")
        if layer % 2 == 0:
            h, s2 = hyb_fwd(h, *hyb_args(i), f"l{layer}_hyb")
        else:
            h, s2 = mla_fwd(h, *mla_args(i), f"l{layer}_mla")
        h, s3 = ffn_fwd(h, *ffn_args("ffn2", layer), f"l{layer}_ffn2")
        saved.append((s1, s2, s3))

    dh, loss_tile = loss_head(h, loss_target.reshape(t, D_MODEL))

    small, done, halves = [], [], []

    def begin(group, tag):
        recs, ops = [], []
        for name, idx, g in group:
            if name in SHARDED and not (name.endswith(("w_gate", "w_up")) or name == NATURAL):
                g = shard_major(g, SHARDED[name] - 1)
            if is_own_operand(name, g):
                op, n, shape = grad_operand(name, g)
                recs.append((name, idx, n, shape))
                ops.append(op)
            else:
                small.append((name, idx, g))
        return recs, pair_reduce(ops, tag)

    def finish(recs, sums, ys, tag):
        done.extend(recs)
        halves.extend(chip_reduce(sums, ys, tag))

    waiting = ([], [])
    for layer in reversed(range(DEPTH)):
        i = layer // 2
        s1, s2, s3 = saved[layer]
        dh, dg, dwg, dwu, dwd = ffn_bwd(dh, s3, *ffn_args("ffn2", layer), f"l{layer}_ffn2")
        recs, sums = begin([("ffn2_norm", layer, dg[0]), ("ffn2_w_gate", layer, dwg), ("ffn2_w_up", layer, dwu),
                            ("ffn2_w_down", layer, dwd)], f"l{layer}a")
        recs, sums = waiting[0] + recs, waiting[1] + sums
        if layer % 2 == 0:
            dh, g, ys = hyb_bwd(dh, s2, *hyb_args(i), f"l{layer}_hyb", carry=sums)
            dmix, dw_qkvz, dw_ab, dw_u, dconv, da_log, ddt, dog, dpw, dps, dwo = g
            group = [("hyb_w_in", i, regroup_dw_in(dw_qkvz, dw_ab, dw_u)), ("gdn_conv", i, dconv[:CONV_K]),
                     ("gdn_a_log", i, da_log.reshape(GDN_HEADS, HEAD_W).sum(axis=1)),
                     ("gdn_dt_bias", i, ddt.reshape(GDN_HEADS, HEAD_W).sum(axis=1)), ("gdn_out_norm", i, dog[0]),
                     ("pool_w", i, dpw), ("pool_scale", i, dps[0]), ("hyb_w_out", i, dwo)]
        else:
            dh, g, ys = mla_bwd(dh, s2, *mla_args(i), f"l{layer}_mla", carry=sums)
            dmix, dw_in, dqg, dkvg, dwq, dwkv, dqh, dkh, dwo = g
            dwq = dwq.reshape(LORA, MLA_HEADS, 2 * HEAD_W)[:, :, :QK_HEAD].reshape(LORA, -1)
            group = [("mla_w_in", i, dw_in[:, :ODD_IN]), ("mla_q_norm", i, dqg[0]), ("mla_kv_norm", i, dkvg[0]),
                     ("mla_w_q_up", i, dwq), ("mla_w_kv_up", i, dwkv), ("mla_q_head_norm", i, dqh),
                     ("mla_k_head_norm", i, dkh), ("mla_w_out", i, dwo)]
        finish(recs, sums, ys, f"l{layer}")
        dh, dg, dwg, dwu, dwd = ffn_bwd(dh, s1, *ffn_args("ffn1", layer), f"l{layer}_ffn1")
        group += [("mix_norm", layer, dmix[0]), ("ffn1_norm", layer, dg[0]), ("ffn1_w_gate", layer, dwg),
                  ("ffn1_w_up", layer, dwu), ("ffn1_w_down", layer, dwd)]
        waiting = begin(group, f"l{layer}b")

    misc, layout = misc_operand(small, loss_tile)
    recs, sums = waiting[0] + [("misc", 0, 0, None)], waiting[1] + pair_reduce([misc], "misc")
    finish(recs, sums, chip_scatter(sums, "chip_scatter_tail"), "tail")
    wholes = pair_finish(halves, "grads")

    n_layers = {n: DEPTH if n.startswith(("ffn", "mix")) else (n_even if n in EVEN_ONLY else n_odd) for n in WEIGHTS}
    per_layer = {n: [None] * n_layers[n] for n in WEIGHTS}
    loss = None
    for (name, idx, n, shape), r in zip(done, wholes):
        if name != "misc":
            per_layer[name][idx] = from_operand(name, r, n, shape)
            continue
        flat, off = r.reshape(-1), 0
        for small_name, small_idx, size, small_shape in layout:
            piece = flat[off:off + size].reshape(small_shape)
            off += size
            if small_name == "loss":
                loss = piece[0, 0]
            else:
                per_layer[small_name][small_idx] = piece
    grads = {n: jnp.stack(per_layer[n]) for n in WEIGHTS}

    deltas, new_m, new_v = {}, {}, {}
    for n in WEIGHTS:
        d2, m2, v2 = adamw(_as2d(w[n]), _as2d(grads[n]), _as2d(moments_m[n]), _as2d(moments_v[n]), f"adamw_{n}")
        deltas[n], new_m[n], new_v[n] = d2.reshape(w[n].shape), m2.reshape(w[n].shape), v2.reshape(w[n].shape)
    return (loss, dh.reshape(x.shape), *[grads[n] for n in WEIGHTS], *[deltas[n] for n in WEIGHTS],
            *[new_m[n] for n in WEIGHTS], *[new_v[n] for n in WEIGHTS])
```

```python
import functools
import math

import jax
import jax.numpy as jnp
import numpy as np
from jax import lax
from jax.experimental import pallas as pl
from jax.experimental.pallas import tpu as pltpu

F32 = jnp.float32
BF = jnp.bfloat16
HI = lax.Precision.HIGHEST
MESH = pl.DeviceIdType.MESH

D_MODEL = 2048
D_FF = 4096
DEPTH = 4
GDN_HEADS = 8
HEAD_W = 128
GDN_W = GDN_HEADS * HEAD_W
CONV_K = 4
CHUNK = 64
POOL_WINDOWS = (2, 4, 8, 16)
POOL_W = 1024
POOL_GROUP_W = 256
EVEN_IN = 5136
MLA_HEADS = 16
LORA = 512
ROPE = 64
QK_HEAD = HEAD_W + ROPE
ODD_IN = 2 * LORA + ROPE
ODD_IN_PAD = 2 * LORA + HEAD_W
ROPE_THETA = 10000.0
EPS = 1e-6
N_CHIPS = 4

ADAM_LR = 0.001
ADAM_B1 = 0.9
ADAM_B2 = 0.999
ADAM_EPS = 1e-08
ADAM_WD = 0.01
ADAM_STEP = 10

ROW_BLOCK = 256
ROW_BLOCK_ELEMS = 256 * 1024
COL_BLOCK = 1024
MM_TILE = 1024
MM_TILE_K = 2048
VMEM_LIMIT = 56 * 1024 * 1024


def _cparams(sem=None):
    return pltpu.CompilerParams(dimension_semantics=sem, vmem_limit_bytes=VMEM_LIMIT)


def _bdot(a, b, ca, cb):
    return lax.dot_general(a.astype(BF), b.astype(BF), (((ca,), (cb,)), ((), ())),
                           preferred_element_type=F32)


@jax.custom_vjp
def mm_nn(a, b):
    return _bdot(a, b, 1, 0)


@jax.custom_vjp
def mm_nt(a, b):
    return _bdot(a, b, 1, 1)


@jax.custom_vjp
def mm_tn(a, b):
    return _bdot(a, b, 0, 0)


mm_nn.defvjp(lambda a, b: (mm_nn(a, b), (a, b)), lambda r, g: (mm_nt(g, r[1]), mm_tn(r[0], g)))
mm_nt.defvjp(lambda a, b: (mm_nt(a, b), (a, b)), lambda r, g: (mm_nn(g, r[1]), mm_tn(g, r[0])))
mm_tn.defvjp(lambda a, b: (mm_tn(a, b), (a, b)), lambda r, g: (mm_nt(r[1], g), mm_nn(r[0], g)))


def hdot(a, b):
    return lax.dot_general(a, b, (((1,), (0,)), ((), ())), precision=HI, preferred_element_type=F32)


def hdot_nt(a, b):
    return lax.dot_general(a, b, (((1,), (1,)), ((), ())), precision=HI, preferred_element_type=F32)


def _sigmoid(x):
    return 1.0 / (1.0 + jnp.exp(-x))


def _silu(x):
    return x * _sigmoid(x)


def _softplus(x):
    return jnp.maximum(x, 0.0) + jnp.log(1.0 + jnp.exp(-jnp.abs(x)))


def _tile(dim, cap):
    if dim <= cap:
        return dim
    t = (cap // 128) * 128
    while t >= 128:
        if dim % t == 0:
            return t
        t -= 128
    raise ValueError(f"no tile for {dim}")


def _as_tuple(r):
    return tuple(r) if isinstance(r, (tuple, list)) else (r,)


def _row_block(t, cols=None):
    rows = ROW_BLOCK if cols is None else max(ROW_BLOCK, ROW_BLOCK_ELEMS // cols)
    rows = min(rows, t)
    assert t % rows == 0, (t, rows)
    return rows


def matmul(a, b, mode, *, name, alpha=1.0, add=None, out_dtype=F32, out_shards=1):
    if mode == "nn":
        (m, k), (k2, n) = a.shape, b.shape
    elif mode == "nt":
        (m, k), (n, k2) = a.shape, b.shape
    else:
        (k, m), (k2, n) = a.shape, b.shape
    assert k == k2, (a.shape, b.shape, mode)
    tm, tn, tk = _tile(m, MM_TILE), _tile(n, MM_TILE), _tile(k, MM_TILE_K)
    nk = k // tk
    ca = 0 if mode == "tn" else 1
    cb = 1 if mode == "nt" else 0
    a_spec = (pl.BlockSpec((tk, tm), lambda i, j, kk: (kk, i)) if mode == "tn"
              else pl.BlockSpec((tm, tk), lambda i, j, kk: (i, kk)))
    b_spec = (pl.BlockSpec((tn, tk), lambda i, j, kk: (j, kk)) if mode == "nt"
              else pl.BlockSpec((tk, tn), lambda i, j, kk: (kk, j)))
    o_spec = pl.BlockSpec((tm, tn), lambda i, j, kk: (i, j))
    has_add = add is not None

    def body(*refs):
        if has_add:
            a_ref, b_ref, add_ref, o_ref, acc_ref = refs
        else:
            a_ref, b_ref, o_ref, acc_ref = refs
        kk = pl.program_id(2)

        @pl.when(kk == 0)
        def _():
            acc_ref[...] = jnp.zeros_like(acc_ref)

        acc_ref[...] += _bdot(a_ref[...], b_ref[...], ca, cb)

        @pl.when(kk == nk - 1)
        def _():
            r = acc_ref[...]
            if alpha != 1.0:
                r = r * alpha
            if has_add:
                r = r + add_ref[...].astype(F32)
            o_ref[...] = r.astype(out_dtype)

    in_specs = [a_spec, b_spec] + ([o_spec] if has_add else [])
    args = (a, b) + ((add,) if has_add else ())
    out_spec, out_shape = o_spec, (m, n)
    if out_shards > 1:
        assert not has_add and (n // out_shards) % tn == 0
        per = n // out_shards // tn
        out_spec = pl.BlockSpec((None, tm, tn), lambda i, j, kk: (j // per, i, j % per))
        out_shape = (out_shards, m, n // out_shards)
    return pl.pallas_call(
        body, name=name, grid=(m // tm, n // tn, nk), in_specs=in_specs, out_specs=out_spec,
        out_shape=jax.ShapeDtypeStruct(out_shape, out_dtype),
        scratch_shapes=[pltpu.VMEM((tm, tn), F32)],
        compiler_params=_cparams(("parallel", "parallel", "arbitrary")),
    )(*args)


def _row_spec(tb, bc, cf):
    return pl.BlockSpec((tb, bc), lambda i, j, cf=cf: (i, cf(j)))


def _par_spec(p):
    return pl.BlockSpec(p.shape, lambda i, j: (0, 0))


def rowwise(f, rows, pars, outs, *, ncol=1, name):
    t = rows[0][0].shape[0]
    tb = _row_block(t, max([bc for _, bc, _ in rows] + [bc for _, bc, _, _ in outs]))
    nr = len(rows)

    def body(*refs):
        vals = [r[...].astype(F32) for r in refs[:nr + len(pars)]]
        res = _as_tuple(f(*vals))
        for o_ref, r in zip(refs[nr + len(pars):], res):
            o_ref[...] = r.astype(o_ref.dtype)

    return pl.pallas_call(
        body, name=name, grid=(t // tb, ncol),
        in_specs=[_row_spec(tb, bc, cf) for _, bc, cf in rows] + [_par_spec(p) for p in pars],
        out_specs=[_row_spec(tb, bc, cf) for _, bc, cf, _ in outs],
        out_shape=[jax.ShapeDtypeStruct((t, tc), dt) for tc, _, _, dt in outs],
        compiler_params=_cparams(("parallel", "arbitrary")),
    )(*[r[0] for r in rows], *pars)


def rowwise_vjp(f, rows, pars, cts, row_grads, *, ncol=1, name, par_grads=True, consts=(), grad_dtype=F32):
    t = rows[0][0].shape[0]
    tb = _row_block(t, max([bc for _, bc, _ in rows + cts] + [g[1] for g in row_grads if g is not None]))
    consts = list(consts)
    nr, npar, nct, ncon = len(rows), len(pars), len(cts), len(consts)
    diff_rows = [i for i, g in enumerate(row_grads) if g is not None]
    adds = [row_grads[i][3] for i in diff_rows]
    add_idx = [i for i, a in enumerate(adds) if a is not None]

    def body(*refs):
        pos = 0
        row_refs = refs[pos:pos + nr]; pos += nr
        par_refs = refs[pos:pos + npar]; pos += npar
        con_refs = refs[pos:pos + ncon]; pos += ncon
        ct_refs = refs[pos:pos + nct]; pos += nct
        add_refs = refs[pos:pos + len(add_idx)]; pos += len(add_idx)
        grow_refs = refs[pos:pos + len(diff_rows)]; pos += len(diff_rows)
        gpar_refs = refs[pos:]
        row_vals = [r[...].astype(F32) for r in row_refs]
        par_vals = [r[...].astype(F32) for r in par_refs]
        con_vals = [r[...].astype(F32) for r in con_refs]

        def g(*dvals):
            rv = list(row_vals)
            for i, v in zip(diff_rows, dvals[:len(diff_rows)]):
                rv[i] = v
            pv = dvals[len(diff_rows):] if par_grads else par_vals
            return _as_tuple(f(*rv, *pv, *con_vals))

        prim = [row_vals[i] for i in diff_rows] + (par_vals if par_grads else [])
        _, pull = jax.vjp(g, *prim)
        grads = pull(tuple(c[...].astype(F32) for c in ct_refs))
        for n, ref in enumerate(grow_refs):
            gr = grads[n]
            if n in add_idx:
                gr = gr + add_refs[add_idx.index(n)][...]
            ref[...] = gr.astype(ref.dtype)
        if par_grads:
            first = jnp.logical_and(pl.program_id(0) == 0, pl.program_id(1) == 0)
            for ref, gr in zip(gpar_refs, grads[len(diff_rows):]):
                @pl.when(first)
                def _(ref=ref):
                    ref[...] = jnp.zeros_like(ref)
                ref[...] += gr

    gspecs = [row_grads[i] for i in diff_rows]
    in_specs = ([_row_spec(tb, bc, cf) for _, bc, cf in rows] + [_par_spec(p) for p in pars + consts]
                + [_row_spec(tb, bc, cf) for _, bc, cf in cts]
                + [_row_spec(tb, gspecs[i][1], gspecs[i][2]) for i in add_idx])
    out_specs = [_row_spec(tb, bc, cf) for _, bc, cf, _ in gspecs]
    out_shape = [jax.ShapeDtypeStruct((t, tc), grad_dtype) for tc, _, _, _ in gspecs]
    if par_grads:
        out_specs += [_par_spec(p) for p in pars]
        out_shape += [jax.ShapeDtypeStruct(p.shape, F32) for p in pars]
    res = pl.pallas_call(
        body, name=name, grid=(t // tb, ncol), in_specs=in_specs, out_specs=out_specs,
        out_shape=out_shape, compiler_params=_cparams(("arbitrary", "arbitrary")),
    )(*[r[0] for r in rows], *pars, *consts, *[c[0] for c in cts], *[adds[i] for i in add_idx])
    return list(res[:len(diff_rows)]), list(res[len(diff_rows):])


def _col0(j):
    return 0


def _colj(j):
    return j


def _full(a):
    return (a, a.shape[1], _col0)


def _rms(x, gain):
    return x * lax.rsqrt(jnp.mean(x * x, axis=-1, keepdims=True) + EPS) * gain


def _swiglu_act(g, u):
    return _silu(g) * u


def ffn_fwd(x, gain, wg, wu, wd, tag):
    d = x.shape[1]
    (h,) = rowwise(_rms, [_full(x)], [gain], [(d, d, _col0, BF)], name=f"{tag}_norm")
    g = matmul(h, wg, "nn", name=f"{tag}_gate")
    u = matmul(h, wu, "nn", name=f"{tag}_up")
    f = g.shape[1]
    cb = _tile(f, COL_BLOCK)
    (a,) = rowwise(_swiglu_act, [(g, cb, _colj), (u, cb, _colj)], [], [(f, cb, _colj, BF)], ncol=f // cb,
                   name=f"{tag}_act")
    y = matmul(a, wd, "nn", alpha=0.5, add=x, name=f"{tag}_down")
    return y, (x, h, g, u, a)


def ffn_bwd(dy, saved, gain, wg, wu, wd, tag):
    x, h, g, u, a = saved
    d, f = x.shape[1], g.shape[1]
    da = matmul(dy, wd, "nt", alpha=0.5, name=f"{tag}_down_dx")
    dwd = matmul(a, dy, "tn", alpha=0.5, name=f"{tag}_down_dw")
    cb = _tile(f, COL_BLOCK)
    (dg, du), _ = rowwise_vjp(_swiglu_act, [(g, cb, _colj), (u, cb, _colj)], [], [(da, cb, _colj)],
                              [(f, cb, _colj, None), (f, cb, _colj, None)], ncol=f // cb,
                              name=f"{tag}_act_bwd", par_grads=False, grad_dtype=BF)
    dwg = matmul(h, dg, "tn", name=f"{tag}_gate_dw", out_shards=N_CHIPS)
    dwu = matmul(h, du, "tn", name=f"{tag}_up_dw", out_shards=N_CHIPS)
    dh = matmul(dg, wg, "nt", name=f"{tag}_gate_dx")
    dh = matmul(du, wu, "nt", add=dh, name=f"{tag}_up_dx")
    (dx,), (dgain,) = rowwise_vjp(_rms, [_full(x)], [gain], [_full(dh)], [(d, d, _col0, dy)],
                                  name=f"{tag}_norm_bwd")
    return dx, dgain, dwg, dwu, dwd


def rope_tables(positions, inv_freq, sign):
    t = positions.shape[0]
    tb = _row_block(t)

    def body(pos_ref, f_ref, s_ref, c_ref, sn_ref):
        ang = pos_ref[...].astype(F32) * f_ref[...]
        live = jnp.abs(s_ref[...])
        c_ref[...] = jnp.cos(ang) * live
        sn_ref[...] = jnp.sin(ang) * s_ref[...]

    return pl.pallas_call(
        body, name="rope_tables", grid=(t // tb,),
        in_specs=[pl.BlockSpec((tb, 1), lambda i: (i, 0)), pl.BlockSpec((1, HEAD_W), lambda i: (0, 0)),
                  pl.BlockSpec((1, HEAD_W), lambda i: (0, 0))],
        out_specs=[pl.BlockSpec((tb, HEAD_W), lambda i: (i, 0))] * 2,
        out_shape=[jax.ShapeDtypeStruct((t, HEAD_W), F32)] * 2,
        compiler_params=_cparams(("parallel",)),
    )(positions, inv_freq, sign)


def _rope(p, c, s, swap):
    return p * c + hdot(p, swap) * s


def _pe_norm(pe, gp):
    return pe * lax.rsqrt(jnp.sum(pe * pe, axis=-1, keepdims=True) * (1.0 / ROPE) + EPS) * gp


def _q_head(nope, pe, c, s, gn, gp, swap):
    return _rms(nope, gn), _rope(_pe_norm(pe, gp), c, s, swap)


def _k_head(nope, v, gn):
    return _rms(nope, gn), v


def _kpe_head(pe, c, s, gp, swap):
    return _rope(_pe_norm(pe, gp), c, s, swap)


ATT_BLOCK = 1024
ATT_SCALE = QK_HEAD ** -0.5
NEG = float(np.finfo(np.float32).min)


def _att_block(t):
    return min(ATT_BLOCK, t)


def _scores(qn, qp, kn, kp, diag):
    q = jnp.concatenate([qn, qp], axis=1)
    k = jnp.concatenate([kn, kp], axis=1)
    s = lax.dot_general(q, k, (((1,), (1,)), ((), ())), preferred_element_type=F32) * ATT_SCALE
    if diag:
        rows = lax.broadcasted_iota(jnp.int32, s.shape, 0)
        cols = lax.broadcasted_iota(jnp.int32, s.shape, 1)
        s = jnp.where(rows >= cols, s, NEG)
    return s, q, k


def _below_or_on_diagonal(i, j, step):
    @pl.when(j < i)
    def _():
        step(False)

    @pl.when(j == i)
    def _():
        step(True)


def attention_fwd(qn, qp, kn, kp, v):
    t = qn.shape[0]
    h = qn.shape[1] // HEAD_W
    tq = _att_block(t)
    nq = t // tq

    def body(qn_ref, qp_ref, kn_ref, kp_ref, v_ref, o_ref, lse_ref, m_ref, l_ref, acc_ref):
        i, j = pl.program_id(1), pl.program_id(2)

        @pl.when(j == 0)
        def _():
            m_ref[...] = jnp.full_like(m_ref, NEG)
            l_ref[...] = jnp.zeros_like(l_ref)
            acc_ref[...] = jnp.zeros_like(acc_ref)

        def step(diag):
            s, _, _ = _scores(qn_ref[...], qp_ref[...], kn_ref[...], kp_ref[...], diag)
            m_new = jnp.maximum(m_ref[...], jnp.max(s, axis=-1, keepdims=True))
            a = jnp.exp(m_ref[...] - m_new)
            p = jnp.exp(s - m_new)
            l_ref[...] = a * l_ref[...] + jnp.sum(p, axis=-1, keepdims=True)
            acc_ref[...] = a * acc_ref[...] + jnp.dot(p.astype(BF), v_ref[...], preferred_element_type=F32)
            m_ref[...] = m_new

        _below_or_on_diagonal(i, j, step)

        @pl.when(j == nq - 1)
        def _():
            o_ref[...] = acc_ref[...] / l_ref[...]
            lse_ref[...] = m_ref[...] + jnp.log(l_ref[...])

    qspec = pl.BlockSpec((tq, HEAD_W), lambda hh, i, j: (i, hh))
    kspec = pl.BlockSpec((tq, HEAD_W), lambda hh, i, j: (jnp.minimum(i, j), hh))
    kpspec = pl.BlockSpec((tq, HEAD_W), lambda hh, i, j: (jnp.minimum(i, j), 0))
    return pl.pallas_call(
        body, name="attention_fwd", grid=(h, nq, nq),
        in_specs=[qspec, qspec, kspec, kpspec, kspec],
        out_specs=[qspec, pl.BlockSpec((None, tq, 1), lambda hh, i, j: (hh, i, 0))],
        out_shape=[jax.ShapeDtypeStruct((t, h * HEAD_W), F32), jax.ShapeDtypeStruct((h, t, 1), F32)],
        scratch_shapes=[pltpu.VMEM((tq, 1), F32), pltpu.VMEM((tq, 1), F32), pltpu.VMEM((tq, HEAD_W), F32)],
        compiler_params=_cparams(("parallel", "parallel", "arbitrary")),
    )(qn, qp, kn, kp, v)


def attention_delta(o, do):
    t = o.shape[0]
    h = o.shape[1] // HEAD_W
    tq = _att_block(t)

    def body(o_ref, do_ref, d_ref):
        d_ref[...] = jnp.sum(o_ref[...] * do_ref[...], axis=-1, keepdims=True)

    spec = pl.BlockSpec((tq, HEAD_W), lambda hh, i: (i, hh))
    return pl.pallas_call(
        body, name="attention_delta", grid=(h, t // tq), in_specs=[spec, spec],
        out_specs=pl.BlockSpec((None, tq, 1), lambda hh, i: (hh, i, 0)),
        out_shape=jax.ShapeDtypeStruct((h, t, 1), F32),
        compiler_params=_cparams(("parallel", "parallel")),
    )(o, do)


def _att_grads(qn_ref, qp_ref, kn_ref, kp_ref, v_ref, do_ref, lse_ref, dl_ref, diag):
    s, q, k = _scores(qn_ref[...], qp_ref[...], kn_ref[...], kp_ref[...], diag)
    p = jnp.exp(s - lse_ref[...])
    do = do_ref[...].astype(BF)
    dp = lax.dot_general(do, v_ref[...], (((1,), (1,)), ((), ())), preferred_element_type=F32)
    ds = p * (dp - dl_ref[...]) * ATT_SCALE
    return p, ds, q, k, do


def attention_bwd_q(qn, qp, kn, kp, v, do, lse, delta, name, carry=()):
    t = qn.shape[0]
    h = qn.shape[1] // HEAD_W
    tq = _att_block(t)
    nq = t // tq

    def body(qn_ref, qp_ref, kn_ref, kp_ref, v_ref, do_ref, lse_ref, dl_ref, dqn_ref, dqp_ref, acc_ref):
        i, j = pl.program_id(1), pl.program_id(2)

        @pl.when(j == 0)
        def _():
            acc_ref[...] = jnp.zeros_like(acc_ref)

        def step(diag):
            _, ds, _, k, _ = _att_grads(qn_ref, qp_ref, kn_ref, kp_ref, v_ref, do_ref, lse_ref, dl_ref, diag)
            acc_ref[...] += jnp.dot(ds.astype(BF), k, preferred_element_type=F32)

        _below_or_on_diagonal(i, j, step)

        @pl.when(j == nq - 1)
        def _():
            dqn_ref[...] = acc_ref[:, :HEAD_W]
            dqp_ref[...] = acc_ref[:, HEAD_W:]

    qspec = pl.BlockSpec((tq, HEAD_W), lambda hh, i, j: (i, hh))
    kspec = pl.BlockSpec((tq, HEAD_W), lambda hh, i, j: (jnp.minimum(i, j), hh))
    kpspec = pl.BlockSpec((tq, HEAD_W), lambda hh, i, j: (jnp.minimum(i, j), 0))
    vec = pl.BlockSpec((None, tq, 1), lambda hh, i, j: (hh, i, 0))
    call = dict(
        name=name, grid=(h, nq, nq),
        in_specs=[qspec, qspec, kspec, kpspec, kspec, qspec, vec, vec],
        out_specs=[qspec, qspec],
        out_shape=[jax.ShapeDtypeStruct((t, h * HEAD_W), F32)] * 2,
        scratch_shapes=[pltpu.VMEM((tq, 2 * HEAD_W), F32)],
    )
    args = (qn, qp, kn, kp, v, do, lse, delta)
    if carry:
        res = carried_call(body, carry, args=args, **call)
        return res[:2], res[2:]
    return pl.pallas_call(body, compiler_params=_cparams(("parallel", "parallel", "arbitrary")), **call)(*args), []


def attention_bwd_kv(qn, qp, kn, kp, v, do, lse, delta):
    t = qn.shape[0]
    h = qn.shape[1] // HEAD_W
    tq = _att_block(t)
    nq = t // tq

    def body(qn_ref, qp_ref, kn_ref, kp_ref, v_ref, do_ref, lse_ref, dl_ref, dkn_ref, dkp_ref, dv_ref,
             dk_acc, dv_acc):
        j, hh, i = pl.program_id(0), pl.program_id(1), pl.program_id(2)

        @pl.when(i == 0)
        def _():
            dk_acc[...] = jnp.zeros_like(dk_acc)
            dv_acc[...] = jnp.zeros_like(dv_acc)

        @pl.when(jnp.logical_and(i == 0, hh == 0))
        def _():
            dkp_ref[...] = jnp.zeros_like(dkp_ref)

        def step(diag):
            p, ds, q, _, do = _att_grads(qn_ref, qp_ref, kn_ref, kp_ref, v_ref, do_ref, lse_ref, dl_ref, diag)
            dv_acc[...] += lax.dot_general(p.astype(BF), do, (((0,), (0,)), ((), ())), preferred_element_type=F32)
            dk_acc[...] += lax.dot_general(ds.astype(BF), q, (((0,), (0,)), ((), ())), preferred_element_type=F32)

        _below_or_on_diagonal(i, j, step)

        @pl.when(i == nq - 1)
        def _():
            dkn_ref[...] = dk_acc[:, :HEAD_W]
            dkp_ref[...] += dk_acc[:, HEAD_W:]
            dv_ref[...] = dv_acc[...]

    qspec = pl.BlockSpec((tq, HEAD_W), lambda j, hh, i: (jnp.maximum(i, j), hh))
    kspec = pl.BlockSpec((tq, HEAD_W), lambda j, hh, i: (j, hh))
    kpspec = pl.BlockSpec((tq, HEAD_W), lambda j, hh, i: (j, 0))
    vec = pl.BlockSpec((None, tq, 1), lambda j, hh, i: (hh, jnp.maximum(i, j), 0))
    return pl.pallas_call(
        body, name="attention_bwd_kv", grid=(nq, h, nq),
        in_specs=[qspec, qspec, kspec, kpspec, kspec, qspec, vec, vec],
        out_specs=[kspec, kpspec, kspec],
        out_shape=[jax.ShapeDtypeStruct((t, h * HEAD_W), F32), jax.ShapeDtypeStruct((t, HEAD_W), F32),
                   jax.ShapeDtypeStruct((t, h * HEAD_W), F32)],
        scratch_shapes=[pltpu.VMEM((tq, 2 * HEAD_W), F32), pltpu.VMEM((tq, HEAD_W), F32)],
        compiler_params=_cparams(("parallel", "arbitrary", "arbitrary")),
    )(qn, qp, kn, kp, v, do, lse, delta)


def attention_bwd(qn, qp, kn, kp, v, do, lse, delta, name, carry=()):
    t = qn.shape[0]
    h = qn.shape[1] // HEAD_W
    tq = _att_block(t)
    nq = t // tq

    def body(qn_ref, qp_ref, kn_ref, kp_ref, v_ref, do_ref, lse_ref, dl_ref,
             dqn_ref, dqp_ref, dkn_ref, dkp_ref, dv_ref, dq_acc, dk_acc, dv_acc):
        j, i = pl.program_id(1), pl.program_id(2)
        rows = pl.ds(pl.multiple_of(i * tq, tq), tq)

        @pl.when(i == 0)
        def _():
            dk_acc[...] = jnp.zeros_like(dk_acc)
            dv_acc[...] = jnp.zeros_like(dv_acc)

        @pl.when(j == 0)
        def _():
            dq_acc[rows, :] = jnp.zeros((tq, 2 * HEAD_W), F32)

        def step(diag):
            p, ds, q, k, dov = _att_grads(qn_ref, qp_ref, kn_ref, kp_ref, v_ref, do_ref, lse_ref, dl_ref, diag)
            dsb = ds.astype(BF)
            dv_acc[...] += lax.dot_general(p.astype(BF), dov, (((0,), (0,)), ((), ())), preferred_element_type=F32)
            dk_acc[...] += lax.dot_general(dsb, q, (((0,), (0,)), ((), ())), preferred_element_type=F32)
            dq_acc[rows, :] += jnp.dot(dsb, k, preferred_element_type=F32)

        _below_or_on_diagonal(i, j, step)

        @pl.when(i == nq - 1)
        def _():
            dkn_ref[...] = dk_acc[:, :HEAD_W]
            dkp_ref[...] = dk_acc[:, HEAD_W:]
            dv_ref[...] = dv_acc[...]

        @pl.when(j == nq - 1)
        def _():
            dqn_ref[...] = dq_acc[rows, :HEAD_W]
            dqp_ref[...] = dq_acc[rows, HEAD_W:]

    qspec = pl.BlockSpec((tq, HEAD_W), lambda hh, j, i: (jnp.maximum(i, j), hh))
    kspec = pl.BlockSpec((tq, HEAD_W), lambda hh, j, i: (j, hh))
    kpspec = pl.BlockSpec((tq, HEAD_W), lambda hh, j, i: (j, 0))
    vec = pl.BlockSpec((None, tq, 1), lambda hh, j, i: (hh, jnp.maximum(i, j), 0))
    dqspec = pl.BlockSpec((tq, HEAD_W), lambda hh, j, i: (jnp.where(j == nq - 1, i, 0), hh))
    wide = jax.ShapeDtypeStruct((t, h * HEAD_W), F32)
    call = dict(
        name=name, grid=(h, nq, nq),
        in_specs=[qspec, qspec, kspec, kpspec, kspec, qspec, vec, vec],
        out_specs=[dqspec, dqspec, kspec, kspec, kspec],
        out_shape=[wide] * 5,
        scratch_shapes=[pltpu.VMEM((t, 2 * HEAD_W), F32), pltpu.VMEM((tq, 2 * HEAD_W), F32),
                        pltpu.VMEM((tq, HEAD_W), F32)],
    )
    args = (qn, qp, kn, kp, v, do, lse, delta)
    if carry:
        res = carried_call(body, carry, args=args, **call)
        return res[:5], res[5:]
    return pl.pallas_call(body, compiler_params=_cparams(("arbitrary",) * 3), **call)(*args), []


def _sum_slabs(*slabs):
    return functools.reduce(lambda a, b: a + b, slabs)


def _even(j):
    return 2 * j


def _odd(j):
    return 2 * j + 1


def _rope_consts():
    lane = np.arange(HEAD_W)
    half = ROPE // 2
    swap = np.zeros((HEAD_W, HEAD_W), np.float32)
    swap[lane[:half] + half, lane[:half]] = 1.0
    swap[lane[:half], lane[:half] + half] = 1.0
    sign = np.where(lane < half, -1.0, np.where(lane < ROPE, 1.0, 0.0)).astype(np.float32)[None]
    inv_freq = ROPE_THETA ** (-jnp.arange(0, ROPE, 2, dtype=F32) / ROPE)
    inv_freq = jnp.concatenate([inv_freq, inv_freq, jnp.zeros((HEAD_W - ROPE,), F32)])[None]
    return jnp.asarray(swap), jnp.asarray(sign), inv_freq


def _pad_gain(g):
    return g[None, :HEAD_W], jnp.pad(g[HEAD_W:], (0, HEAD_W - ROPE))[None]


def mla_fwd(x, cos, sin, mix_gain, w_in, q_gain, kv_gain, w_q, w_kv, qh_gain, kh_gain, w_out, tag):
    d = x.shape[1]
    nh = MLA_HEADS
    swap = _rope_consts()[0]
    (h,) = rowwise(_rms, [_full(x)], [mix_gain], [(d, d, _col0, BF)], name=f"{tag}_norm")
    proj = matmul(h, w_in, "nn", name=f"{tag}_in")
    (qlat,) = rowwise(_rms, [(proj, LORA, _col0)], [q_gain], [(LORA, LORA, _col0, BF)], name=f"{tag}_qnorm")
    (kvlat,) = rowwise(_rms, [(proj, LORA, lambda j: 1)], [kv_gain], [(LORA, LORA, _col0, BF)],
                       name=f"{tag}_kvnorm")
    q = matmul(qlat, w_q, "nn", name=f"{tag}_qup")
    kv = matmul(kvlat, w_kv, "nn", name=f"{tag}_kvup")
    qgn, qgp = _pad_gain(qh_gain)
    kgn, kgp = _pad_gain(kh_gain)
    w = nh * HEAD_W
    qn, qp = rowwise(_q_head, [(q, HEAD_W, _even), (q, HEAD_W, _odd), _full(cos), _full(sin)], [qgn, qgp, swap],
                     [(w, HEAD_W, _colj, BF), (w, HEAD_W, _colj, BF)], ncol=nh, name=f"{tag}_qhead")
    kn, v = rowwise(_k_head, [(kv, HEAD_W, _even), (kv, HEAD_W, _odd)], [kgn],
                    [(w, HEAD_W, _colj, BF), (w, HEAD_W, _colj, BF)], ncol=nh, name=f"{tag}_khead")
    (kp,) = rowwise(_kpe_head, [(proj, HEAD_W, lambda j: 2 * LORA // HEAD_W), _full(cos), _full(sin)], [kgp, swap],
                    [(HEAD_W, HEAD_W, _col0, BF)], name=f"{tag}_kpe")
    o, lse = attention_fwd(qn, qp, kn, kp, v)
    y = matmul(o, w_out, "nn", add=x, name=f"{tag}_out")
    return y, (x, h, proj, qlat, kvlat, q, kv, qn, qp, kn, kp, v, o, lse)


def mla_bwd(dy, saved, cos, sin, mix_gain, w_in, q_gain, kv_gain, w_q, w_kv, qh_gain, kh_gain, w_out, tag, carry=()):
    x, h, proj, qlat, kvlat, q, kv, qn, qp, kn, kp, v, o, lse = saved
    d = x.shape[1]
    nh = MLA_HEADS
    w = nh * HEAD_W
    swap = _rope_consts()[0]
    qgn, qgp = _pad_gain(qh_gain)
    kgn, kgp = _pad_gain(kh_gain)
    do = matmul(dy, w_out, "nt", name=f"{tag}_out_dx")
    dw_out = matmul(o, dy, "tn", name=f"{tag}_out_dw")
    delta = attention_delta(o, do)
    (dqn, dqp, dkn, dkp_heads, dv), carried = attention_bwd(qn, qp, kn, kp, v, do, lse, delta,
                                                            f"{tag}_attention_bwd", carry)
    (dkp,) = rowwise(_sum_slabs, [(dkp_heads, HEAD_W, lambda j, hh=hh: hh) for hh in range(nh)], [],
                     [(HEAD_W, HEAD_W, _col0, F32)], name=f"{tag}_kpe_sum")
    (dq_a, dq_b), (dqgn, dqgp) = rowwise_vjp(
        _q_head, [(q, HEAD_W, _even), (q, HEAD_W, _odd), _full(cos), _full(sin)], [qgn, qgp],
        [(dqn, HEAD_W, _colj), (dqp, HEAD_W, _colj)],
        [(w, HEAD_W, _colj, None), (w, HEAD_W, _colj, None), None, None], ncol=nh, consts=[swap],
        name=f"{tag}_qhead_bwd")
    dq = _interleave(dq_a, dq_b)
    (dkv_a, dkv_b), (dkgn,) = rowwise_vjp(
        _k_head, [(kv, HEAD_W, _even), (kv, HEAD_W, _odd)], [kgn], [(dkn, HEAD_W, _colj), (dv, HEAD_W, _colj)],
        [(w, HEAD_W, _colj, None), (w, HEAD_W, _colj, None)], ncol=nh, name=f"{tag}_khead_bwd")
    dkv = _interleave(dkv_a, dkv_b)
    (dpe,), (dkgp,) = rowwise_vjp(
        _kpe_head, [(proj, HEAD_W, lambda j: 2 * LORA // HEAD_W), _full(cos), _full(sin)], [kgp], [_full(dkp)],
        [(HEAD_W, HEAD_W, _col0, None), None, None], consts=[swap], name=f"{tag}_kpe_bwd")
    dw_q = matmul(qlat, dq, "tn", name=f"{tag}_qup_dw")
    dw_kv = matmul(kvlat, dkv, "tn", name=f"{tag}_kvup_dw")
    dqlat = matmul(dq, w_q, "nt", name=f"{tag}_qup_dx")
    dkvlat = matmul(dkv, w_kv, "nt", name=f"{tag}_kvup_dx")
    (dpq,), (dq_gain,) = rowwise_vjp(_rms, [(proj, LORA, _col0)], [q_gain], [_full(dqlat)],
                                     [(LORA, LORA, _col0, None)], name=f"{tag}_qnorm_bwd")
    (dpkv,), (dkv_gain,) = rowwise_vjp(_rms, [(proj, LORA, lambda j: 1)], [kv_gain], [_full(dkvlat)],
                                       [(LORA, LORA, _col0, None)], name=f"{tag}_kvnorm_bwd")
    dproj = jnp.concatenate([dpq, dpkv, dpe], axis=1)
    dw_in = matmul(h, dproj, "tn", name=f"{tag}_in_dw")
    dh = matmul(dproj, w_in, "nt", name=f"{tag}_in_dx")
    (dx,), (dmix,) = rowwise_vjp(_rms, [_full(x)], [mix_gain], [_full(dh)], [(d, d, _col0, dy)],
                                 name=f"{tag}_norm_bwd")
    dqh = jnp.concatenate([dqgn[0], dqgp[0, :ROPE]])
    dkh = jnp.concatenate([dkgn[0], dkgp[0, :ROPE]])
    return dx, (dmix, dw_in, dq_gain, dkv_gain, dw_q, dw_kv, dqh, dkh, dw_out), carried


def _interleave(a, b):
    t, w = a.shape
    n = w // HEAD_W
    return jnp.stack([a.reshape(t, n, HEAD_W), b.reshape(t, n, HEAD_W)], axis=2).reshape(t, 2 * w)


GDN_HEADS_PER_STEP = 8
CONV_COLS = 512
HALO = 8


def conv_fwd(x, w):
    t = x.shape[0]
    c = w.shape[1]
    tb = _row_block(t)
    assert t % tb == 0

    def body(x_ref, prev_ref, w_ref, y_ref):
        i = pl.program_id(1)
        xv = x_ref[...]
        prev = jnp.where(i > 0, prev_ref[...], 0.0)
        ext = jnp.concatenate([prev, xv], axis=0)
        acc = xv * w_ref[CONV_K - 1:CONV_K, :]
        for s in range(1, CONV_K):
            acc = acc + pltpu.roll(ext, s, 0)[HALO:] * w_ref[CONV_K - 1 - s:CONV_K - s, :]
        y_ref[...] = acc

    spec = pl.BlockSpec((tb, CONV_COLS), lambda j, i: (i, j))
    return pl.pallas_call(
        body, name="conv_fwd", grid=(c // CONV_COLS, t // tb),
        in_specs=[spec, pl.BlockSpec((HALO, CONV_COLS), lambda j, i: (jnp.maximum(i * (tb // HALO) - 1, 0), j)),
                  pl.BlockSpec((HALO, CONV_COLS), lambda j, i: (0, j))],
        out_specs=spec, out_shape=jax.ShapeDtypeStruct((t, c), F32),
        compiler_params=_cparams(("parallel", "parallel")),
    )(x, x, w)


def conv_bwd(x, dy, w):
    t = x.shape[0]
    c = w.shape[1]
    tb = _row_block(t)
    nrb = t // tb

    def body(x_ref, prev_ref, dy_ref, next_ref, w_ref, dx_ref, dw_ref):
        i = pl.program_id(1)
        ext_x = jnp.concatenate([jnp.where(i > 0, prev_ref[...], 0.0), x_ref[...]], axis=0)
        dyv = dy_ref[...]
        ext_dy = jnp.concatenate([dyv, jnp.where(i < nrb - 1, next_ref[...], 0.0)], axis=0)

        @pl.when(i == 0)
        def _():
            dw_ref[...] = jnp.zeros_like(dw_ref)

        acc = dyv * w_ref[CONV_K - 1:CONV_K, :]
        dw_ref[CONV_K - 1:CONV_K, :] += jnp.sum(dyv * x_ref[...], axis=0, keepdims=True)
        for s in range(1, CONV_K):
            acc = acc + pltpu.roll(ext_dy, tb + HALO - s, 0)[:tb] * w_ref[CONV_K - 1 - s:CONV_K - s, :]
            dw_ref[CONV_K - 1 - s:CONV_K - s, :] += jnp.sum(dyv * pltpu.roll(ext_x, s, 0)[HALO:], axis=0, keepdims=True)
        dx_ref[...] = acc

    spec = pl.BlockSpec((tb, CONV_COLS), lambda j, i: (i, j))
    wspec = pl.BlockSpec((HALO, CONV_COLS), lambda j, i: (0, j))
    return pl.pallas_call(
        body, name="conv_bwd", grid=(c // CONV_COLS, nrb),
        in_specs=[spec, pl.BlockSpec((HALO, CONV_COLS), lambda j, i: (jnp.maximum(i * (tb // HALO) - 1, 0), j)),
                  spec, pl.BlockSpec((HALO, CONV_COLS), lambda j, i: (jnp.minimum(i + 1, nrb - 1) * (tb // HALO), j)),
                  wspec],
        out_specs=[spec, wspec],
        out_shape=[jax.ShapeDtypeStruct((t, c), F32), jax.ShapeDtypeStruct((HALO, c), F32)],
        compiler_params=_cparams(("parallel", "arbitrary")),
    )(x, x, dy, dy, w)


def _l2_silu(c):
    a = _silu(c)
    return a * lax.rsqrt(jnp.sum(a * a, axis=-1, keepdims=True) + EPS)


def _gates(ab, a_log, dt_bias, ea, eb):
    g = -jnp.exp(a_log) * _softplus(hdot(ab, ea) + dt_bias)
    return g, _sigmoid(hdot(ab, eb))


def _gate_consts():
    ea = np.zeros((HEAD_W, GDN_W), np.float32)
    eb = np.zeros((HEAD_W, GDN_W), np.float32)
    for h in range(GDN_HEADS):
        ea[h, h * HEAD_W:(h + 1) * HEAD_W] = 1.0
        eb[GDN_HEADS + h, h * HEAD_W:(h + 1) * HEAD_W] = 1.0
    return jnp.asarray(ea), jnp.asarray(eb)


def _batched_dots(precision, to_bf16):
    def raw(a, b, ca, cb):
        if to_bf16:
            a, b = a.astype(BF), b.astype(BF)
        return lax.dot_general(a, b, (((ca,), (cb,)), ((0,), (0,))), precision=precision,
                               preferred_element_type=F32)

    nn = jax.custom_vjp(lambda a, b: raw(a, b, 2, 1))
    nt = jax.custom_vjp(lambda a, b: raw(a, b, 2, 2))
    tn = jax.custom_vjp(lambda a, b: raw(a, b, 1, 1))
    nn.defvjp(lambda a, b: (nn(a, b), (a, b)), lambda r, g: (nt(g, r[1]), tn(r[0], g)))
    nt.defvjp(lambda a, b: (nt(a, b), (a, b)), lambda r, g: (nn(g, r[1]), tn(g, r[0])))
    tn.defvjp(lambda a, b: (tn(a, b), (a, b)), lambda r, g: (nt(r[1], g), nn(r[0], g)))
    return nn, nt, tn


bmm_nn, bmm_nt, bmm_tn = _batched_dots(None, True)
bh_nn, bh_nt, bh_tn = _batched_dots(HI, False)


def _gdn_chunk(q, k, v, g, beta, state):
    n = CHUNK
    nb = q.shape[0]
    ii = lax.broadcasted_iota(jnp.int32, (nb, n, n), 1)
    jj = lax.broadcasted_iota(jnp.int32, (nb, n, n), 2)
    causal = ii >= jj
    eye = (ii == jj).astype(F32)
    gc = bh_nn(causal.astype(F32), g)
    mean_w = jnp.full((nb, n, HEAD_W), 1.0 / HEAD_W, F32)
    gc_i = bh_nt(gc, mean_w)
    gc_j = bh_nt(mean_w, gc)
    decay = jnp.exp(jnp.where(causal, gc_i - gc_j, -1e30))
    kb = k * beta
    vb = v * beta
    m = jnp.where(ii > jj, bmm_nt(kb, k) * decay, 0.0)
    inv = eye - m
    pw = m
    for _ in range(5):
        pw = bh_nn(pw, pw)
        inv = bh_nn(inv, eye + pw)
    eg = jnp.exp(gc)
    u = bh_nn(inv, vb)
    w = bh_nn(inv, kb * eg)
    qs = q * (HEAD_W ** -0.5)
    attn = bmm_nt(qs, k) * decay
    g_last = bh_nn((jj == n - 1).astype(F32), gc)
    k_dec = k * jnp.exp(g_last - gc)
    v_new = u - bmm_nn(w, state)
    out = bmm_nn(qs * eg, state) + bmm_nn(attn, v_new)
    new_state = state * jnp.exp(jnp.concatenate([g_last, g_last], axis=1)) + bmm_tn(k_dec, v_new)
    return out, new_state


def _heads(ref, hb):
    return jnp.stack([ref[:, j * HEAD_W:(j + 1) * HEAD_W] for j in range(hb)])


def gdn_fwd(qk, v, g, beta):
    t = v.shape[0]
    n = t // CHUNK
    nh = GDN_HEADS

    hb = GDN_HEADS_PER_STEP
    ng = nh // hb

    def body(q_ref, k_ref, v_ref, g_ref, b_ref, o_ref, s_ref, state):
        @pl.when(pl.program_id(1) == 0)
        def _():
            state[...] = jnp.zeros_like(state)

        old = state[...]
        out, new = _gdn_chunk(*[_heads(r, hb) for r in (q_ref, k_ref, v_ref, g_ref, b_ref)], old)
        s_ref[:, 0] = old
        for j in range(hb):
            o_ref[:, j * HEAD_W:(j + 1) * HEAD_W] = out[j]
        state[...] = new

    spec = pl.BlockSpec((CHUNK, hb * HEAD_W), lambda h, c: (c, h))
    return pl.pallas_call(
        body, name="gdn_fwd", grid=(ng, n),
        in_specs=[spec, pl.BlockSpec((CHUNK, hb * HEAD_W), lambda h, c: (c, ng + h)), spec, spec, spec],
        out_specs=[spec, pl.BlockSpec((hb, 1, HEAD_W, HEAD_W), lambda h, c: (h, c, 0, 0))],
        out_shape=[jax.ShapeDtypeStruct((t, nh * HEAD_W), F32), jax.ShapeDtypeStruct((nh, n, HEAD_W, HEAD_W), F32)],
        scratch_shapes=[pltpu.VMEM((hb, HEAD_W, HEAD_W), F32)],
        compiler_params=_cparams(("parallel", "arbitrary")),
    )(qk, qk, v, g, beta)


def gdn_bwd(qk, v, g, beta, states, do, name, carry=()):
    t = v.shape[0]
    n = t // CHUNK
    nh = GDN_HEADS

    hb = GDN_HEADS_PER_STEP
    ng = nh // hb

    def body(q_ref, k_ref, v_ref, g_ref, b_ref, s_ref, do_ref, dq_ref, dk_ref, dv_ref, dg_ref, db_ref, dstate):
        @pl.when(pl.program_id(1) == 0)
        def _():
            dstate[...] = jnp.zeros_like(dstate)

        _, pull = jax.vjp(_gdn_chunk, *[_heads(r, hb) for r in (q_ref, k_ref, v_ref, g_ref, b_ref)], s_ref[:, 0])
        grads = pull((_heads(do_ref, hb), dstate[...]))
        for ref, gr in zip((dq_ref, dk_ref, dv_ref, dg_ref, db_ref), grads[:5]):
            for j in range(hb):
                ref[:, j * HEAD_W:(j + 1) * HEAD_W] = gr[j]
        dstate[...] = grads[5]

    spec = pl.BlockSpec((CHUNK, hb * HEAD_W), lambda h, c: (n - 1 - c, h))
    call = dict(
        name=name, grid=(ng, n),
        in_specs=[spec, pl.BlockSpec((CHUNK, hb * HEAD_W), lambda h, c: (n - 1 - c, ng + h)), spec, spec, spec,
                  pl.BlockSpec((hb, 1, HEAD_W, HEAD_W), lambda h, c: (h, n - 1 - c, 0, 0)), spec],
        out_specs=[spec] * 5,
        out_shape=[jax.ShapeDtypeStruct((t, nh * HEAD_W), F32)] * 5,
        scratch_shapes=[pltpu.VMEM((hb, HEAD_W, HEAD_W), F32)],
    )
    args = (qk, qk, v, g, beta, states, do)
    if carry:
        res = carried_call(body, carry, args=args, **call)
        return res[:5], res[5:]
    return pl.pallas_call(body, compiler_params=_cparams(("parallel", "arbitrary")), **call)(*args), []


def _gdn_post(o, z, gain):
    return _rms(o, gain) * _silu(z)


POOL_HALO = 16


def _pool_counts(t0, rows, cols):
    tt = t0 + lax.broadcasted_iota(jnp.int32, (rows, cols), 0) + 1
    grp = lax.broadcasted_iota(jnp.int32, (rows, cols), 1) // POOL_GROUP_W
    win = jnp.left_shift(2, grp)
    return jnp.minimum(tt, win).astype(F32), grp


def _by_group(grp, parts):
    out = parts[-1]
    for gi in range(len(parts) - 2, -1, -1):
        out = jnp.where(grp == gi, parts[gi], out)
    return out


def pool_window_fwd(u):
    t, c = u.shape
    tb = _row_block(t)

    def body(u_ref, prev_ref, d_ref):
        i = pl.program_id(0)
        xv = u_ref[...]
        ext = jnp.concatenate([jnp.where(i > 0, prev_ref[...], 0.0), xv], axis=0)
        sums = []
        s = ext
        for sh in (1, 2, 4, 8):
            s = s + pltpu.roll(s, sh, 0)
            sums.append(s[POOL_HALO:])
        cnt, grp = _pool_counts(i * tb, tb, c)
        d_ref[...] = _by_group(grp, sums) / cnt - xv

    spec = pl.BlockSpec((tb, c), lambda i: (i, 0))
    return pl.pallas_call(
        body, name="pool_window_fwd", grid=(t // tb,),
        in_specs=[spec, pl.BlockSpec((POOL_HALO, c), lambda i: (jnp.maximum(i * (tb // POOL_HALO) - 1, 0), 0))],
        out_specs=spec, out_shape=jax.ShapeDtypeStruct((t, c), F32),
        compiler_params=_cparams(("parallel",)),
    )(u, u)


def pool_window_bwd(dd):
    t, c = dd.shape
    tb = _row_block(t)
    nrb = t // tb
    length = tb + POOL_HALO

    def body(d_ref, next_ref, du_ref):
        i = pl.program_id(0)
        dv = d_ref[...]
        ext = jnp.concatenate([dv, jnp.where(i < nrb - 1, next_ref[...], 0.0)], axis=0)
        cnt, grp = _pool_counts(i * tb, length, c)
        s = ext / cnt
        sums = []
        for sh in (1, 2, 4, 8):
            s = s + pltpu.roll(s, length - sh, 0)
            sums.append(s[:tb])
        du_ref[...] = _by_group(grp[:tb], sums) - dv

    spec = pl.BlockSpec((tb, c), lambda i: (i, 0))
    return pl.pallas_call(
        body, name="pool_window_bwd", grid=(nrb,),
        in_specs=[spec, pl.BlockSpec((POOL_HALO, c), lambda i: (jnp.minimum(i + 1, nrb - 1) * (tb // POOL_HALO), 0))],
        out_specs=spec, out_shape=jax.ShapeDtypeStruct((t, c), F32),
        compiler_params=_cparams(("parallel",)),
    )(dd, dd)


def _pool_mix(d, w, scale):
    return mm_nn(d, w) * scale


def pool_mix_fwd(diff, w, scale):
    t = diff.shape[0]
    tb = _row_block(t)
    gw = POOL_GROUP_W

    def body(d_ref, w_ref, s_ref, o_ref):
        o_ref[...] = _pool_mix(d_ref[...], w_ref[...], s_ref[...]).astype(o_ref.dtype)

    spec = pl.BlockSpec((tb, gw), lambda i, g: (i, g))
    return pl.pallas_call(
        body, name="pool_mix_fwd", grid=(t // tb, POOL_W // gw),
        in_specs=[spec, pl.BlockSpec((None, gw, gw), lambda i, g: (g, 0, 0)), pl.BlockSpec((1, gw), lambda i, g: (0, g))],
        out_specs=spec, out_shape=jax.ShapeDtypeStruct((t, POOL_W), BF),
        compiler_params=_cparams(("parallel", "parallel")),
    )(diff, w, scale)


def pool_mix_bwd(diff, w, scale, dp, dp_col0):
    t = diff.shape[0]
    tb = _row_block(t)
    gw = POOL_GROUP_W

    def body(d_ref, w_ref, s_ref, dp_ref, dd_ref, dw_ref, ds_ref):
        @pl.when(pl.program_id(1) == 0)
        def _():
            dw_ref[...] = jnp.zeros_like(dw_ref)
            ds_ref[...] = jnp.zeros_like(ds_ref)

        _, pull = jax.vjp(_pool_mix, d_ref[...], w_ref[...].astype(F32), s_ref[...])
        dd, dw, ds = pull(dp_ref[...])
        dd_ref[...] = dd
        dw_ref[...] += dw
        ds_ref[...] += ds

    spec = pl.BlockSpec((tb, gw), lambda g, i: (i, g))
    wspec = pl.BlockSpec((None, gw, gw), lambda g, i: (g, 0, 0))
    sspec = pl.BlockSpec((1, gw), lambda g, i: (0, g))
    return pl.pallas_call(
        body, name="pool_mix_bwd", grid=(POOL_W // gw, t // tb),
        in_specs=[spec, wspec, sspec, pl.BlockSpec((tb, gw), lambda g, i: (i, dp_col0 + g))],
        out_specs=[spec, wspec, sspec],
        out_shape=[jax.ShapeDtypeStruct((t, POOL_W), F32), jax.ShapeDtypeStruct(w.shape, F32),
                   jax.ShapeDtypeStruct(scale.shape, F32)],
        compiler_params=_cparams(("parallel", "arbitrary")),
    )(diff, w, scale, dp)


def hyb_fwd(x, mix_gain, w_qkvz, w_ab, w_u, conv_w, a_log, dt_bias, out_gain, pool_w, pool_scale, w_out, tag):
    d = x.shape[1]
    ea, eb = _gate_consts()
    nh = GDN_HEADS
    (h,) = rowwise(_rms, [_full(x)], [mix_gain], [(d, d, _col0, BF)], name=f"{tag}_norm")
    p1 = matmul(h, w_qkvz, "nn", name=f"{tag}_in_qkvz")
    ab = matmul(h, w_ab, "nn", name=f"{tag}_in_ab")
    u = matmul(h, w_u, "nn", name=f"{tag}_in_u")
    cv = conv_fwd(p1, conv_w)
    (qk,) = rowwise(_l2_silu, [(cv, HEAD_W, _colj)], [], [(2 * GDN_W, HEAD_W, _colj, F32)], ncol=2 * nh,
                    name=f"{tag}_qk_act")
    (v,) = rowwise(_silu, [(cv, GDN_W, lambda j: 2)], [], [(GDN_W, GDN_W, _col0, F32)], name=f"{tag}_v_act")
    g, beta = rowwise(_gates, [_full(ab)], [a_log, dt_bias, ea, eb],
                      [(GDN_W, GDN_W, _col0, F32), (GDN_W, GDN_W, _col0, F32)], name=f"{tag}_gates")
    o, states = gdn_fwd(qk, v, g, beta)
    (on,) = rowwise(_gdn_post, [(o, HEAD_W, _colj), (p1, HEAD_W, lambda j: 3 * nh + j)], [out_gain],
                    [(GDN_W, HEAD_W, _colj, BF)], ncol=nh, name=f"{tag}_post")
    diff = pool_window_fwd(u)
    pm = pool_mix_fwd(diff, pool_w, pool_scale)
    cat = jnp.concatenate([on, pm], axis=1)
    y = matmul(cat, w_out, "nn", add=x, name=f"{tag}_out")
    return y, (x, h, p1, ab, cv, qk, v, g, beta, o, states, diff, cat)


def hyb_bwd(dy, saved, mix_gain, w_qkvz, w_ab, w_u, conv_w, a_log, dt_bias, out_gain, pool_w, pool_scale, w_out, tag,
            carry=()):
    x, h, p1, ab, cv, qk, v, g, beta, o, states, diff, cat = saved
    d = x.shape[1]
    nh = GDN_HEADS
    ea, eb = _gate_consts()
    dcat = matmul(dy, w_out, "nt", name=f"{tag}_out_dx")
    dw_out = matmul(cat, dy, "tn", name=f"{tag}_out_dw")
    (do, dz), (dout_gain,) = rowwise_vjp(
        _gdn_post, [(o, HEAD_W, _colj), (p1, HEAD_W, lambda j: 3 * nh + j)], [out_gain], [(dcat, HEAD_W, _colj)],
        [(GDN_W, HEAD_W, _colj, None), (GDN_W, HEAD_W, _colj, None)], ncol=nh, name=f"{tag}_post_bwd")
    ddiff, dpool_w, dpool_scale = pool_mix_bwd(diff, pool_w, pool_scale, dcat, GDN_W // POOL_GROUP_W)
    du = pool_window_bwd(ddiff)
    (dq, dk, dv, dg, dbeta), carried = gdn_bwd(qk, v, g, beta, states, do, f"{tag}_gdn_bwd", carry)
    (dab,), (da_log, ddt_bias) = rowwise_vjp(
        _gates, [_full(ab)], [a_log, dt_bias], [_full(dg), _full(dbeta)], [(HEAD_W, HEAD_W, _col0, None)],
        consts=[ea, eb], name=f"{tag}_gates_bwd")
    (dcq,), _ = rowwise_vjp(_l2_silu, [(cv, HEAD_W, _colj)], [], [(dq, HEAD_W, _colj)],
                            [(GDN_W, HEAD_W, _colj, None)], ncol=nh, par_grads=False, name=f"{tag}_q_act_bwd")
    (dck,), _ = rowwise_vjp(_l2_silu, [(cv, HEAD_W, lambda j: nh + j)], [], [(dk, HEAD_W, _colj)],
                            [(GDN_W, HEAD_W, _colj, None)], ncol=nh, par_grads=False, name=f"{tag}_k_act_bwd")
    (dcv,), _ = rowwise_vjp(_silu, [(cv, GDN_W, lambda j: 2)], [], [_full(dv)],
                            [(GDN_W, GDN_W, _col0, None)], par_grads=False, name=f"{tag}_v_act_bwd")
    dqkv, dconv_w = conv_bwd(p1, jnp.concatenate([dcq, dck, dcv], axis=1), conv_w)
    dp1 = jnp.concatenate([dqkv, dz], axis=1)
    dw_qkvz = matmul(h, dp1, "tn", name=f"{tag}_in_qkvz_dw")
    dw_ab = matmul(h, dab, "tn", name=f"{tag}_in_ab_dw")
    dw_u = matmul(h, du, "tn", name=f"{tag}_in_u_dw")
    dh = matmul(dp1, w_qkvz, "nt", name=f"{tag}_in_qkvz_dx")
    dh = matmul(dab, w_ab, "nt", add=dh, name=f"{tag}_in_ab_dx")
    dh = matmul(du, w_u, "nt", add=dh, name=f"{tag}_in_u_dx")
    (dx,), (dmix,) = rowwise_vjp(_rms, [_full(x)], [mix_gain], [_full(dh)], [(d, d, _col0, dy)],
                                 name=f"{tag}_norm_bwd")
    return dx, (dmix, dw_qkvz, dw_ab, dw_u, dconv_w, da_log, ddt_bias, dout_gain, dpool_w, dpool_scale, dw_out), carried


def loss_head(y, target):
    t, d = y.shape
    tb = _row_block(t)

    def body(y_ref, t_ref, dy_ref, loss_ref):
        @pl.when(pl.program_id(0) == 0)
        def _():
            loss_ref[...] = jnp.zeros_like(loss_ref)

        e = y_ref[...] - t_ref[...]
        dy_ref[...] = e * (1.0 / d)
        loss_ref[...] += 0.5 * jnp.sum(jnp.mean(e * e, axis=-1, keepdims=True))

    spec = pl.BlockSpec((tb, d), lambda i: (i, 0))
    return pl.pallas_call(
        body, name="loss_head", grid=(t // tb,), in_specs=[spec, spec],
        out_specs=[spec, pl.BlockSpec((8, 128), lambda i: (0, 0))],
        out_shape=[jax.ShapeDtypeStruct((t, d), F32), jax.ShapeDtypeStruct((8, 128), F32)],
        compiler_params=_cparams(("arbitrary",)),
    )(y, target)


ADAM_BLOCK_ELEMS = 256 * 1024


def _adam_rows(rows, cols):
    tb = 1024
    while tb >= 8:
        if rows % tb == 0 and tb * cols <= ADAM_BLOCK_ELEMS:
            return tb
        tb //= 2
    return rows


def adamw(w, g, m, v, name):
    rows, cols = w.shape
    tb = _adam_rows(rows, cols)
    c1 = 1.0 - ADAM_B1 ** ADAM_STEP
    c2 = 1.0 - ADAM_B2 ** ADAM_STEP

    def body(w_ref, g_ref, m_ref, v_ref, d_ref, nm_ref, nv_ref):
        gv = g_ref[...]
        nm = ADAM_B1 * m_ref[...] + (1.0 - ADAM_B1) * gv
        nv = ADAM_B2 * v_ref[...] + (1.0 - ADAM_B2) * (gv * gv)
        d_ref[...] = -ADAM_LR * ((nm / c1) / (jnp.sqrt(nv / c2) + ADAM_EPS) + ADAM_WD * w_ref[...])
        nm_ref[...] = nm
        nv_ref[...] = nv

    spec = pl.BlockSpec((tb, cols), lambda i: (i, 0))
    return pl.pallas_call(
        body, name=name, grid=(rows // tb,), in_specs=[spec] * 4, out_specs=[spec] * 3,
        out_shape=[jax.ShapeDtypeStruct((rows, cols), F32)] * 3,
        compiler_params=_cparams(("parallel",)),
    )(w, g, m, v)


LANES = 1024
HBM = pl.BlockSpec(memory_space=pltpu.HBM)


def _place():
    x, y, c = lax.axis_index("x"), lax.axis_index("y"), lax.axis_index("c")
    others = [(1 - x, y), (x, 1 - y), (1 - x, 1 - y)]
    return x, y, c, 2 * x + y, others


def _comm_call(body, name, out_shape, n_sems, *args):
    return pl.pallas_call(
        body, name=name, out_shape=out_shape, in_specs=[HBM] * len(args),
        out_specs=[HBM] * len(out_shape),
        scratch_shapes=[pltpu.SemaphoreType.DMA((n_sems,)), pltpu.SemaphoreType.DMA((n_sems,))],
        compiler_params=pltpu.CompilerParams(has_side_effects=True),
    )(*args)


AG_CHUNKS = 4


def all_gather_chips(xs):
    n = len(xs)
    for x in xs:
        assert x.shape[1] % (AG_CHUNKS * 16) == 0
    pieces = [(t, q) for t in range(n) for q in range(AG_CHUNKS)]
    base = 3 * len(pieces)

    def body(*refs):
        x_refs, out_refs, send_sems, recv_sems = refs[:n], refs[n:2 * n], refs[2 * n], refs[2 * n + 1]
        x, y, c, me, others = _place()
        sib = (x, y, 1 - c)

        def copy(k, src, dst, to):
            return pltpu.make_async_remote_copy(src_ref=src, dst_ref=dst, send_sem=send_sems.at[k],
                                                recv_sem=recv_sems.at[k], device_id=to, device_id_type=MESH)

        def rows(t, q):
            hc = x_refs[t].shape[1] // AG_CHUNKS
            return pl.ds(q * hc, hc)

        first, passed = [], []
        for p, (t, q) in enumerate(pieces):
            for k, chip in enumerate(others):
                cp = copy(p * 3 + k, x_refs[t].at[c, rows(t, q)], out_refs[t].at[me, c, rows(t, q)], (*chip, c))
                cp.start()
                first.append(cp)
        for p, (t, q) in enumerate(pieces):
            for k, (cx, cy) in enumerate(others):
                slot = out_refs[t].at[2 * cx + cy, c, rows(t, q)]
                copy(p * 3 + k, slot, slot, sib).wait_recv()
                fwd = copy(base + p * 3 + k, slot, slot, sib)
                fwd.start()
                passed.append(fwd)
        for p, (t, q) in enumerate(pieces):
            for k, (cx, cy) in enumerate(others):
                slot = out_refs[t].at[2 * cx + cy, 1 - c, rows(t, q)]
                copy(base + p * 3 + k, slot, slot, sib).wait_recv()
        for cp in first + passed:
            cp.wait_send()

    return _comm_call(body, "all_gather_chips", [jax.ShapeDtypeStruct((N_CHIPS,) + x.shape, x.dtype) for x in xs],
                      2 * base, *xs)


def pair_swap(gs, name):
    n = len(gs)

    def body(*refs):
        g_refs, r_refs, send_sems, recv_sems = refs[:n], refs[n:2 * n], refs[2 * n], refs[2 * n + 1]
        x, y, c, _, _ = _place()
        cps = [pltpu.make_async_remote_copy(src_ref=g.at[:, 1 - c], dst_ref=r, send_sem=send_sems.at[i],
                                            recv_sem=recv_sems.at[i], device_id=(x, y, 1 - c), device_id_type=MESH)
               for i, (g, r) in enumerate(zip(g_refs, r_refs))]
        for cp in cps:
            cp.start()
        for cp in cps:
            cp.wait()

    out = [jax.ShapeDtypeStruct((g.shape[0],) + g.shape[2:], g.dtype) for g in gs]
    return _comm_call(body, name, out, n, *gs)


def _chip_scatter_ops(s_refs, y_refs, send_sems, recv_sems):
    x, y, c, me, others = _place()

    def copies(to_peer):
        out = []
        for i, (s, dst) in enumerate(zip(s_refs, y_refs)):
            for k, (cx, cy) in enumerate(others):
                slot = dst.at[2 * cx + cy]
                out.append(pltpu.make_async_remote_copy(
                    src_ref=s.at[2 * cx + cy] if to_peer else slot, dst_ref=dst.at[me] if to_peer else slot,
                    send_sem=send_sems.at[3 * i + k], recv_sem=recv_sems.at[3 * i + k],
                    device_id=(cx, cy, c), device_id_type=MESH))
        return out

    def start():
        for cp in copies(True):
            cp.start()

    def wait():
        for cp in copies(False):
            cp.wait_recv()
        for cp in copies(True):
            cp.wait_send()

    return start, wait


def chip_scatter(ss, name):
    n = len(ss)

    def body(*refs):
        start, wait = _chip_scatter_ops(refs[:n], refs[n:2 * n], refs[2 * n], refs[2 * n + 1])
        start()
        wait()

    return _comm_call(body, name, [jax.ShapeDtypeStruct(s.shape, s.dtype) for s in ss], 3 * n, *ss)


def carried_call(body, carry, *, name, grid, in_specs, out_specs, out_shape, scratch_shapes, args):
    m = len(carry)
    n_in, n_out, n_scr = len(in_specs), len(out_specs), len(scratch_shapes)

    def wrapped(*refs):
        pos = [0]

        def take(k):
            part = refs[pos[0]:pos[0] + k]
            pos[0] += k
            return part

        ins, c_in, outs, c_out, scr, (send_sems, recv_sems) = take(n_in), take(m), take(n_out), take(m), take(n_scr), take(2)
        start, wait = _chip_scatter_ops(c_in, c_out, send_sems, recv_sems)
        ids = [pl.program_id(a) for a in range(len(grid))]
        first = functools.reduce(jnp.logical_and, [i == 0 for i in ids])
        last = functools.reduce(jnp.logical_and, [i == g - 1 for i, g in zip(ids, grid)])

        @pl.when(first)
        def _():
            start()

        body(*ins, *outs, *scr)

        @pl.when(last)
        def _():
            wait()

    return pl.pallas_call(
        wrapped, name=name, grid=grid, in_specs=list(in_specs) + [HBM] * m, out_specs=list(out_specs) + [HBM] * m,
        out_shape=list(out_shape) + [jax.ShapeDtypeStruct(s.shape, s.dtype) for s in carry],
        scratch_shapes=list(scratch_shapes) + [pltpu.SemaphoreType.DMA((3 * m,)), pltpu.SemaphoreType.DMA((3 * m,))],
        compiler_params=pltpu.CompilerParams(dimension_semantics=("arbitrary",) * len(grid),
                                             vmem_limit_bytes=VMEM_LIMIT, has_side_effects=True),
    )(*args, *carry)


def pair_join(fs, name):
    n = len(fs)

    def body(*refs):
        f_refs, o_refs, send_sems, recv_sems = refs[:n], refs[n:2 * n], refs[2 * n], refs[2 * n + 1]
        x, y, c, _, _ = _place()
        cps = [pltpu.make_async_remote_copy(src_ref=f, dst_ref=o, send_sem=send_sems.at[i], recv_sem=recv_sems.at[i],
                                            device_id=(x, y, 1 - c), device_id_type=MESH)
               for i, (f, o) in enumerate(zip(f_refs, o_refs))]
        for cp in cps:
            cp.start()
        for cp in cps:
            cp.wait()

    return _comm_call(body, name, [jax.ShapeDtypeStruct(f.shape, f.dtype) for f in fs], n, *fs)


SUM_ROWS = 256


def _tile_rows(h, cap):
    tb = cap
    while tb > 8 and h % tb:
        tb //= 2
    assert h % tb == 0, (h, tb)
    return tb


def sum_blocks(a, name):
    n, h, lanes = a.shape
    tb = _tile_rows(h, SUM_ROWS)

    def body(a_ref, o_ref):
        acc = a_ref[0]
        for k in range(1, n):
            acc = acc + a_ref[k]
        o_ref[...] = acc

    return pl.pallas_call(
        body, name=name, grid=(h // tb,), in_specs=[pl.BlockSpec((n, tb, lanes), lambda i: (0, i, 0))],
        out_specs=pl.BlockSpec((tb, lanes), lambda i: (i, 0)), out_shape=jax.ShapeDtypeStruct((h, lanes), a.dtype),
        compiler_params=_cparams(("parallel",)),
    )(a)


def add2(a, b, name):
    r, lanes = a.shape
    tb = _tile_rows(r, SUM_ROWS)

    def body(a_ref, b_ref, o_ref):
        o_ref[...] = a_ref[...] + b_ref[...]

    spec = pl.BlockSpec((tb, lanes), lambda i: (i, 0))
    return pl.pallas_call(
        body, name=name, grid=(r // tb,), in_specs=[spec, spec], out_specs=spec,
        out_shape=jax.ShapeDtypeStruct((r, lanes), a.dtype), compiler_params=_cparams(("parallel",)),
    )(a, b)


def pair_reduce(gs, tag):
    c = lax.axis_index("c")
    got = pair_swap(gs, f"pair_swap_{tag}")
    sums = []
    for i, (g, r) in enumerate(zip(gs, got)):
        n, _, h, lanes = g.shape
        mine = lax.dynamic_index_in_dim(g, c, axis=1, keepdims=False)
        sums.append(add2(mine.reshape(n * h, lanes), r.reshape(n * h, lanes), f"pair_sum_{tag}_{i}").reshape(n, h, lanes))
    return sums


def chip_reduce(sums, ys, tag):
    me = 2 * lax.axis_index("x") + lax.axis_index("y")
    fs = []
    for i, (s, y) in enumerate(zip(sums, ys)):
        own = lax.dynamic_index_in_dim(s, me, axis=0, keepdims=True)
        fs.append(sum_blocks(lax.dynamic_update_slice_in_dim(y, own, me, axis=0), f"chip_sum_{tag}_{i}"))
    return fs


def pair_finish(fs, tag):
    c = lax.axis_index("c")
    others = pair_join(fs, f"pair_join_{tag}")
    return [jnp.concatenate([jnp.where(c == 0, f, o), jnp.where(c == 0, o, f)]) for f, o in zip(fs, others)]


SHARDED = {
    "ffn1_w_gate": 2, "ffn1_w_up": 2, "ffn1_w_down": 1, "ffn2_w_gate": 2, "ffn2_w_up": 2, "ffn2_w_down": 1,
    "hyb_w_in": 2, "gdn_conv": 2, "pool_w": 2, "hyb_w_out": 1, "mla_w_in": 1, "mla_q_norm": 1, "mla_kv_norm": 1,
    "mla_w_q_up": 2, "mla_w_kv_up": 2, "mla_w_out": 1,
}
EXACT = ("gdn_conv", "mla_q_norm", "mla_kv_norm")
EVEN_ONLY = ("hyb_w_in", "gdn_conv", "gdn_a_log", "gdn_dt_bias", "gdn_out_norm", "pool_w", "pool_scale", "hyb_w_out")
WEIGHTS = ["ffn1_norm", "ffn1_w_gate", "ffn1_w_up", "ffn1_w_down", "mix_norm", "ffn2_norm", "ffn2_w_gate",
           "ffn2_w_up", "ffn2_w_down", "hyb_w_in", "gdn_conv", "gdn_a_log", "gdn_dt_bias", "gdn_out_norm", "pool_w",
           "pool_scale", "hyb_w_out", "mla_w_in", "mla_q_norm", "mla_kv_norm", "mla_w_q_up", "mla_w_kv_up",
           "mla_q_head_norm", "mla_k_head_norm", "mla_w_out"]


def _pad_rows(flat, mult):
    n = flat.shape[0]
    rows = -(-n // LANES)
    rows = -(-rows // mult) * mult
    return jnp.pad(flat, (0, rows * LANES - n)), rows


NATURAL = "hyb_w_in"


def regroup_w_in(shards, layer):
    _, _, rows, width = shards.shape
    tb = _row_block(rows, EVEN_IN)
    cut = 4 * GDN_W
    nab = 2 * GDN_HEADS

    def body(s_ref, qkvz_ref, ab_ref, u_ref, full):
        for k in range(N_CHIPS):
            full[:, k * width:(k + 1) * width] = s_ref[k]
        qkvz_ref[...] = full[:, :cut]
        ab_ref[...] = jnp.zeros_like(ab_ref)
        ab_ref[:, :nab] = full[:, cut:cut + nab]
        u_ref[...] = full[:, cut + nab:]

    return pl.pallas_call(
        body, name=f"regroup_w_in_{layer}", grid=(rows // tb,),
        in_specs=[pl.BlockSpec((N_CHIPS, None, tb, width), lambda i: (0, layer, i, 0))],
        out_specs=[pl.BlockSpec((tb, cut), lambda i: (i, 0)), pl.BlockSpec((tb, HEAD_W), lambda i: (i, 0)),
                   pl.BlockSpec((tb, POOL_W), lambda i: (i, 0))],
        out_shape=[jax.ShapeDtypeStruct((rows, cut), shards.dtype), jax.ShapeDtypeStruct((rows, HEAD_W), shards.dtype),
                   jax.ShapeDtypeStruct((rows, POOL_W), shards.dtype)],
        scratch_shapes=[pltpu.VMEM((tb, N_CHIPS * width), shards.dtype)],
        compiler_params=_cparams(("parallel",)),
    )(shards)


def regroup_dw_in(dw_qkvz, dw_ab, dw_u):
    rows = dw_qkvz.shape[0]
    width = EVEN_IN // N_CHIPS
    tb = _row_block(rows, EVEN_IN)
    cut = 4 * GDN_W
    nab = 2 * GDN_HEADS

    def body(qkvz_ref, ab_ref, u_ref, o_ref, full):
        full[:, :cut] = qkvz_ref[...]
        full[:, cut:cut + nab] = ab_ref[:, :nab]
        full[:, cut + nab:] = u_ref[...]
        for k in range(N_CHIPS):
            o_ref[k] = full[:, k * width:(k + 1) * width]

    return pl.pallas_call(
        body, name="regroup_dw_in", grid=(rows // tb,),
        in_specs=[pl.BlockSpec((tb, cut), lambda i: (i, 0)), pl.BlockSpec((tb, HEAD_W), lambda i: (i, 0)),
                  pl.BlockSpec((tb, POOL_W), lambda i: (i, 0))],
        out_specs=pl.BlockSpec((N_CHIPS, tb, width), lambda i: (0, i, 0)),
        out_shape=jax.ShapeDtypeStruct((N_CHIPS, rows, width), F32),
        scratch_shapes=[pltpu.VMEM((tb, EVEN_IN), F32)],
        compiler_params=_cparams(("parallel",)),
    )(dw_qkvz, dw_ab, dw_u)


def gather_weights(w):
    parts = []
    for name in SHARDED:
        if name == NATURAL:
            continue
        a = w[name]
        parts.append(lax.bitcast_convert_type(a, BF).reshape(-1) if name in EXACT else a.astype(BF).reshape(-1))
    flat, rows = _pad_rows(jnp.concatenate(parts), 2 * 16 * AG_CHUNKS)
    mine = [flat.reshape(2, rows // 2, LANES), w[NATURAL].astype(BF)]
    me = 2 * lax.axis_index("x") + lax.axis_index("y")
    got = [lax.dynamic_update_slice_in_dim(g, m[None], me, axis=0) for g, m in zip(all_gather_chips(mine), mine)]
    full = {NATURAL: got[1]}
    got = got[0].reshape(N_CHIPS, rows * LANES)
    off = 0
    for name, axis in SHARDED.items():
        if name == NATURAL:
            continue
        a = w[name]
        n = a.size * (2 if name in EXACT else 1)
        seg = got[:, off:off + n]
        off += n
        if name in EXACT:
            seg = lax.bitcast_convert_type(seg.reshape((N_CHIPS,) + a.shape + (2,)), F32)
        else:
            seg = seg.reshape((N_CHIPS,) + a.shape)
        seg = jnp.moveaxis(seg, 0, axis)
        full[name] = seg.reshape(a.shape[:axis] + (N_CHIPS * a.shape[axis],) + a.shape[axis + 1:])
    return full


OWN_OPERAND = 256 * 1024
EXCHANGE_UNIT = 2 * 8 * LANES


def shard_major(g, axis):
    size = g.shape[axis] // N_CHIPS
    return jnp.moveaxis(g.reshape(g.shape[:axis] + (N_CHIPS, size) + g.shape[axis + 1:]), axis, 0)


def _exchange_operand(flat):
    n = flat.shape[1]
    padded = -(-n // EXCHANGE_UNIT) * EXCHANGE_UNIT
    return jnp.pad(flat, ((0, 0), (0, padded - n))).reshape(N_CHIPS, 2, padded // (2 * LANES), LANES)


def is_own_operand(name, g):
    return name in SHARDED and g.size // N_CHIPS >= OWN_OPERAND


def grad_operand(name, g):
    if name == NATURAL:
        n4, rows, width = g.shape
        return g.reshape(n4, 2, rows // 2, width), rows * width, g.shape[1:]
    flat = g.reshape(N_CHIPS, -1)
    return _exchange_operand(flat), flat.shape[1], g.shape[1:]


def from_operand(name, r, n, shape):
    return r if name == NATURAL else r.reshape(-1)[:n].reshape(shape)


def misc_operand(small, loss_tile):
    flats, layout = [], []
    for name, layer, g in small + [("loss", 0, loss_tile)]:
        if name in SHARDED:
            flat, shape = g.reshape(N_CHIPS, -1), g.shape[1:]
        else:
            flat, shape = jnp.broadcast_to(g.reshape(1, -1), (N_CHIPS, g.size)), g.shape
        flats.append(flat)
        layout.append((name, layer, flat.shape[1], shape))
    return _exchange_operand(jnp.concatenate(flats, axis=1)), layout


def _as2d(a):
    return a.reshape(-1, a.shape[-1])


def kernel(x, positions, ffn1_norm, ffn1_w_gate, ffn1_w_up, ffn1_w_down, mix_norm, ffn2_norm, ffn2_w_gate, ffn2_w_up, ffn2_w_down, hyb_w_in, gdn_conv, gdn_a_log, gdn_dt_bias, gdn_out_norm, pool_w, pool_scale, hyb_w_out, mla_w_in, mla_q_norm, mla_kv_norm, mla_w_q_up, mla_w_kv_up, mla_q_head_norm, mla_k_head_norm, mla_w_out, loss_target, m_ffn1_norm, m_ffn1_w_gate, m_ffn1_w_up, m_ffn1_w_down, m_mix_norm, m_ffn2_norm, m_ffn2_w_gate, m_ffn2_w_up, m_ffn2_w_down, m_hyb_w_in, m_gdn_conv, m_gdn_a_log, m_gdn_dt_bias, m_gdn_out_norm, m_pool_w, m_pool_scale, m_hyb_w_out, m_mla_w_in, m_mla_q_norm, m_mla_kv_norm, m_mla_w_q_up, m_mla_w_kv_up, m_mla_q_head_norm, m_mla_k_head_norm, m_mla_w_out, v_ffn1_norm, v_ffn1_w_gate, v_ffn1_w_up, v_ffn1_w_down, v_mix_norm, v_ffn2_norm, v_ffn2_w_gate, v_ffn2_w_up, v_ffn2_w_down, v_hyb_w_in, v_gdn_conv, v_gdn_a_log, v_gdn_dt_bias, v_gdn_out_norm, v_pool_w, v_pool_scale, v_hyb_w_out, v_mla_w_in, v_mla_q_norm, v_mla_kv_norm, v_mla_w_q_up, v_mla_w_kv_up, v_mla_q_head_norm, v_mla_k_head_norm, v_mla_w_out):
    given = dict(locals())
    w = {n: given[n] for n in WEIGHTS}
    moments_m = {n: given["m_" + n] for n in WEIGHTS}
    moments_v = {n: given["v_" + n] for n in WEIGHTS}
    t = x.shape[1]
    xs = x.reshape(t, D_MODEL)
    full = gather_weights(w)
    n_even = hyb_w_in.shape[0]
    n_odd = mla_w_in.shape[0]

    _, sign, inv_freq = _rope_consts()
    cos, sin = rope_tables(positions.reshape(t, 1), inv_freq, sign)

    def ffn_args(which, layer):
        return (w[f"{which}_norm"][layer][None], full[f"{which}_w_gate"][layer], full[f"{which}_w_up"][layer],
                full[f"{which}_w_down"][layer])

    w_in_groups = [regroup_w_in(full[NATURAL], i) for i in range(n_even)]

    def hyb_args(i):
        w_qkvz, w_ab, w_u = w_in_groups[i]
        return (w["mix_norm"][2 * i][None], w_qkvz, w_ab, w_u,
                jnp.pad(full["gdn_conv"][i], ((0, HALO - CONV_K), (0, 0))), jnp.repeat(w["gdn_a_log"][i], HEAD_W)[None],
                jnp.repeat(w["gdn_dt_bias"][i], HEAD_W)[None], w["gdn_out_norm"][i][None], full["pool_w"][i],
                w["pool_scale"][i][None], full["hyb_w_out"][i])

    def mla_args(i):
        w_in = jnp.pad(full["mla_w_in"][i], ((0, 0), (0, ODD_IN_PAD - ODD_IN)))
        w_q = jnp.pad(full["mla_w_q_up"][i].reshape(LORA, MLA_HEADS, QK_HEAD),
                      ((0, 0), (0, 0), (0, 2 * HEAD_W - QK_HEAD))).reshape(LORA, MLA_HEADS * 2 * HEAD_W)
        return (cos, sin, w["mix_norm"][2 * i + 1][None], w_in, full["mla_q_norm"][i][None], full["mla_kv_norm"][i][None],
                w_q, full["mla_w_kv_up"][i], w["mla_q_head_norm"][i], w["mla_k_head_norm"][i], full["mla_w_out"][i])

    saved = []
    h = xs
    for layer in range(DEPTH):
        i = layer // 2
        h, s1 = ffn_fwd(h, *ffn_args("ffn1", layer), f"l{layer}_ffn1")
        if layer % 2 == 0:
            h, s2 = hyb_fwd(h, *hyb_args(i), f"l{layer}_hyb")
        else:
            h, s2 = mla_fwd(h, *mla_args(i), f"l{layer}_mla")
        h, s3 = ffn_fwd(h, *ffn_args("ffn2", layer), f"l{layer}_ffn2")
        saved.append((s1, s2, s3))

    dh, loss_tile = loss_head(h, loss_target.reshape(t, D_MODEL))

    small, done, halves = [], [], []

    def begin(group, tag):
        recs, ops = [], []
        for name, idx, g in group:
            if name in SHARDED and not (name.endswith(("w_gate", "w_up")) or name == NATURAL):
                g = shard_major(g, SHARDED[name] - 1)
            if is_own_operand(name, g):
                op, n, shape = grad_operand(name, g)
                recs.append((name, idx, n, shape))
                ops.append(op)
            else:
                small.append((name, idx, g))
        return recs, pair_reduce(ops, tag)

    def finish(recs, sums, ys, tag):
        done.extend(recs)
        halves.extend(chip_reduce(sums, ys, tag))

    waiting = ([], [])
    for layer in reversed(range(DEPTH)):
        i = layer // 2
        s1, s2, s3 = saved[layer]
        dh, dg, dwg, dwu, dwd = ffn_bwd(dh, s3, *ffn_args("ffn2", layer), f"l{layer}_ffn2")
        recs, sums = begin([("ffn2_norm", layer, dg[0]), ("ffn2_w_gate", layer, dwg), ("ffn2_w_up", layer, dwu),
                            ("ffn2_w_down", layer, dwd)], f"l{layer}a")
        recs, sums = waiting[0] + recs, waiting[1] + sums
        if layer % 2 == 0:
            dh, g, ys = hyb_bwd(dh, s2, *hyb_args(i), f"l{layer}_hyb", carry=sums)
            dmix, dw_qkvz, dw_ab, dw_u, dconv, da_log, ddt, dog, dpw, dps, dwo = g
            group = [("hyb_w_in", i, regroup_dw_in(dw_qkvz, dw_ab, dw_u)), ("gdn_conv", i, dconv[:CONV_K]),
                     ("gdn_a_log", i, da_log.reshape(GDN_HEADS, HEAD_W).sum(axis=1)),
                     ("gdn_dt_bias", i, ddt.reshape(GDN_HEADS, HEAD_W).sum(axis=1)), ("gdn_out_norm", i, dog[0]),
                     ("pool_w", i, dpw), ("pool_scale", i, dps[0]), ("hyb_w_out", i, dwo)]
        else:
            dh, g, ys = mla_bwd(dh, s2, *mla_args(i), f"l{layer}_mla", carry=sums)
            dmix, dw_in, dqg, dkvg, dwq, dwkv, dqh, dkh, dwo = g
            dwq = dwq.reshape(LORA, MLA_HEADS, 2 * HEAD_W)[:, :, :QK_HEAD].reshape(LORA, -1)
            group = [("mla_w_in", i, dw_in[:, :ODD_IN]), ("mla_q_norm", i, dqg[0]), ("mla_kv_norm", i, dkvg[0]),
                     ("mla_w_q_up", i, dwq), ("mla_w_kv_up", i, dwkv), ("mla_q_head_norm", i, dqh),
                     ("mla_k_head_norm", i, dkh), ("mla_w_out", i, dwo)]
        finish(recs, sums, ys, f"l{layer}")
        dh, dg, dwg, dwu, dwd = ffn_bwd(dh, s1, *ffn_args("ffn1", layer), f"l{layer}_ffn1")
        group += [("mix_norm", layer, dmix[0]), ("ffn1_norm", layer, dg[0]), ("ffn1_w_gate", layer, dwg),
                  ("ffn1_w_up", layer, dwu), ("ffn1_w_down", layer, dwd)]
        waiting = begin(group, f"l{layer}b")

    misc, layout = misc_operand(small, loss_tile)
    recs, sums = waiting[0] + [("misc", 0, 0, None)], waiting[1] + pair_reduce([misc], "misc")
    finish(recs, sums, chip_scatter(sums, "chip_scatter_tail"), "tail")
    wholes = pair_finish(halves, "grads")

    n_layers = {n: DEPTH if n.startswith(("ffn", "mix")) else (n_even if n in EVEN_ONLY else n_odd) for n in WEIGHTS}
    per_layer = {n: [None] * n_layers[n] for n in WEIGHTS}
    loss = None
    for (name, idx, n, shape), r in zip(done, wholes):
        if name != "misc":
            per_layer[name][idx] = from_operand(name, r, n, shape)
            continue
        flat, off = r.reshape(-1), 0
        for small_name, small_idx, size, small_shape in layout:
            piece = flat[off:off + size].reshape(small_shape)
            off += size
            if small_name == "loss":
                loss = piece[0, 0]
            else:
                per_layer[small_name][small_idx] = piece
    grads = {n: jnp.stack(per_layer[n]) for n in WEIGHTS}

    deltas, new_m, new_v = {}, {}, {}
    for n in WEIGHTS:
        d2, m2, v2 = adamw(_as2d(w[n]), _as2d(grads[n]), _as2d(moments_m[n]), _as2d(moments_v[n]), f"adamw_{n}")
        deltas[n], new_m[n], new_v[n] = d2.reshape(w[n].shape), m2.reshape(w[n].shape), v2.reshape(w[n].shape)
    return (loss, dh.reshape(x.shape), *[grads[n] for n in WEIGHTS], *[deltas[n] for n in WEIGHTS],
            *[new_m[n] for n in WEIGHTS], *[new_v[n] for n in WEIGHTS])
```

```python
import functools

import jax
import jax.numpy as jnp
import numpy as np
from jax import lax
from jax.experimental import pallas as pl
from jax.experimental.pallas import tpu as pltpu

F32 = jnp.float32
BF = jnp.bfloat16
HI = lax.Precision.HIGHEST
MESH = pl.DeviceIdType.MESH

D_MODEL = 2048
DEPTH = 4
GDN_HEADS = 8
HEAD_W = 128
GDN_W = GDN_HEADS * HEAD_W
CONV_K = 4
CHUNK = 64
POOL_WINDOWS = (2, 4, 8, 16)
POOL_W = 1024
POOL_GROUP_W = 256
EVEN_IN = 5136
MLA_HEADS = 16
LORA = 512
ROPE = 64
QK_HEAD = HEAD_W + ROPE
ODD_IN = 2 * LORA + ROPE
ODD_IN_PAD = 2 * LORA + HEAD_W
ROPE_THETA = 10000.0
EPS = 1e-6
N_CHIPS = 4

ADAM_LR = 0.001
ADAM_B1 = 0.9
ADAM_B2 = 0.999
ADAM_EPS = 1e-08
ADAM_WD = 0.01
ADAM_STEP = 10

ROW_BLOCK = 256
ROW_BLOCK_ELEMS = 256 * 1024
MM_TILE = 1024
MM_TILE_K = 2048
VMEM_LIMIT = 56 * 1024 * 1024


def _cparams(sem=None):
    return pltpu.CompilerParams(dimension_semantics=sem, vmem_limit_bytes=VMEM_LIMIT)


def _bdot(a, b, ca, cb):
    return lax.dot_general(a.astype(BF), b.astype(BF), (((ca,), (cb,)), ((), ())),
                           preferred_element_type=F32)


@jax.custom_vjp
def mm_nn(a, b):
    return _bdot(a, b, 1, 0)


@jax.custom_vjp
def mm_nt(a, b):
    return _bdot(a, b, 1, 1)


@jax.custom_vjp
def mm_tn(a, b):
    return _bdot(a, b, 0, 0)


mm_nn.defvjp(lambda a, b: (mm_nn(a, b), (a, b)), lambda r, g: (mm_nt(g, r[1]), mm_tn(r[0], g)))
mm_nt.defvjp(lambda a, b: (mm_nt(a, b), (a, b)), lambda r, g: (mm_nn(g, r[1]), mm_tn(g, r[0])))
mm_tn.defvjp(lambda a, b: (mm_tn(a, b), (a, b)), lambda r, g: (mm_nt(r[1], g), mm_nn(r[0], g)))


def hdot(a, b):
    return lax.dot_general(a, b, (((1,), (0,)), ((), ())), precision=HI, preferred_element_type=F32)


def _sigmoid(x):
    return 1.0 / (1.0 + jnp.exp(-x))


def _silu(x):
    return x * _sigmoid(x)


def _softplus(x):
    return jnp.maximum(x, 0.0) + jnp.log(1.0 + jnp.exp(-jnp.abs(x)))


def _tile(dim, cap):
    if dim <= cap:
        return dim
    t = (cap // 128) * 128
    while t >= 128:
        if dim % t == 0:
            return t
        t -= 128
    raise ValueError(f"no tile for {dim}")


def _as_tuple(r):
    return tuple(r) if isinstance(r, (tuple, list)) else (r,)


def _row_block(t, cols=None):
    rows = ROW_BLOCK if cols is None else max(ROW_BLOCK, ROW_BLOCK_ELEMS // cols)
    rows = min(rows, t)
    assert t % rows == 0, (t, rows)
    return rows


def matmul(a, b, mode, *, name, alpha=1.0, add=None, out_dtype=F32, out_shards=1):
    if mode == "nn":
        (m, k), (k2, n) = a.shape, b.shape
    elif mode == "nt":
        (m, k), (n, k2) = a.shape, b.shape
    else:
        (k, m), (k2, n) = a.shape, b.shape
    assert k == k2, (a.shape, b.shape, mode)
    tm, tn, tk = _tile(m, MM_TILE), _tile(n, MM_TILE), _tile(k, MM_TILE_K)
    nk = k // tk
    ca = 0 if mode == "tn" else 1
    cb = 1 if mode == "nt" else 0
    a_spec = (pl.BlockSpec((tk, tm), lambda i, j, kk: (kk, i)) if mode == "tn"
              else pl.BlockSpec((tm, tk), lambda i, j, kk: (i, kk)))
    b_spec = (pl.BlockSpec((tn, tk), lambda i, j, kk: (j, kk)) if mode == "nt"
              else pl.BlockSpec((tk, tn), lambda i, j, kk: (kk, j)))
    o_spec = pl.BlockSpec((tm, tn), lambda i, j, kk: (i, j))
    has_add = add is not None

    def body(*refs):
        if has_add:
            a_ref, b_ref, add_ref, o_ref, acc_ref = refs
        else:
            a_ref, b_ref, o_ref, acc_ref = refs
        kk = pl.program_id(2)

        @pl.when(kk == 0)
        def _():
            acc_ref[...] = jnp.zeros_like(acc_ref)

        acc_ref[...] += _bdot(a_ref[...], b_ref[...], ca, cb)

        @pl.when(kk == nk - 1)
        def _():
            r = acc_ref[...]
            if alpha != 1.0:
                r = r * alpha
            if has_add:
                r = r + add_ref[...].astype(F32)
            o_ref[...] = r.astype(out_dtype)

    in_specs = [a_spec, b_spec] + ([o_spec] if has_add else [])
    args = (a, b) + ((add,) if has_add else ())
    out_spec, out_shape = o_spec, (m, n)
    if out_shards > 1:
        assert not has_add and (n // out_shards) % tn == 0
        per = n // out_shards // tn
        out_spec = pl.BlockSpec((None, tm, tn), lambda i, j, kk: (j // per, i, j % per))
        out_shape = (out_shards, m, n // out_shards)
    return pl.pallas_call(
        body, name=name, grid=(m // tm, n // tn, nk), in_specs=in_specs, out_specs=out_spec,
        out_shape=jax.ShapeDtypeStruct(out_shape, out_dtype),
        scratch_shapes=[pltpu.VMEM((tm, tn), F32)],
        compiler_params=_cparams(("parallel", "parallel", "arbitrary")),
    )(*args)


def gate_up(h, wg, wu, *, name, tile_n=MM_TILE // 2):
    (m, k), (_, n) = h.shape, wg.shape
    tm, tn, tk = _tile(m, MM_TILE), _tile(n, tile_n), _tile(k, MM_TILE_K)
    nk = k // tk

    def body(h_ref, wg_ref, wu_ref, g_ref, u_ref, a_ref, g_acc, u_acc):
        kk = pl.program_id(2)

        @pl.when(kk == 0)
        def _():
            g_acc[...] = jnp.zeros_like(g_acc)
            u_acc[...] = jnp.zeros_like(u_acc)

        hv = h_ref[...]
        g_acc[...] += _bdot(hv, wg_ref[...], 1, 0)
        u_acc[...] += _bdot(hv, wu_ref[...], 1, 0)

        @pl.when(kk == nk - 1)
        def _():
            g_ref[...] = g_acc[...]
            u_ref[...] = u_acc[...]
            a_ref[...] = _swiglu_act(g_acc[...], u_acc[...]).astype(a_ref.dtype)

    w_spec = pl.BlockSpec((tk, tn), lambda i, j, kk: (kk, j))
    o_spec = pl.BlockSpec((tm, tn), lambda i, j, kk: (i, j))
    return pl.pallas_call(
        body, name=name, grid=(m // tm, n // tn, nk),
        in_specs=[pl.BlockSpec((tm, tk), lambda i, j, kk: (i, kk)), w_spec, w_spec],
        out_specs=[o_spec] * 3,
        out_shape=[jax.ShapeDtypeStruct((m, n), F32), jax.ShapeDtypeStruct((m, n), F32),
                   jax.ShapeDtypeStruct((m, n), BF)],
        scratch_shapes=[pltpu.VMEM((tm, tn), F32), pltpu.VMEM((tm, tn), F32)],
        compiler_params=_cparams(("parallel", "parallel", "arbitrary")),
    )(h, wg, wu)


def matmul_nt_epilogue(a, b, extras, epilogue, out_dtypes, *, name, alpha=1.0, tile_n=MM_TILE // 2):
    (m, k), (n, k2) = a.shape, b.shape
    assert k == k2
    tm, tn, tk = _tile(m, MM_TILE), _tile(n, tile_n), _tile(k, MM_TILE_K)
    nk = k // tk
    ne, no = len(extras), len(out_dtypes)

    def body(*refs):
        a_ref, b_ref = refs[:2]
        e_refs, o_refs, acc_ref = refs[2:2 + ne], refs[2 + ne:2 + ne + no], refs[2 + ne + no]
        kk = pl.program_id(2)

        @pl.when(kk == 0)
        def _():
            acc_ref[...] = jnp.zeros_like(acc_ref)

        acc_ref[...] += _bdot(a_ref[...], b_ref[...], 1, 1)

        @pl.when(kk == nk - 1)
        def _():
            res = epilogue(acc_ref[...] * alpha, *[e[...].astype(F32) for e in e_refs])
            for o_ref, r in zip(o_refs, res):
                o_ref[...] = r.astype(o_ref.dtype)

    o_spec = pl.BlockSpec((tm, tn), lambda i, j, kk: (i, j))
    return pl.pallas_call(
        body, name=name, grid=(m // tm, n // tn, nk),
        in_specs=[pl.BlockSpec((tm, tk), lambda i, j, kk: (i, kk)), pl.BlockSpec((tn, tk), lambda i, j, kk: (j, kk))]
        + [o_spec] * ne,
        out_specs=[o_spec] * no, out_shape=[jax.ShapeDtypeStruct((m, n), dt) for dt in out_dtypes],
        scratch_shapes=[pltpu.VMEM((tm, tn), F32)],
        compiler_params=_cparams(("parallel", "parallel", "arbitrary")),
    )(a, b, *extras)


def _row_spec(tb, bc, cf):
    return pl.BlockSpec((tb, bc), lambda i, j, cf=cf: (i, cf(j)))


def _par_spec(p):
    return pl.BlockSpec(p.shape, lambda i, j: (0, 0))


def rowwise(f, rows, pars, outs, *, ncol=1, name):
    t = rows[0][0].shape[0]
    tb = _row_block(t, max([bc for _, bc, _ in rows] + [bc for _, bc, _, _ in outs]))
    nr = len(rows)

    def body(*refs):
        vals = [r[...].astype(F32) for r in refs[:nr + len(pars)]]
        res = _as_tuple(f(*vals))
        for o_ref, r in zip(refs[nr + len(pars):], res):
            o_ref[...] = r.astype(o_ref.dtype)

    return pl.pallas_call(
        body, name=name, grid=(t // tb, ncol),
        in_specs=[_row_spec(tb, bc, cf) for _, bc, cf in rows] + [_par_spec(p) for p in pars],
        out_specs=[_row_spec(tb, bc, cf) for _, bc, cf, _ in outs],
        out_shape=[jax.ShapeDtypeStruct((t, tc), dt) for tc, _, _, dt in outs],
        compiler_params=_cparams(("parallel", "arbitrary")),
    )(*[r[0] for r in rows], *pars)


def rowwise_vjp(f, rows, pars, cts, row_grads, *, ncol=1, name, par_grads=True, consts=(), grad_dtype=F32):
    t = rows[0][0].shape[0]
    tb = _row_block(t, max([bc for _, bc, _ in rows + cts] + [g[1] for g in row_grads if g is not None]))
    consts = list(consts)
    nr, npar, nct, ncon = len(rows), len(pars), len(cts), len(consts)
    diff_rows = [i for i, g in enumerate(row_grads) if g is not None]
    adds = [row_grads[i][3] for i in diff_rows]
    add_idx = [i for i, a in enumerate(adds) if a is not None]

    def body(*refs):
        pos = 0
        row_refs = refs[pos:pos + nr]; pos += nr
        par_refs = refs[pos:pos + npar]; pos += npar
        con_refs = refs[pos:pos + ncon]; pos += ncon
        ct_refs = refs[pos:pos + nct]; pos += nct
        add_refs = refs[pos:pos + len(add_idx)]; pos += len(add_idx)
        grow_refs = refs[pos:pos + len(diff_rows)]; pos += len(diff_rows)
        gpar_refs = refs[pos:]
        row_vals = [r[...].astype(F32) for r in row_refs]
        par_vals = [r[...].astype(F32) for r in par_refs]
        con_vals = [r[...].astype(F32) for r in con_refs]

        def g(*dvals):
            rv = list(row_vals)
            for i, v in zip(diff_rows, dvals[:len(diff_rows)]):
                rv[i] = v
            pv = dvals[len(diff_rows):] if par_grads else par_vals
            return _as_tuple(f(*rv, *pv, *con_vals))

        prim = [row_vals[i] for i in diff_rows] + (par_vals if par_grads else [])
        _, pull = jax.vjp(g, *prim)
        grads = pull(tuple(c[...].astype(F32) for c in ct_refs))
        for n, ref in enumerate(grow_refs):
            gr = grads[n]
            if n in add_idx:
                gr = gr + add_refs[add_idx.index(n)][...]
            ref[...] = gr.astype(ref.dtype)
        if par_grads:
            first = jnp.logical_and(pl.program_id(0) == 0, pl.program_id(1) == 0)
            for ref, gr in zip(gpar_refs, grads[len(diff_rows):]):
                @pl.when(first)
                def _(ref=ref):
                    ref[...] = jnp.zeros_like(ref)
                ref[...] += gr

    gspecs = [row_grads[i] for i in diff_rows]
    in_specs = ([_row_spec(tb, bc, cf) for _, bc, cf in rows] + [_par_spec(p) for p in pars + consts]
                + [_row_spec(tb, bc, cf) for _, bc, cf in cts]
                + [_row_spec(tb, gspecs[i][1], gspecs[i][2]) for i in add_idx])
    out_specs = [_row_spec(tb, bc, cf) for _, bc, cf, _ in gspecs]
    out_shape = [jax.ShapeDtypeStruct((t, tc), grad_dtype) for tc, _, _, _ in gspecs]
    if par_grads:
        out_specs += [_par_spec(p) for p in pars]
        out_shape += [jax.ShapeDtypeStruct(p.shape, F32) for p in pars]
    res = pl.pallas_call(
        body, name=name, grid=(t // tb, ncol), in_specs=in_specs, out_specs=out_specs,
        out_shape=out_shape, compiler_params=_cparams(("arbitrary", "arbitrary")),
    )(*[r[0] for r in rows], *pars, *consts, *[c[0] for c in cts], *[adds[i] for i in add_idx])
    return list(res[:len(diff_rows)]), list(res[len(diff_rows):])


def _col0(j):
    return 0


def _colj(j):
    return j


def _full(a):
    return (a, a.shape[1], _col0)


def _rms(x, gain):
    return x * lax.rsqrt(jnp.mean(x * x, axis=-1, keepdims=True) + EPS) * gain


def _swiglu_act(g, u):
    return _silu(g) * u


def ffn_fwd(x, gain, wg, wu, wd, tag):
    d = x.shape[1]
    (h,) = rowwise(_rms, [_full(x)], [gain], [(d, d, _col0, BF)], name=f"{tag}_norm")
    g, u, a = gate_up(h, wg, wu, name=f"{tag}_gate_up")
    y = matmul(a, wd, "nn", alpha=0.5, add=x, name=f"{tag}_down")
    return y, (x, h, g, u, a)


def ffn_bwd(dy, saved, gain, wg, wu, wd, tag):
    x, h, g, u, a = saved
    d, f = x.shape[1], g.shape[1]
    def act_bwd(da, gv, uv):
        return jax.vjp(_swiglu_act, gv, uv)[1](da)

    dg, du = matmul_nt_epilogue(dy, wd, [g, u], act_bwd, [BF, BF], alpha=0.5, name=f"{tag}_down_dx")
    dwd = matmul(a, dy, "tn", alpha=0.5, name=f"{tag}_down_dw")
    dwg = matmul(h, dg, "tn", name=f"{tag}_gate_dw", out_shards=N_CHIPS)
    dwu = matmul(h, du, "tn", name=f"{tag}_up_dw", out_shards=N_CHIPS)
    dh = matmul(dg, wg, "nt", name=f"{tag}_gate_dx")
    dh = matmul(du, wu, "nt", add=dh, name=f"{tag}_up_dx")
    (dx,), (dgain,) = rowwise_vjp(_rms, [_full(x)], [gain], [_full(dh)], [(d, d, _col0, dy)],
                                  name=f"{tag}_norm_bwd")
    return dx, dgain, dwg, dwu, dwd


def rope_tables(positions, inv_freq, sign):
    t = positions.shape[0]
    tb = _row_block(t)

    def body(pos_ref, f_ref, s_ref, c_ref, sn_ref):
        ang = pos_ref[...].astype(F32) * f_ref[...]
        live = jnp.abs(s_ref[...])
        c_ref[...] = jnp.cos(ang) * live
        sn_ref[...] = jnp.sin(ang) * s_ref[...]

    return pl.pallas_call(
        body, name="rope_tables", grid=(t // tb,),
        in_specs=[pl.BlockSpec((tb, 1), lambda i: (i, 0)), pl.BlockSpec((1, HEAD_W), lambda i: (0, 0)),
                  pl.BlockSpec((1, HEAD_W), lambda i: (0, 0))],
        out_specs=[pl.BlockSpec((tb, HEAD_W), lambda i: (i, 0))] * 2,
        out_shape=[jax.ShapeDtypeStruct((t, HEAD_W), F32)] * 2,
        compiler_params=_cparams(("parallel",)),
    )(positions, inv_freq, sign)


def _rope(p, c, s, swap):
    return p * c + hdot(p, swap) * s


def _pe_norm(pe, gp):
    return pe * lax.rsqrt(jnp.sum(pe * pe, axis=-1, keepdims=True) * (1.0 / ROPE) + EPS) * gp


def _q_head(nope, pe, c, s, gn, gp, swap):
    return _rms(nope, gn), _rope(_pe_norm(pe, gp), c, s, swap)


def _k_head(nope, v, gn):
    return _rms(nope, gn), v


def _kpe_head(pe, c, s, gp, swap):
    return _rope(_pe_norm(pe, gp), c, s, swap)


ATT_BLOCK = 1024
ATT_SCALE = QK_HEAD ** -0.5
NEG = float(np.finfo(np.float32).min)


def _att_block(t):
    return min(ATT_BLOCK, t)


def _scores(qn, qp, kn, kp, diag):
    q = jnp.concatenate([qn, qp], axis=1)
    k = jnp.concatenate([kn, kp], axis=1)
    s = lax.dot_general(q, k, (((1,), (1,)), ((), ())), preferred_element_type=F32) * ATT_SCALE
    if diag:
        rows = lax.broadcasted_iota(jnp.int32, s.shape, 0)
        cols = lax.broadcasted_iota(jnp.int32, s.shape, 1)
        s = jnp.where(rows >= cols, s, NEG)
    return s, q, k


def _below_or_on_diagonal(i, j, step):
    @pl.when(j < i)
    def _():
        step(False)

    @pl.when(j == i)
    def _():
        step(True)


def attention_fwd(qn, qp, kn, kp, v):
    t = qn.shape[0]
    h = qn.shape[1] // HEAD_W
    tq = _att_block(t)
    nq = t // tq

    def body(qn_ref, qp_ref, kn_ref, kp_ref, v_ref, o_ref, lse_ref, m_ref, l_ref, acc_ref):
        i, j = pl.program_id(1), pl.program_id(2)

        @pl.when(j == 0)
        def _():
            m_ref[...] = jnp.full_like(m_ref, NEG)
            l_ref[...] = jnp.zeros_like(l_ref)
            acc_ref[...] = jnp.zeros_like(acc_ref)

        def step(diag):
            s, _, _ = _scores(qn_ref[...], qp_ref[...], kn_ref[...], kp_ref[...], diag)
            m_new = jnp.maximum(m_ref[...], jnp.max(s, axis=-1, keepdims=True))
            a = jnp.exp(m_ref[...] - m_new)
            p = jnp.exp(s - m_new)
            l_ref[...] = a * l_ref[...] + jnp.sum(p, axis=-1, keepdims=True)
            acc_ref[...] = a * acc_ref[...] + jnp.dot(p.astype(BF), v_ref[...], preferred_element_type=F32)
            m_ref[...] = m_new

        _below_or_on_diagonal(i, j, step)

        @pl.when(j == nq - 1)
        def _():
            o_ref[...] = acc_ref[...] / l_ref[...]
            lse_ref[...] = m_ref[...] + jnp.log(l_ref[...])

    qspec = pl.BlockSpec((tq, HEAD_W), lambda hh, i, j: (i, hh))
    kspec = pl.BlockSpec((tq, HEAD_W), lambda hh, i, j: (jnp.minimum(i, j), hh))
    kpspec = pl.BlockSpec((tq, HEAD_W), lambda hh, i, j: (jnp.minimum(i, j), 0))
    return pl.pallas_call(
        body, name="attention_fwd", grid=(h, nq, nq),
        in_specs=[qspec, qspec, kspec, kpspec, kspec],
        out_specs=[qspec, pl.BlockSpec((None, tq, 1), lambda hh, i, j: (hh, i, 0))],
        out_shape=[jax.ShapeDtypeStruct((t, h * HEAD_W), F32), jax.ShapeDtypeStruct((h, t, 1), F32)],
        scratch_shapes=[pltpu.VMEM((tq, 1), F32), pltpu.VMEM((tq, 1), F32), pltpu.VMEM((tq, HEAD_W), F32)],
        compiler_params=_cparams(("parallel", "parallel", "arbitrary")),
    )(qn, qp, kn, kp, v)


def attention_delta(o, do):
    t = o.shape[0]
    h = o.shape[1] // HEAD_W
    tq = _att_block(t)

    def body(o_ref, do_ref, d_ref):
        d_ref[...] = jnp.sum(o_ref[...] * do_ref[...], axis=-1, keepdims=True)

    spec = pl.BlockSpec((tq, HEAD_W), lambda hh, i: (i, hh))
    return pl.pallas_call(
        body, name="attention_delta", grid=(h, t // tq), in_specs=[spec, spec],
        out_specs=pl.BlockSpec((None, tq, 1), lambda hh, i: (hh, i, 0)),
        out_shape=jax.ShapeDtypeStruct((h, t, 1), F32),
        compiler_params=_cparams(("parallel", "parallel")),
    )(o, do)


def _att_grads(qn_ref, qp_ref, kn_ref, kp_ref, v_ref, do_ref, lse_ref, dl_ref, diag):
    s, q, k = _scores(qn_ref[...], qp_ref[...], kn_ref[...], kp_ref[...], diag)
    p = jnp.exp(s - lse_ref[...])
    do = do_ref[...].astype(BF)
    dp = lax.dot_general(do, v_ref[...], (((1,), (1,)), ((), ())), preferred_element_type=F32)
    ds = p * (dp - dl_ref[...]) * ATT_SCALE
    return p, ds, q, k, do


def attention_bwd(qn, qp, kn, kp, v, do, lse, delta, name, carry=()):
    t = qn.shape[0]
    h = qn.shape[1] // HEAD_W
    tq = _att_block(t)
    nq = t // tq

    def body(qn_ref, qp_ref, kn_ref, kp_ref, v_ref, do_ref, lse_ref, dl_ref,
             dqn_ref, dqp_ref, dkn_ref, dkp_ref, dv_ref, dq_acc, dk_acc, dv_acc):
        j, i = pl.program_id(1), pl.program_id(2)
        rows = pl.ds(pl.multiple_of(i * tq, tq), tq)

        @pl.when(i == 0)
        def _():
            dk_acc[...] = jnp.zeros_like(dk_acc)
            dv_acc[...] = jnp.zeros_like(dv_acc)

        @pl.when(j == 0)
        def _():
            dq_acc[rows, :] = jnp.zeros((tq, 2 * HEAD_W), F32)

        def step(diag):
            p, ds, q, k, dov = _att_grads(qn_ref, qp_ref, kn_ref, kp_ref, v_ref, do_ref, lse_ref, dl_ref, diag)
            dsb = ds.astype(BF)
            dv_acc[...] += lax.dot_general(p.astype(BF), dov, (((0,), (0,)), ((), ())), preferred_element_type=F32)
            dk_acc[...] += lax.dot_general(dsb, q, (((0,), (0,)), ((), ())), preferred_element_type=F32)
            dq_acc[rows, :] += jnp.dot(dsb, k, preferred_element_type=F32)

        _below_or_on_diagonal(i, j, step)

        @pl.when(i == nq - 1)
        def _():
            dkn_ref[...] = dk_acc[:, :HEAD_W]
            dkp_ref[...] = dk_acc[:, HEAD_W:]
            dv_ref[...] = dv_acc[...]

        @pl.when(j == nq - 1)
        def _():
            dqn_ref[...] = dq_acc[rows, :HEAD_W]
            dqp_ref[...] = dq_acc[rows, HEAD_W:]

    qspec = pl.BlockSpec((tq, HEAD_W), lambda hh, j, i: (jnp.maximum(i, j), hh))
    kspec = pl.BlockSpec((tq, HEAD_W), lambda hh, j, i: (j, hh))
    kpspec = pl.BlockSpec((tq, HEAD_W), lambda hh, j, i: (j, 0))
    vec = pl.BlockSpec((None, tq, 1), lambda hh, j, i: (hh, jnp.maximum(i, j), 0))
    dqspec = pl.BlockSpec((tq, HEAD_W), lambda hh, j, i: (jnp.where(j == nq - 1, i, 0), hh))
    wide = jax.ShapeDtypeStruct((t, h * HEAD_W), F32)
    call = dict(
        name=name, grid=(h, nq, nq),
        in_specs=[qspec, qspec, kspec, kpspec, kspec, qspec, vec, vec],
        out_specs=[dqspec, dqspec, kspec, kspec, kspec],
        out_shape=[wide] * 5,
        scratch_shapes=[pltpu.VMEM((t, 2 * HEAD_W), F32), pltpu.VMEM((tq, 2 * HEAD_W), F32),
                        pltpu.VMEM((tq, HEAD_W), F32)],
    )
    args = (qn, qp, kn, kp, v, do, lse, delta)
    if carry:
        res = carried_call(body, carry, args=args, **call)
        return res[:5], res[5:]
    return pl.pallas_call(body, compiler_params=_cparams(("arbitrary",) * 3), **call)(*args), []


def _sum_slabs(*slabs):
    return functools.reduce(lambda a, b: a + b, slabs)


def _even(j):
    return 2 * j


def _odd(j):
    return 2 * j + 1


def _rope_consts():
    lane = np.arange(HEAD_W)
    half = ROPE // 2
    swap = np.zeros((HEAD_W, HEAD_W), np.float32)
    swap[lane[:half] + half, lane[:half]] = 1.0
    swap[lane[:half], lane[:half] + half] = 1.0
    sign = np.where(lane < half, -1.0, np.where(lane < ROPE, 1.0, 0.0)).astype(np.float32)[None]
    inv_freq = ROPE_THETA ** (-jnp.arange(0, ROPE, 2, dtype=F32) / ROPE)
    inv_freq = jnp.concatenate([inv_freq, inv_freq, jnp.zeros((HEAD_W - ROPE,), F32)])[None]
    return jnp.asarray(swap), jnp.asarray(sign), inv_freq


def _pad_gain(g):
    return g[None, :HEAD_W], jnp.pad(g[HEAD_W:], (0, HEAD_W - ROPE))[None]


def mla_fwd(x, cos, sin, mix_gain, w_in, q_gain, kv_gain, w_q, w_kv, qh_gain, kh_gain, w_out, tag):
    d = x.shape[1]
    nh = MLA_HEADS
    swap = _rope_consts()[0]
    (h,) = rowwise(_rms, [_full(x)], [mix_gain], [(d, d, _col0, BF)], name=f"{tag}_norm")
    proj = matmul(h, w_in, "nn", name=f"{tag}_in")
    (qlat,) = rowwise(_rms, [(proj, LORA, _col0)], [q_gain], [(LORA, LORA, _col0, BF)], name=f"{tag}_qnorm")
    (kvlat,) = rowwise(_rms, [(proj, LORA, lambda j: 1)], [kv_gain], [(LORA, LORA, _col0, BF)],
                       name=f"{tag}_kvnorm")
    q = matmul(qlat, w_q, "nn", name=f"{tag}_qup")
    kv = matmul(kvlat, w_kv, "nn", name=f"{tag}_kvup")
    qgn, qgp = _pad_gain(qh_gain)
    kgn, kgp = _pad_gain(kh_gain)
    w = nh * HEAD_W
    qn, qp = rowwise(_q_head, [(q, HEAD_W, _even), (q, HEAD_W, _odd), _full(cos), _full(sin)], [qgn, qgp, swap],
                     [(w, HEAD_W, _colj, BF), (w, HEAD_W, _colj, BF)], ncol=nh, name=f"{tag}_qhead")
    kn, v = rowwise(_k_head, [(kv, HEAD_W, _even), (kv, HEAD_W, _odd)], [kgn],
                    [(w, HEAD_W, _colj, BF), (w, HEAD_W, _colj, BF)], ncol=nh, name=f"{tag}_khead")
    (kp,) = rowwise(_kpe_head, [(proj, HEAD_W, lambda j: 2 * LORA // HEAD_W), _full(cos), _full(sin)], [kgp, swap],
                    [(HEAD_W, HEAD_W, _col0, BF)], name=f"{tag}_kpe")
    o, lse = attention_fwd(qn, qp, kn, kp, v)
    y = matmul(o, w_out, "nn", add=x, name=f"{tag}_out")
    return y, (x, h, proj, qlat, kvlat, q, kv, qn, qp, kn, kp, v, o, lse)


def mla_bwd(dy, saved, cos, sin, mix_gain, w_in, q_gain, kv_gain, w_q, w_kv, qh_gain, kh_gain, w_out, tag, carry=()):
    x, h, proj, qlat, kvlat, q, kv, qn, qp, kn, kp, v, o, lse = saved
    d = x.shape[1]
    nh = MLA_HEADS
    w = nh * HEAD_W
    swap = _rope_consts()[0]
    qgn, qgp = _pad_gain(qh_gain)
    kgn, kgp = _pad_gain(kh_gain)
    do = matmul(dy, w_out, "nt", name=f"{tag}_out_dx")
    dw_out = matmul(o, dy, "tn", name=f"{tag}_out_dw")
    delta = attention_delta(o, do)
    (dqn, dqp, dkn, dkp_heads, dv), carried = attention_bwd(qn, qp, kn, kp, v, do, lse, delta,
                                                            f"{tag}_attention_bwd", carry)
    (dkp,) = rowwise(_sum_slabs, [(dkp_heads, HEAD_W, lambda j, hh=hh: hh) for hh in range(nh)], [],
                     [(HEAD_W, HEAD_W, _col0, F32)], name=f"{tag}_kpe_sum")
    (dq_a, dq_b), (dqgn, dqgp) = rowwise_vjp(
        _q_head, [(q, HEAD_W, _even), (q, HEAD_W, _odd), _full(cos), _full(sin)], [qgn, qgp],
        [(dqn, HEAD_W, _colj), (dqp, HEAD_W, _colj)],
        [(w, HEAD_W, _colj, None), (w, HEAD_W, _colj, None), None, None], ncol=nh, consts=[swap],
        name=f"{tag}_qhead_bwd")
    dq = _interleave(dq_a, dq_b)
    (dkv_a, dkv_b), (dkgn,) = rowwise_vjp(
        _k_head, [(kv, HEAD_W, _even), (kv, HEAD_W, _odd)], [kgn], [(dkn, HEAD_W, _colj), (dv, HEAD_W, _colj)],
        [(w, HEAD_W, _colj, None), (w, HEAD_W, _colj, None)], ncol=nh, name=f"{tag}_khead_bwd")
    dkv = _interleave(dkv_a, dkv_b)
    (dpe,), (dkgp,) = rowwise_vjp(
        _kpe_head, [(proj, HEAD_W, lambda j: 2 * LORA // HEAD_W), _full(cos), _full(sin)], [kgp], [_full(dkp)],
        [(HEAD_W, HEAD_W, _col0, None), None, None], consts=[swap], name=f"{tag}_kpe_bwd")
    dw_q = matmul(qlat, dq, "tn", name=f"{tag}_qup_dw")
    dw_kv = matmul(kvlat, dkv, "tn", name=f"{tag}_kvup_dw")
    dqlat = matmul(dq, w_q, "nt", name=f"{tag}_qup_dx")
    dkvlat = matmul(dkv, w_kv, "nt", name=f"{tag}_kvup_dx")
    (dpq,), (dq_gain,) = rowwise_vjp(_rms, [(proj, LORA, _col0)], [q_gain], [_full(dqlat)],
                                     [(LORA, LORA, _col0, None)], name=f"{tag}_qnorm_bwd")
    (dpkv,), (dkv_gain,) = rowwise_vjp(_rms, [(proj, LORA, lambda j: 1)], [kv_gain], [_full(dkvlat)],
                                       [(LORA, LORA, _col0, None)], name=f"{tag}_kvnorm_bwd")
    dproj = jnp.concatenate([dpq, dpkv, dpe], axis=1)
    dw_in = matmul(h, dproj, "tn", name=f"{tag}_in_dw")
    dh = matmul(dproj, w_in, "nt", name=f"{tag}_in_dx")
    (dx,), (dmix,) = rowwise_vjp(_rms, [_full(x)], [mix_gain], [_full(dh)], [(d, d, _col0, dy)],
                                 name=f"{tag}_norm_bwd")
    dqh = jnp.concatenate([dqgn[0], dqgp[0, :ROPE]])
    dkh = jnp.concatenate([dkgn[0], dkgp[0, :ROPE]])
    return dx, (dmix, dw_in, dq_gain, dkv_gain, dw_q, dw_kv, dqh, dkh, dw_out), carried


def _interleave(a, b):
    t, w = a.shape
    n = w // HEAD_W
    return jnp.stack([a.reshape(t, n, HEAD_W), b.reshape(t, n, HEAD_W)], axis=2).reshape(t, 2 * w)


GDN_HEADS_PER_STEP = 8
CONV_COLS = 512
HALO = 8


def conv_fwd(x, w):
    t = x.shape[0]
    c = w.shape[1]
    tb = _row_block(t)
    assert t % tb == 0

    def body(x_ref, prev_ref, w_ref, y_ref):
        i = pl.program_id(1)
        xv = x_ref[...]
        prev = jnp.where(i > 0, prev_ref[...], 0.0)
        ext = jnp.concatenate([prev, xv], axis=0)
        acc = xv * w_ref[CONV_K - 1:CONV_K, :]
        for s in range(1, CONV_K):
            acc = acc + pltpu.roll(ext, s, 0)[HALO:] * w_ref[CONV_K - 1 - s:CONV_K - s, :]
        y_ref[...] = acc

    spec = pl.BlockSpec((tb, CONV_COLS), lambda j, i: (i, j))
    return pl.pallas_call(
        body, name="conv_fwd", grid=(c // CONV_COLS, t // tb),
        in_specs=[spec, pl.BlockSpec((HALO, CONV_COLS), lambda j, i: (jnp.maximum(i * (tb // HALO) - 1, 0), j)),
                  pl.BlockSpec((HALO, CONV_COLS), lambda j, i: (0, j))],
        out_specs=spec, out_shape=jax.ShapeDtypeStruct((t, c), F32),
        compiler_params=_cparams(("parallel", "parallel")),
    )(x, x, w)


def conv_bwd(x, dy, w):
    t = x.shape[0]
    c = w.shape[1]
    tb = _row_block(t)
    nrb = t // tb

    def body(x_ref, prev_ref, dy_ref, next_ref, w_ref, dx_ref, dw_ref):
        i = pl.program_id(1)
        ext_x = jnp.concatenate([jnp.where(i > 0, prev_ref[...], 0.0), x_ref[...]], axis=0)
        dyv = dy_ref[...]
        ext_dy = jnp.concatenate([dyv, jnp.where(i < nrb - 1, next_ref[...], 0.0)], axis=0)

        @pl.when(i == 0)
        def _():
            dw_ref[...] = jnp.zeros_like(dw_ref)

        acc = dyv * w_ref[CONV_K - 1:CONV_K, :]
        dw_ref[CONV_K - 1:CONV_K, :] += jnp.sum(dyv * x_ref[...], axis=0, keepdims=True)
        for s in range(1, CONV_K):
            acc = acc + pltpu.roll(ext_dy, tb + HALO - s, 0)[:tb] * w_ref[CONV_K - 1 - s:CONV_K - s, :]
            dw_ref[CONV_K - 1 - s:CONV_K - s, :] += jnp.sum(dyv * pltpu.roll(ext_x, s, 0)[HALO:], axis=0, keepdims=True)
        dx_ref[...] = acc

    spec = pl.BlockSpec((tb, CONV_COLS), lambda j, i: (i, j))
    wspec = pl.BlockSpec((HALO, CONV_COLS), lambda j, i: (0, j))
    return pl.pallas_call(
        body, name="conv_bwd", grid=(c // CONV_COLS, nrb),
        in_specs=[spec, pl.BlockSpec((HALO, CONV_COLS), lambda j, i: (jnp.maximum(i * (tb // HALO) - 1, 0), j)),
                  spec, pl.BlockSpec((HALO, CONV_COLS), lambda j, i: (jnp.minimum(i + 1, nrb - 1) * (tb // HALO), j)),
                  wspec],
        out_specs=[spec, wspec],
        out_shape=[jax.ShapeDtypeStruct((t, c), F32), jax.ShapeDtypeStruct((HALO, c), F32)],
        compiler_params=_cparams(("parallel", "arbitrary")),
    )(x, x, dy, dy, w)


def _l2_silu(c):
    a = _silu(c)
    return a * lax.rsqrt(jnp.sum(a * a, axis=-1, keepdims=True) + EPS)


def _gates(ab, a_log, dt_bias, ea, eb):
    g = -jnp.exp(a_log) * _softplus(hdot(ab, ea) + dt_bias)
    return g, _sigmoid(hdot(ab, eb))


def _gate_consts():
    ea = np.zeros((HEAD_W, GDN_W), np.float32)
    eb = np.zeros((HEAD_W, GDN_W), np.float32)
    for h in range(GDN_HEADS):
        ea[h, h * HEAD_W:(h + 1) * HEAD_W] = 1.0
        eb[GDN_HEADS + h, h * HEAD_W:(h + 1) * HEAD_W] = 1.0
    return jnp.asarray(ea), jnp.asarray(eb)


def _batched_dots(precision, to_bf16):
    def raw(a, b, ca, cb):
        if to_bf16:
            a, b = a.astype(BF), b.astype(BF)
        return lax.dot_general(a, b, (((ca,), (cb,)), ((0,), (0,))), precision=precision,
                               preferred_element_type=F32)

    nn = jax.custom_vjp(lambda a, b: raw(a, b, 2, 1))
    nt = jax.custom_vjp(lambda a, b: raw(a, b, 2, 2))
    tn = jax.custom_vjp(lambda a, b: raw(a, b, 1, 1))
    nn.defvjp(lambda a, b: (nn(a, b), (a, b)), lambda r, g: (nt(g, r[1]), tn(r[0], g)))
    nt.defvjp(lambda a, b: (nt(a, b), (a, b)), lambda r, g: (nn(g, r[1]), tn(g, r[0])))
    tn.defvjp(lambda a, b: (tn(a, b), (a, b)), lambda r, g: (nt(r[1], g), nn(r[0], g)))
    return nn, nt, tn


bmm_nn, bmm_nt, bmm_tn = _batched_dots(None, True)
bh_nn, bh_nt, bh_tn = _batched_dots(HI, False)


def _gdn_chunk(q, k, v, g, beta, state):
    n = CHUNK
    nb = q.shape[0]
    ii = lax.broadcasted_iota(jnp.int32, (nb, n, n), 1)
    jj = lax.broadcasted_iota(jnp.int32, (nb, n, n), 2)
    causal = ii >= jj
    eye = (ii == jj).astype(F32)
    gc = bh_nn(causal.astype(F32), g)
    mean_w = jnp.full((nb, n, HEAD_W), 1.0 / HEAD_W, F32)
    gc_i = bh_nt(gc, mean_w)
    gc_j = bh_nt(mean_w, gc)
    decay = jnp.exp(jnp.where(causal, gc_i - gc_j, -1e30))
    kb = k * beta
    vb = v * beta
    m = jnp.where(ii > jj, bmm_nt(kb, k) * decay, 0.0)
    inv = eye - m
    pw = m
    for _ in range(5):
        pw = bh_nn(pw, pw)
        inv = bh_nn(inv, eye + pw)
    eg = jnp.exp(gc)
    u = bh_nn(inv, vb)
    w = bh_nn(inv, kb * eg)
    qs = q * (HEAD_W ** -0.5)
    attn = bmm_nt(qs, k) * decay
    g_last = bh_nn((jj == n - 1).astype(F32), gc)
    k_dec = k * jnp.exp(g_last - gc)
    v_new = u - bmm_nn(w, state)
    out = bmm_nn(qs * eg, state) + bmm_nn(attn, v_new)
    new_state = state * jnp.exp(jnp.concatenate([g_last, g_last], axis=1)) + bmm_tn(k_dec, v_new)
    return out, new_state


def _heads(ref, hb):
    return jnp.stack([ref[:, j * HEAD_W:(j + 1) * HEAD_W] for j in range(hb)])


def gdn_fwd(qk, v, g, beta):
    t = v.shape[0]
    n = t // CHUNK
    nh = GDN_HEADS

    hb = GDN_HEADS_PER_STEP
    ng = nh // hb

    def body(q_ref, k_ref, v_ref, g_ref, b_ref, o_ref, s_ref, state):
        @pl.when(pl.program_id(1) == 0)
        def _():
            state[...] = jnp.zeros_like(state)

        old = state[...]
        out, new = _gdn_chunk(*[_heads(r, hb) for r in (q_ref, k_ref, v_ref, g_ref, b_ref)], old)
        s_ref[:, 0] = old
        for j in range(hb):
            o_ref[:, j * HEAD_W:(j + 1) * HEAD_W] = out[j]
        state[...] = new

    spec = pl.BlockSpec((CHUNK, hb * HEAD_W), lambda h, c: (c, h))
    return pl.pallas_call(
        body, name="gdn_fwd", grid=(ng, n),
        in_specs=[spec, pl.BlockSpec((CHUNK, hb * HEAD_W), lambda h, c: (c, ng + h)), spec, spec, spec],
        out_specs=[spec, pl.BlockSpec((hb, 1, HEAD_W, HEAD_W), lambda h, c: (h, c, 0, 0))],
        out_shape=[jax.ShapeDtypeStruct((t, nh * HEAD_W), F32), jax.ShapeDtypeStruct((nh, n, HEAD_W, HEAD_W), F32)],
        scratch_shapes=[pltpu.VMEM((hb, HEAD_W, HEAD_W), F32)],
        compiler_params=_cparams(("parallel", "arbitrary")),
    )(qk, qk, v, g, beta)


def gdn_bwd(qk, v, g, beta, states, do, name, carry=()):
    t = v.shape[0]
    n = t // CHUNK
    nh = GDN_HEADS

    hb = GDN_HEADS_PER_STEP
    ng = nh // hb

    def body(q_ref, k_ref, v_ref, g_ref, b_ref, s_ref, do_ref, dq_ref, dk_ref, dv_ref, dg_ref, db_ref, dstate):
        @pl.when(pl.program_id(1) == 0)
        def _():
            dstate[...] = jnp.zeros_like(dstate)

        _, pull = jax.vjp(_gdn_chunk, *[_heads(r, hb) for r in (q_ref, k_ref, v_ref, g_ref, b_ref)], s_ref[:, 0])
        grads = pull((_heads(do_ref, hb), dstate[...]))
        for ref, gr in zip((dq_ref, dk_ref, dv_ref, dg_ref, db_ref), grads[:5]):
            for j in range(hb):
                ref[:, j * HEAD_W:(j + 1) * HEAD_W] = gr[j]
        dstate[...] = grads[5]

    spec = pl.BlockSpec((CHUNK, hb * HEAD_W), lambda h, c: (n - 1 - c, h))
    call = dict(
        name=name, grid=(ng, n),
        in_specs=[spec, pl.BlockSpec((CHUNK, hb * HEAD_W), lambda h, c: (n - 1 - c, ng + h)), spec, spec, spec,
                  pl.BlockSpec((hb, 1, HEAD_W, HEAD_W), lambda h, c: (h, n - 1 - c, 0, 0)), spec],
        out_specs=[spec] * 5,
        out_shape=[jax.ShapeDtypeStruct((t, nh * HEAD_W), F32)] * 5,
        scratch_shapes=[pltpu.VMEM((hb, HEAD_W, HEAD_W), F32)],
    )
    args = (qk, qk, v, g, beta, states, do)
    if carry:
        res = carried_call(body, carry, args=args, **call)
        return res[:5], res[5:]
    return pl.pallas_call(body, compiler_params=_cparams(("parallel", "arbitrary")), **call)(*args), []


def _gdn_post(o, z, gain):
    return _rms(o, gain) * _silu(z)


POOL_HALO = 16


def _pool_counts(t0, rows, cols):
    tt = t0 + lax.broadcasted_iota(jnp.int32, (rows, cols), 0) + 1
    grp = lax.broadcasted_iota(jnp.int32, (rows, cols), 1) // POOL_GROUP_W
    win = jnp.left_shift(2, grp)
    return jnp.minimum(tt, win).astype(F32), grp


def _by_group(grp, parts):
    out = parts[-1]
    for gi in range(len(parts) - 2, -1, -1):
        out = jnp.where(grp == gi, parts[gi], out)
    return out


def pool_window_fwd(u):
    t, c = u.shape
    tb = _row_block(t)

    def body(u_ref, prev_ref, d_ref):
        i = pl.program_id(0)
        xv = u_ref[...]
        ext = jnp.concatenate([jnp.where(i > 0, prev_ref[...], 0.0), xv], axis=0)
        sums = []
        s = ext
        for sh in (1, 2, 4, 8):
            s = s + pltpu.roll(s, sh, 0)
            sums.append(s[POOL_HALO:])
        cnt, grp = _pool_counts(i * tb, tb, c)
        d_ref[...] = _by_group(grp, sums) / cnt - xv

    spec = pl.BlockSpec((tb, c), lambda i: (i, 0))
    return pl.pallas_call(
        body, name="pool_window_fwd", grid=(t // tb,),
        in_specs=[spec, pl.BlockSpec((POOL_HALO, c), lambda i: (jnp.maximum(i * (tb // POOL_HALO) - 1, 0), 0))],
        out_specs=spec, out_shape=jax.ShapeDtypeStruct((t, c), F32),
        compiler_params=_cparams(("parallel",)),
    )(u, u)


def pool_window_bwd(dd):
    t, c = dd.shape
    tb = _row_block(t)
    nrb = t // tb
    length = tb + POOL_HALO

    def body(d_ref, next_ref, du_ref):
        i = pl.program_id(0)
        dv = d_ref[...]
        ext = jnp.concatenate([dv, jnp.where(i < nrb - 1, next_ref[...], 0.0)], axis=0)
        cnt, grp = _pool_counts(i * tb, length, c)
        s = ext / cnt
        sums = []
        for sh in (1, 2, 4, 8):
            s = s + pltpu.roll(s, length - sh, 0)
            sums.append(s[:tb])
        du_ref[...] = _by_group(grp[:tb], sums) - dv

    spec = pl.BlockSpec((tb, c), lambda i: (i, 0))
    return pl.pallas_call(
        body, name="pool_window_bwd", grid=(nrb,),
        in_specs=[spec, pl.BlockSpec((POOL_HALO, c), lambda i: (jnp.minimum(i + 1, nrb - 1) * (tb // POOL_HALO), 0))],
        out_specs=spec, out_shape=jax.ShapeDtypeStruct((t, c), F32),
        compiler_params=_cparams(("parallel",)),
    )(dd, dd)


def _pool_mix(d, w, scale):
    return mm_nn(d, w) * scale


def pool_mix_fwd(diff, w, scale):
    t = diff.shape[0]
    tb = _row_block(t)
    gw = POOL_GROUP_W

    def body(d_ref, w_ref, s_ref, o_ref):
        o_ref[...] = _pool_mix(d_ref[...], w_ref[...], s_ref[...]).astype(o_ref.dtype)

    spec = pl.BlockSpec((tb, gw), lambda i, g: (i, g))
    return pl.pallas_call(
        body, name="pool_mix_fwd", grid=(t // tb, POOL_W // gw),
        in_specs=[spec, pl.BlockSpec((None, gw, gw), lambda i, g: (g, 0, 0)), pl.BlockSpec((1, gw), lambda i, g: (0, g))],
        out_specs=spec, out_shape=jax.ShapeDtypeStruct((t, POOL_W), BF),
        compiler_params=_cparams(("parallel", "parallel")),
    )(diff, w, scale)


def pool_mix_bwd(diff, w, scale, dp, dp_col0):
    t = diff.shape[0]
    tb = _row_block(t)
    gw = POOL_GROUP_W

    def body(d_ref, w_ref, s_ref, dp_ref, dd_ref, dw_ref, ds_ref):
        @pl.when(pl.program_id(1) == 0)
        def _():
            dw_ref[...] = jnp.zeros_like(dw_ref)
            ds_ref[...] = jnp.zeros_like(ds_ref)

        _, pull = jax.vjp(_pool_mix, d_ref[...], w_ref[...].astype(F32), s_ref[...])
        dd, dw, ds = pull(dp_ref[...])
        dd_ref[...] = dd
        dw_ref[...] += dw
        ds_ref[...] += ds

    spec = pl.BlockSpec((tb, gw), lambda g, i: (i, g))
    wspec = pl.BlockSpec((None, gw, gw), lambda g, i: (g, 0, 0))
    sspec = pl.BlockSpec((1, gw), lambda g, i: (0, g))
    return pl.pallas_call(
        body, name="pool_mix_bwd", grid=(POOL_W // gw, t // tb),
        in_specs=[spec, wspec, sspec, pl.BlockSpec((tb, gw), lambda g, i: (i, dp_col0 + g))],
        out_specs=[spec, wspec, sspec],
        out_shape=[jax.ShapeDtypeStruct((t, POOL_W), F32), jax.ShapeDtypeStruct(w.shape, F32),
                   jax.ShapeDtypeStruct(scale.shape, F32)],
        compiler_params=_cparams(("parallel", "arbitrary")),
    )(diff, w, scale, dp)


def hyb_fwd(x, mix_gain, w_qkvz, w_ab, w_u, conv_w, a_log, dt_bias, out_gain, pool_w, pool_scale, w_out, tag):
    d = x.shape[1]
    ea, eb = _gate_consts()
    nh = GDN_HEADS
    (h,) = rowwise(_rms, [_full(x)], [mix_gain], [(d, d, _col0, BF)], name=f"{tag}_norm")
    p1 = matmul(h, w_qkvz, "nn", name=f"{tag}_in_qkvz")
    ab = matmul(h, w_ab, "nn", name=f"{tag}_in_ab")
    u = matmul(h, w_u, "nn", name=f"{tag}_in_u")
    cv = conv_fwd(p1, conv_w)
    (qk,) = rowwise(_l2_silu, [(cv, HEAD_W, _colj)], [], [(2 * GDN_W, HEAD_W, _colj, F32)], ncol=2 * nh,
                    name=f"{tag}_qk_act")
    (v,) = rowwise(_silu, [(cv, GDN_W, lambda j: 2)], [], [(GDN_W, GDN_W, _col0, F32)], name=f"{tag}_v_act")
    g, beta = rowwise(_gates, [_full(ab)], [a_log, dt_bias, ea, eb],
                      [(GDN_W, GDN_W, _col0, F32), (GDN_W, GDN_W, _col0, F32)], name=f"{tag}_gates")
    o, states = gdn_fwd(qk, v, g, beta)
    (on,) = rowwise(_gdn_post, [(o, HEAD_W, _colj), (p1, HEAD_W, lambda j: 3 * nh + j)], [out_gain],
                    [(GDN_W, HEAD_W, _colj, BF)], ncol=nh, name=f"{tag}_post")
    diff = pool_window_fwd(u)
    pm = pool_mix_fwd(diff, pool_w, pool_scale)
    cat = jnp.concatenate([on, pm], axis=1)
    y = matmul(cat, w_out, "nn", add=x, name=f"{tag}_out")
    return y, (x, h, p1, ab, cv, qk, v, g, beta, o, states, diff, cat)


def hyb_bwd(dy, saved, mix_gain, w_qkvz, w_ab, w_u, conv_w, a_log, dt_bias, out_gain, pool_w, pool_scale, w_out, tag,
            carry=()):
    x, h, p1, ab, cv, qk, v, g, beta, o, states, diff, cat = saved
    d = x.shape[1]
    nh = GDN_HEADS
    ea, eb = _gate_consts()
    dcat = matmul(dy, w_out, "nt", name=f"{tag}_out_dx")
    dw_out = matmul(cat, dy, "tn", name=f"{tag}_out_dw")
    (do, dz), (dout_gain,) = rowwise_vjp(
        _gdn_post, [(o, HEAD_W, _colj), (p1, HEAD_W, lambda j: 3 * nh + j)], [out_gain], [(dcat, HEAD_W, _colj)],
        [(GDN_W, HEAD_W, _colj, None), (GDN_W, HEAD_W, _colj, None)], ncol=nh, name=f"{tag}_post_bwd")
    ddiff, dpool_w, dpool_scale = pool_mix_bwd(diff, pool_w, pool_scale, dcat, GDN_W // POOL_GROUP_W)
    du = pool_window_bwd(ddiff)
    (dq, dk, dv, dg, dbeta), carried = gdn_bwd(qk, v, g, beta, states, do, f"{tag}_gdn_bwd", carry)
    (dab,), (da_log, ddt_bias) = rowwise_vjp(
        _gates, [_full(ab)], [a_log, dt_bias], [_full(dg), _full(dbeta)], [(HEAD_W, HEAD_W, _col0, None)],
        consts=[ea, eb], name=f"{tag}_gates_bwd")
    (dcq,), _ = rowwise_vjp(_l2_silu, [(cv, HEAD_W, _colj)], [], [(dq, HEAD_W, _colj)],
                            [(GDN_W, HEAD_W, _colj, None)], ncol=nh, par_grads=False, name=f"{tag}_q_act_bwd")
    (dck,), _ = rowwise_vjp(_l2_silu, [(cv, HEAD_W, lambda j: nh + j)], [], [(dk, HEAD_W, _colj)],
                            [(GDN_W, HEAD_W, _colj, None)], ncol=nh, par_grads=False, name=f"{tag}_k_act_bwd")
    (dcv,), _ = rowwise_vjp(_silu, [(cv, GDN_W, lambda j: 2)], [], [_full(dv)],
                            [(GDN_W, GDN_W, _col0, None)], par_grads=False, name=f"{tag}_v_act_bwd")
    dqkv, dconv_w = conv_bwd(p1, jnp.concatenate([dcq, dck, dcv], axis=1), conv_w)
    dp1 = jnp.concatenate([dqkv, dz], axis=1)
    dw_qkvz = matmul(h, dp1, "tn", name=f"{tag}_in_qkvz_dw")
    dw_ab = matmul(h, dab, "tn", name=f"{tag}_in_ab_dw")
    dw_u = matmul(h, du, "tn", name=f"{tag}_in_u_dw")
    dh = matmul(dp1, w_qkvz, "nt", name=f"{tag}_in_qkvz_dx")
    dh = matmul(dab, w_ab, "nt", add=dh, name=f"{tag}_in_ab_dx")
    dh = matmul(du, w_u, "nt", add=dh, name=f"{tag}_in_u_dx")
    (dx,), (dmix,) = rowwise_vjp(_rms, [_full(x)], [mix_gain], [_full(dh)], [(d, d, _col0, dy)],
                                 name=f"{tag}_norm_bwd")
    return dx, (dmix, dw_qkvz, dw_ab, dw_u, dconv_w, da_log, ddt_bias, dout_gain, dpool_w, dpool_scale, dw_out), carried


def loss_head(y, target):
    t, d = y.shape
    tb = _row_block(t)

    def body(y_ref, t_ref, dy_ref, loss_ref):
        @pl.when(pl.program_id(0) == 0)
        def _():
            loss_ref[...] = jnp.zeros_like(loss_ref)

        e = y_ref[...] - t_ref[...]
        dy_ref[...] = e * (1.0 / d)
        loss_ref[...] += 0.5 * jnp.sum(jnp.mean(e * e, axis=-1, keepdims=True))

    spec = pl.BlockSpec((tb, d), lambda i: (i, 0))
    return pl.pallas_call(
        body, name="loss_head", grid=(t // tb,), in_specs=[spec, spec],
        out_specs=[spec, pl.BlockSpec((8, 128), lambda i: (0, 0))],
        out_shape=[jax.ShapeDtypeStruct((t, d), F32), jax.ShapeDtypeStruct((8, 128), F32)],
        compiler_params=_cparams(("arbitrary",)),
    )(y, target)


ADAM_BLOCK_ELEMS = 256 * 1024


def _adam_rows(rows, cols):
    tb = 1024
    while tb >= 8:
        if rows % tb == 0 and tb * cols <= ADAM_BLOCK_ELEMS:
            return tb
        tb //= 2
    return rows


def adamw(w, g, m, v, name):
    rows, cols = w.shape
    tb = _adam_rows(rows, cols)
    c1 = 1.0 - ADAM_B1 ** ADAM_STEP
    c2 = 1.0 - ADAM_B2 ** ADAM_STEP

    def body(w_ref, g_ref, m_ref, v_ref, d_ref, nm_ref, nv_ref):
        gv = g_ref[...]
        nm = ADAM_B1 * m_ref[...] + (1.0 - ADAM_B1) * gv
        nv = ADAM_B2 * v_ref[...] + (1.0 - ADAM_B2) * (gv * gv)
        d_ref[...] = -ADAM_LR * ((nm / c1) / (jnp.sqrt(nv / c2) + ADAM_EPS) + ADAM_WD * w_ref[...])
        nm_ref[...] = nm
        nv_ref[...] = nv

    spec = pl.BlockSpec((tb, cols), lambda i: (i, 0))
    return pl.pallas_call(
        body, name=name, grid=(rows // tb,), in_specs=[spec] * 4, out_specs=[spec] * 3,
        out_shape=[jax.ShapeDtypeStruct((rows, cols), F32)] * 3,
        compiler_params=_cparams(("parallel",)),
    )(w, g, m, v)


LANES = 1024
HBM = pl.BlockSpec(memory_space=pltpu.HBM)


def _place():
    x, y, c = lax.axis_index("x"), lax.axis_index("y"), lax.axis_index("c")
    others = [(1 - x, y), (x, 1 - y), (1 - x, 1 - y)]
    return x, y, c, 2 * x + y, others


def _comm_call(body, name, out_shape, n_sems, *args):
    return pl.pallas_call(
        body, name=name, out_shape=out_shape, in_specs=[HBM] * len(args),
        out_specs=[HBM] * len(out_shape),
        scratch_shapes=[pltpu.SemaphoreType.DMA((n_sems,)), pltpu.SemaphoreType.DMA((n_sems,))],
        compiler_params=pltpu.CompilerParams(has_side_effects=True),
    )(*args)


AG_CHUNKS = 4


def all_gather_chips(xs):
    n = len(xs)
    for x in xs:
        assert x.shape[1] % (AG_CHUNKS * 16) == 0
    pieces = [(t, q) for t in range(n) for q in range(AG_CHUNKS)]
    base = 3 * len(pieces)

    def body(*refs):
        x_refs, out_refs, send_sems, recv_sems = refs[:n], refs[n:2 * n], refs[2 * n], refs[2 * n + 1]
        x, y, c, me, others = _place()
        sib = (x, y, 1 - c)

        def copy(k, src, dst, to):
            return pltpu.make_async_remote_copy(src_ref=src, dst_ref=dst, send_sem=send_sems.at[k],
                                                recv_sem=recv_sems.at[k], device_id=to, device_id_type=MESH)

        def rows(t, q):
            hc = x_refs[t].shape[1] // AG_CHUNKS
            return pl.ds(q * hc, hc)

        first, passed = [], []
        for p, (t, q) in enumerate(pieces):
            for k, chip in enumerate(others):
                cp = copy(p * 3 + k, x_refs[t].at[c, rows(t, q)], out_refs[t].at[me, c, rows(t, q)], (*chip, c))
                cp.start()
                first.append(cp)
        for p, (t, q) in enumerate(pieces):
            for k, (cx, cy) in enumerate(others):
                slot = out_refs[t].at[2 * cx + cy, c, rows(t, q)]
                copy(p * 3 + k, slot, slot, sib).wait_recv()
                fwd = copy(base + p * 3 + k, slot, slot, sib)
                fwd.start()
                passed.append(fwd)
        for p, (t, q) in enumerate(pieces):
            for k, (cx, cy) in enumerate(others):
                slot = out_refs[t].at[2 * cx + cy, 1 - c, rows(t, q)]
                copy(base + p * 3 + k, slot, slot, sib).wait_recv()
        for cp in first + passed:
            cp.wait_send()

    return _comm_call(body, "all_gather_chips", [jax.ShapeDtypeStruct((N_CHIPS,) + x.shape, x.dtype) for x in xs],
                      2 * base, *xs)


def pair_swap(gs, name):
    n = len(gs)

    def body(*refs):
        g_refs, r_refs, send_sems, recv_sems = refs[:n], refs[n:2 * n], refs[2 * n], refs[2 * n + 1]
        x, y, c, _, _ = _place()
        cps = [pltpu.make_async_remote_copy(src_ref=g.at[:, 1 - c], dst_ref=r, send_sem=send_sems.at[i],
                                            recv_sem=recv_sems.at[i], device_id=(x, y, 1 - c), device_id_type=MESH)
               for i, (g, r) in enumerate(zip(g_refs, r_refs))]
        for cp in cps:
            cp.start()
        for cp in cps:
            cp.wait()

    out = [jax.ShapeDtypeStruct((g.shape[0],) + g.shape[2:], g.dtype) for g in gs]
    return _comm_call(body, name, out, n, *gs)


def _chip_scatter_ops(s_refs, y_refs, send_sems, recv_sems):
    x, y, c, me, others = _place()

    def copies(to_peer):
        out = []
        for i, (s, dst) in enumerate(zip(s_refs, y_refs)):
            for k, (cx, cy) in enumerate(others):
                slot = dst.at[2 * cx + cy]
                out.append(pltpu.make_async_remote_copy(
                    src_ref=s.at[2 * cx + cy] if to_peer else slot, dst_ref=dst.at[me] if to_peer else slot,
                    send_sem=send_sems.at[3 * i + k], recv_sem=recv_sems.at[3 * i + k],
                    device_id=(cx, cy, c), device_id_type=MESH))
        return out

    def start():
        for cp in copies(True):
            cp.start()

    def wait():
        for cp in copies(False):
            cp.wait_recv()
        for cp in copies(True):
            cp.wait_send()

    return start, wait


def chip_scatter(ss, name):
    n = len(ss)

    def body(*refs):
        start, wait = _chip_scatter_ops(refs[:n], refs[n:2 * n], refs[2 * n], refs[2 * n + 1])
        start()
        wait()

    return _comm_call(body, name, [jax.ShapeDtypeStruct(s.shape, s.dtype) for s in ss], 3 * n, *ss)


def carried_call(body, carry, *, name, grid, in_specs, out_specs, out_shape, scratch_shapes, args):
    m = len(carry)
    n_in, n_out, n_scr = len(in_specs), len(out_specs), len(scratch_shapes)

    def wrapped(*refs):
        pos = [0]

        def take(k):
            part = refs[pos[0]:pos[0] + k]
            pos[0] += k
            return part

        ins, c_in, outs, c_out, scr, (send_sems, recv_sems) = take(n_in), take(m), take(n_out), take(m), take(n_scr), take(2)
        start, wait = _chip_scatter_ops(c_in, c_out, send_sems, recv_sems)
        ids = [pl.program_id(a) for a in range(len(grid))]
        first = functools.reduce(jnp.logical_and, [i == 0 for i in ids])
        last = functools.reduce(jnp.logical_and, [i == g - 1 for i, g in zip(ids, grid)])

        @pl.when(first)
        def _():
            start()

        body(*ins, *outs, *scr)

        @pl.when(last)
        def _():
            wait()

    return pl.pallas_call(
        wrapped, name=name, grid=grid, in_specs=list(in_specs) + [HBM] * m, out_specs=list(out_specs) + [HBM] * m,
        out_shape=list(out_shape) + [jax.ShapeDtypeStruct(s.shape, s.dtype) for s in carry],
        scratch_shapes=list(scratch_shapes) + [pltpu.SemaphoreType.DMA((3 * m,)), pltpu.SemaphoreType.DMA((3 * m,))],
        compiler_params=pltpu.CompilerParams(dimension_semantics=("arbitrary",) * len(grid),
                                             vmem_limit_bytes=VMEM_LIMIT, has_side_effects=True),
    )(*args, *carry)


def pair_join(fs, name):
    n = len(fs)

    def body(*refs):
        f_refs, o_refs, send_sems, recv_sems = refs[:n], refs[n:2 * n], refs[2 * n], refs[2 * n + 1]
        x, y, c, _, _ = _place()
        cps = [pltpu.make_async_remote_copy(src_ref=f, dst_ref=o, send_sem=send_sems.at[i], recv_sem=recv_sems.at[i],
                                            device_id=(x, y, 1 - c), device_id_type=MESH)
               for i, (f, o) in enumerate(zip(f_refs, o_refs))]
        for cp in cps:
            cp.start()
        for cp in cps:
            cp.wait()

    return _comm_call(body, name, [jax.ShapeDtypeStruct(f.shape, f.dtype) for f in fs], n, *fs)


SUM_ROWS = 256


def _tile_rows(h, cap):
    tb = cap
    while tb > 8 and h % tb:
        tb //= 2
    assert h % tb == 0, (h, tb)
    return tb


def sum_blocks(a, name):
    n, h, lanes = a.shape
    tb = _tile_rows(h, SUM_ROWS)

    def body(a_ref, o_ref):
        acc = a_ref[0]
        for k in range(1, n):
            acc = acc + a_ref[k]
        o_ref[...] = acc

    return pl.pallas_call(
        body, name=name, grid=(h // tb,), in_specs=[pl.BlockSpec((n, tb, lanes), lambda i: (0, i, 0))],
        out_specs=pl.BlockSpec((tb, lanes), lambda i: (i, 0)), out_shape=jax.ShapeDtypeStruct((h, lanes), a.dtype),
        compiler_params=_cparams(("parallel",)),
    )(a)


def add2(a, b, name):
    r, lanes = a.shape
    tb = _tile_rows(r, SUM_ROWS)

    def body(a_ref, b_ref, o_ref):
        o_ref[...] = a_ref[...] + b_ref[...]

    spec = pl.BlockSpec((tb, lanes), lambda i: (i, 0))
    return pl.pallas_call(
        body, name=name, grid=(r // tb,), in_specs=[spec, spec], out_specs=spec,
        out_shape=jax.ShapeDtypeStruct((r, lanes), a.dtype), compiler_params=_cparams(("parallel",)),
    )(a, b)


def pair_reduce(gs, tag):
    c = lax.axis_index("c")
    got = pair_swap(gs, f"pair_swap_{tag}")
    sums = []
    for i, (g, r) in enumerate(zip(gs, got)):
        n, _, h, lanes = g.shape
        mine = lax.dynamic_index_in_dim(g, c, axis=1, keepdims=False)
        sums.append(add2(mine.reshape(n * h, lanes), r.reshape(n * h, lanes), f"pair_sum_{tag}_{i}").reshape(n, h, lanes))
    return sums


def chip_reduce(sums, ys, tag):
    me = 2 * lax.axis_index("x") + lax.axis_index("y")
    fs = []
    for i, (s, y) in enumerate(zip(sums, ys)):
        own = lax.dynamic_index_in_dim(s, me, axis=0, keepdims=True)
        fs.append(sum_blocks(lax.dynamic_update_slice_in_dim(y, own, me, axis=0), f"chip_sum_{tag}_{i}"))
    return fs


def pair_finish(fs, tag):
    c = lax.axis_index("c")
    others = pair_join(fs, f"pair_join_{tag}")
    return [jnp.concatenate([jnp.where(c == 0, f, o), jnp.where(c == 0, o, f)]) for f, o in zip(fs, others)]


SHARDED = {
    "ffn1_w_gate": 2, "ffn1_w_up": 2, "ffn1_w_down": 1, "ffn2_w_gate": 2, "ffn2_w_up": 2, "ffn2_w_down": 1,
    "hyb_w_in": 2, "gdn_conv": 2, "pool_w": 2, "hyb_w_out": 1, "mla_w_in": 1, "mla_q_norm": 1, "mla_kv_norm": 1,
    "mla_w_q_up": 2, "mla_w_kv_up": 2, "mla_w_out": 1,
}
EXACT = ("gdn_conv", "mla_q_norm", "mla_kv_norm")
EVEN_ONLY = ("hyb_w_in", "gdn_conv", "gdn_a_log", "gdn_dt_bias", "gdn_out_norm", "pool_w", "pool_scale", "hyb_w_out")
WEIGHTS = ["ffn1_norm", "ffn1_w_gate", "ffn1_w_up", "ffn1_w_down", "mix_norm", "ffn2_norm", "ffn2_w_gate",
           "ffn2_w_up", "ffn2_w_down", "hyb_w_in", "gdn_conv", "gdn_a_log", "gdn_dt_bias", "gdn_out_norm", "pool_w",
           "pool_scale", "hyb_w_out", "mla_w_in", "mla_q_norm", "mla_kv_norm", "mla_w_q_up", "mla_w_kv_up",
           "mla_q_head_norm", "mla_k_head_norm", "mla_w_out"]


def _pad_rows(flat, mult):
    n = flat.shape[0]
    rows = -(-n // LANES)
    rows = -(-rows // mult) * mult
    return jnp.pad(flat, (0, rows * LANES - n)), rows


NATURAL = "hyb_w_in"


def regroup_w_in(shards, layer):
    _, _, rows, width = shards.shape
    tb = _row_block(rows, EVEN_IN)
    cut = 4 * GDN_W
    nab = 2 * GDN_HEADS

    def body(s_ref, qkvz_ref, ab_ref, u_ref, full):
        for k in range(N_CHIPS):
            full[:, k * width:(k + 1) * width] = s_ref[k]
        qkvz_ref[...] = full[:, :cut]
        ab_ref[...] = jnp.zeros_like(ab_ref)
        ab_ref[:, :nab] = full[:, cut:cut + nab]
        u_ref[...] = full[:, cut + nab:]

    return pl.pallas_call(
        body, name=f"regroup_w_in_{layer}", grid=(rows // tb,),
        in_specs=[pl.BlockSpec((N_CHIPS, None, tb, width), lambda i: (0, layer, i, 0))],
        out_specs=[pl.BlockSpec((tb, cut), lambda i: (i, 0)), pl.BlockSpec((tb, HEAD_W), lambda i: (i, 0)),
                   pl.BlockSpec((tb, POOL_W), lambda i: (i, 0))],
        out_shape=[jax.ShapeDtypeStruct((rows, cut), shards.dtype), jax.ShapeDtypeStruct((rows, HEAD_W), shards.dtype),
                   jax.ShapeDtypeStruct((rows, POOL_W), shards.dtype)],
        scratch_shapes=[pltpu.VMEM((tb, N_CHIPS * width), shards.dtype)],
        compiler_params=_cparams(("parallel",)),
    )(shards)


def regroup_dw_in(dw_qkvz, dw_ab, dw_u):
    rows = dw_qkvz.shape[0]
    width = EVEN_IN // N_CHIPS
    tb = _row_block(rows, EVEN_IN)
    cut = 4 * GDN_W
    nab = 2 * GDN_HEADS

    def body(qkvz_ref, ab_ref, u_ref, o_ref, full):
        full[:, :cut] = qkvz_ref[...]
        full[:, cut:cut + nab] = ab_ref[:, :nab]
        full[:, cut + nab:] = u_ref[...]
        for k in range(N_CHIPS):
            o_ref[k] = full[:, k * width:(k + 1) * width]

    return pl.pallas_call(
        body, name="regroup_dw_in", grid=(rows // tb,),
        in_specs=[pl.BlockSpec((tb, cut), lambda i: (i, 0)), pl.BlockSpec((tb, HEAD_W), lambda i: (i, 0)),
                  pl.BlockSpec((tb, POOL_W), lambda i: (i, 0))],
        out_specs=pl.BlockSpec((N_CHIPS, tb, width), lambda i: (0, i, 0)),
        out_shape=jax.ShapeDtypeStruct((N_CHIPS, rows, width), F32),
        scratch_shapes=[pltpu.VMEM((tb, EVEN_IN), F32)],
        compiler_params=_cparams(("parallel",)),
    )(dw_qkvz, dw_ab, dw_u)


def gather_weights(w):
    parts = []
    for name in SHARDED:
        if name == NATURAL:
            continue
        a = w[name]
        parts.append(lax.bitcast_convert_type(a, BF).reshape(-1) if name in EXACT else a.astype(BF).reshape(-1))
    flat, rows = _pad_rows(jnp.concatenate(parts), 2 * 16 * AG_CHUNKS)
    mine = [flat.reshape(2, rows // 2, LANES), w[NATURAL].astype(BF)]
    me = 2 * lax.axis_index("x") + lax.axis_index("y")
    got = [lax.dynamic_update_slice_in_dim(g, m[None], me, axis=0) for g, m in zip(all_gather_chips(mine), mine)]
    full = {NATURAL: got[1]}
    got = got[0].reshape(N_CHIPS, rows * LANES)
    off = 0
    for name, axis in SHARDED.items():
        if name == NATURAL:
            continue
        a = w[name]
        n = a.size * (2 if name in EXACT else 1)
        seg = got[:, off:off + n]
        off += n
        if name in EXACT:
            seg = lax.bitcast_convert_type(seg.reshape((N_CHIPS,) + a.shape + (2,)), F32)
        else:
            seg = seg.reshape((N_CHIPS,) + a.shape)
        seg = jnp.moveaxis(seg, 0, axis)
        full[name] = seg.reshape(a.shape[:axis] + (N_CHIPS * a.shape[axis],) + a.shape[axis + 1:])
    return full


OWN_OPERAND = 256 * 1024
EXCHANGE_UNIT = 2 * 8 * LANES


def shard_major(g, axis):
    size = g.shape[axis] // N_CHIPS
    return jnp.moveaxis(g.reshape(g.shape[:axis] + (N_CHIPS, size) + g.shape[axis + 1:]), axis, 0)


def _exchange_operand(flat):
    n = flat.shape[1]
    padded = -(-n // EXCHANGE_UNIT) * EXCHANGE_UNIT
    return jnp.pad(flat, ((0, 0), (0, padded - n))).reshape(N_CHIPS, 2, padded // (2 * LANES), LANES)


def is_own_operand(name, g):
    return name in SHARDED and g.size // N_CHIPS >= OWN_OPERAND


def grad_operand(name, g):
    if name == NATURAL:
        n4, rows, width = g.shape
        return g.reshape(n4, 2, rows // 2, width), rows * width, g.shape[1:]
    flat = g.reshape(N_CHIPS, -1)
    return _exchange_operand(flat), flat.shape[1], g.shape[1:]


def from_operand(name, r, n, shape):
    return r if name == NATURAL else r.reshape(-1)[:n].reshape(shape)


def misc_operand(small, loss_tile):
    flats, layout = [], []
    for name, layer, g in small + [("loss", 0, loss_tile)]:
        if name in SHARDED:
            flat, shape = g.reshape(N_CHIPS, -1), g.shape[1:]
        else:
            flat, shape = jnp.broadcast_to(g.reshape(1, -1), (N_CHIPS, g.size)), g.shape
        flats.append(flat)
        layout.append((name, layer, flat.shape[1], shape))
    return _exchange_operand(jnp.concatenate(flats, axis=1)), layout


def _as2d(a):
    return a.reshape(-1, a.shape[-1])


def kernel(x, positions, ffn1_norm, ffn1_w_gate, ffn1_w_up, ffn1_w_down, mix_norm, ffn2_norm, ffn2_w_gate, ffn2_w_up, ffn2_w_down, hyb_w_in, gdn_conv, gdn_a_log, gdn_dt_bias, gdn_out_norm, pool_w, pool_scale, hyb_w_out, mla_w_in, mla_q_norm, mla_kv_norm, mla_w_q_up, mla_w_kv_up, mla_q_head_norm, mla_k_head_norm, mla_w_out, loss_target, m_ffn1_norm, m_ffn1_w_gate, m_ffn1_w_up, m_ffn1_w_down, m_mix_norm, m_ffn2_norm, m_ffn2_w_gate, m_ffn2_w_up, m_ffn2_w_down, m_hyb_w_in, m_gdn_conv, m_gdn_a_log, m_gdn_dt_bias, m_gdn_out_norm, m_pool_w, m_pool_scale, m_hyb_w_out, m_mla_w_in, m_mla_q_norm, m_mla_kv_norm, m_mla_w_q_up, m_mla_w_kv_up, m_mla_q_head_norm, m_mla_k_head_norm, m_mla_w_out, v_ffn1_norm, v_ffn1_w_gate, v_ffn1_w_up, v_ffn1_w_down, v_mix_norm, v_ffn2_norm, v_ffn2_w_gate, v_ffn2_w_up, v_ffn2_w_down, v_hyb_w_in, v_gdn_conv, v_gdn_a_log, v_gdn_dt_bias, v_gdn_out_norm, v_pool_w, v_pool_scale, v_hyb_w_out, v_mla_w_in, v_mla_q_norm, v_mla_kv_norm, v_mla_w_q_up, v_mla_w_kv_up, v_mla_q_head_norm, v_mla_k_head_norm, v_mla_w_out):
    given = dict(locals())
    w = {n: given[n] for n in WEIGHTS}
    moments_m = {n: given["m_" + n] for n in WEIGHTS}
    moments_v = {n: given["v_" + n] for n in WEIGHTS}
    t = x.shape[1]
    xs = x.reshape(t, D_MODEL)
    full = gather_weights(w)
    n_even = hyb_w_in.shape[0]
    n_odd = mla_w_in.shape[0]

    _, sign, inv_freq = _rope_consts()
    cos, sin = rope_tables(positions.reshape(t, 1), inv_freq, sign)

    def ffn_args(which, layer):
        return (w[f"{which}_norm"][layer][None], full[f"{which}_w_gate"][layer], full[f"{which}_w_up"][layer],
                full[f"{which}_w_down"][layer])

    w_in_groups = [regroup_w_in(full[NATURAL], i) for i in range(n_even)]

    def hyb_args(i):
        w_qkvz, w_ab, w_u = w_in_groups[i]
        return (w["mix_norm"][2 * i][None], w_qkvz, w_ab, w_u,
                jnp.pad(full["gdn_conv"][i], ((0, HALO - CONV_K), (0, 0))), jnp.repeat(w["gdn_a_log"][i], HEAD_W)[None],
                jnp.repeat(w["gdn_dt_bias"][i], HEAD_W)[None], w["gdn_out_norm"][i][None], full["pool_w"][i],
                w["pool_scale"][i][None], full["hyb_w_out"][i])

    def mla_args(i):
        w_in = jnp.pad(full["mla_w_in"][i], ((0, 0), (0, ODD_IN_PAD - ODD_IN)))
        w_q = jnp.pad(full["mla_w_q_up"][i].reshape(LORA, MLA_HEADS, QK_HEAD),
                      ((0, 0), (0, 0), (0, 2 * HEAD_W - QK_HEAD))).reshape(LORA, MLA_HEADS * 2 * HEAD_W)
        return (cos, sin, w["mix_norm"][2 * i + 1][None], w_in, full["mla_q_norm"][i][None], full["mla_kv_norm"][i][None],
                w_q, full["mla_w_kv_up"][i], w["mla_q_head_norm"][i], w["mla_k_head_norm"][i], full["mla_w_out"][i])

    saved = []
    h = xs
    for layer in range(DEPTH):
        i = layer // 2
        h, s1 = ffn_fwd(h, *ffn_args("ffn1", layer), f"l{layer}_ffn1")
        if layer % 2 == 0:
            h, s2 = hyb_fwd(h, *hyb_args(i), f"l{layer}_hyb")
        else:
            h, s2 = mla_fwd(h, *mla_args(i), f"l{layer}_mla")
        h, s3 = ffn_fwd(h, *ffn_args("ffn2", layer), f"l{layer}_ffn2")
        saved.append((s1, s2, s3))

    dh, loss_tile = loss_head(h, loss_target.reshape(t, D_MODEL))

    small, done, halves = [], [], []

    def begin(group, tag):
        recs, ops = [], []
        for name, idx, g in group:
            if name in SHARDED and not (name.endswith(("w_gate", "w_up")) or name == NATURAL):
                g = shard_major(g, SHARDED[name] - 1)
            if is_own_operand(name, g):
                op, n, shape = grad_operand(name, g)
                recs.append((name, idx, n, shape))
                ops.append(op)
            else:
                small.append((name, idx, g))
        return recs, pair_reduce(ops, tag)

    def finish(recs, sums, ys, tag):
        done.extend(recs)
        halves.extend(chip_reduce(sums, ys, tag))

    waiting = ([], [])
    for layer in reversed(range(DEPTH)):
        i = layer // 2
        s1, s2, s3 = saved[layer]
        dh, dg, dwg, dwu, dwd = ffn_bwd(dh, s3, *ffn_args("ffn2", layer), f"l{layer}_ffn2")
        recs, sums = begin([("ffn2_norm", layer, dg[0]), ("ffn2_w_gate", layer, dwg), ("ffn2_w_up", layer, dwu),
                            ("ffn2_w_down", layer, dwd)], f"l{layer}a")
        recs, sums = waiting[0] + recs, waiting[1] + sums
        if layer % 2 == 0:
            dh, g, ys = hyb_bwd(dh, s2, *hyb_args(i), f"l{layer}_hyb", carry=sums)
            dmix, dw_qkvz, dw_ab, dw_u, dconv, da_log, ddt, dog, dpw, dps, dwo = g
            group = [("hyb_w_in", i, regroup_dw_in(dw_qkvz, dw_ab, dw_u)), ("gdn_conv", i, dconv[:CONV_K]),
                     ("gdn_a_log", i, da_log.reshape(GDN_HEADS, HEAD_W).sum(axis=1)),
                     ("gdn_dt_bias", i, ddt.reshape(GDN_HEADS, HEAD_W).sum(axis=1)), ("gdn_out_norm", i, dog[0]),
                     ("pool_w", i, dpw), ("pool_scale", i, dps[0]), ("hyb_w_out", i, dwo)]
        else:
            dh, g, ys = mla_bwd(dh, s2, *mla_args(i), f"l{layer}_mla", carry=sums)
            dmix, dw_in, dqg, dkvg, dwq, dwkv, dqh, dkh, dwo = g
            dwq = dwq.reshape(LORA, MLA_HEADS, 2 * HEAD_W)[:, :, :QK_HEAD].reshape(LORA, -1)
            group = [("mla_w_in", i, dw_in[:, :ODD_IN]), ("mla_q_norm", i, dqg[0]), ("mla_kv_norm", i, dkvg[0]),
                     ("mla_w_q_up", i, dwq), ("mla_w_kv_up", i, dwkv), ("mla_q_head_norm", i, dqh),
                     ("mla_k_head_norm", i, dkh), ("mla_w_out", i, dwo)]
        finish(recs, sums, ys, f"l{layer}")
        dh, dg, dwg, dwu, dwd = ffn_bwd(dh, s1, *ffn_args("ffn1", layer), f"l{layer}_ffn1")
        group += [("mix_norm", layer, dmix[0]), ("ffn1_norm", layer, dg[0]), ("ffn1_w_gate", layer, dwg),
                  ("ffn1_w_up", layer, dwu), ("ffn1_w_down", layer, dwd)]
        waiting = begin(group, f"l{layer}b")

    misc, layout = misc_operand(small, loss_tile)
    recs, sums = waiting[0] + [("misc", 0, 0, None)], waiting[1] + pair_reduce([misc], "misc")
    finish(recs, sums, chip_scatter(sums, "chip_scatter_tail"), "tail")
    wholes = pair_finish(halves, "grads")

    n_layers = {n: DEPTH if n.startswith(("ffn", "mix")) else (n_even if n in EVEN_ONLY else n_odd) for n in WEIGHTS}
    per_layer = {n: [None] * n_layers[n] for n in WEIGHTS}
    loss = None
    for (name, idx, n, shape), r in zip(done, wholes):
        if name != "misc":
            per_layer[name][idx] = from_operand(name, r, n, shape)
            continue
        flat, off = r.reshape(-1), 0
        for small_name, small_idx, size, small_shape in layout:
            piece = flat[off:off + size].reshape(small_shape)
            off += size
            if small_name == "loss":
                loss = piece[0, 0]
            else:
                per_layer[small_name][small_idx] = piece
    grads = {n: jnp.stack(per_layer[n]) for n in WEIGHTS}

    deltas, new_m, new_v = {}, {}, {}
    for n in WEIGHTS:
        d2, m2, v2 = adamw(_as2d(w[n]), _as2d(grads[n]), _as2d(moments_m[n]), _as2d(moments_v[n]), f"adamw_{n}")
        deltas[n], new_m[n], new_v[n] = d2.reshape(w[n].shape), m2.reshape(w[n].shape), v2.reshape(w[n].shape)
    return (loss, dh.reshape(x.shape), *[grads[n] for n in WEIGHTS], *[deltas[n] for n in WEIGHTS],
            *[new_m[n] for n in WEIGHTS], *[new_v[n] for n in WEIGHTS])
```
